```python
import math
import jax, jax.numpy as jnp
from jax import lax
import numpy as np

D_MODEL = 2048
BATCH = 4
SEQ = 4096
DEPTH = 2

GRID_W = 64
CTX_LEN = 256
EPS = 1e-6
ROPE_BASE = 10000.0
Q_BLOCK = 128
DIFF_DH = 64
DIFF_HEADS = (D_MODEL // 2) // (2 * DIFF_DH)
DIFF_DV = 2 * DIFF_DH
DIFF_QK_W = DIFF_HEADS * 2 * DIFF_DH
DIFF_V_W = DIFF_HEADS * DIFF_DV
CONV_CH = D_MODEL // 2
CONV_W = 31
HY_IN = 2 * DIFF_QK_W + DIFF_V_W + 2 * CONV_CH
HY_OUT = DIFF_V_W + CONV_CH
ML_HEADS = 8
ML_DV = D_MODEL // ML_HEADS
ML_DK = ML_DV // 2
ML_QK_W = ML_HEADS * ML_DK
ML_V_W = ML_HEADS * ML_DV
ML_GATES = 4 * ML_HEADS
ML_IN = 2 * ML_QK_W + ML_V_W + ML_GATES + ML_V_W
ML_CONV_W = 3
ML_CHUNK = 64
D_FF = 4 * D_MODEL

kernel_name = "hybrid_diffattn_conformer_mlstm_dit"

F32 = jnp.float32


def rms_norm(x, g):
    xf = x.astype(F32)
    y = xf * lax.rsqrt(jnp.mean(jnp.square(xf), axis=-1, keepdims=True) + EPS)
    return (y * g.astype(F32)).astype(x.dtype)


def layer_norm(x, g, b):
    xf = x.astype(F32)
    mu = jnp.mean(xf, axis=-1, keepdims=True)
    var = jnp.mean(jnp.square(xf - mu), axis=-1, keepdims=True)
    return ((xf - mu) * lax.rsqrt(var + EPS) * g.astype(F32) + b.astype(F32)).astype(x.dtype)


def modulate(h, shift, scale):
    return h * (1.0 + scale) + shift


def dwconv(x, w, b):
    width = w.shape[0]
    pad = width // 2
    y = lax.conv_general_dilated(x, w[:, None, :], window_strides=(1,), padding=[(pad, width - 1 - pad)],
                                 dimension_numbers=("NWC", "WIO", "NWC"), feature_group_count=x.shape[-1])
    return y + b


def axial_rope_tables(T):
    rows = T // GRID_W
    r = jnp.broadcast_to(jnp.arange(rows)[:, None], (rows, GRID_W)).reshape(T).astype(F32)
    col = jnp.broadcast_to(jnp.arange(GRID_W)[None, :], (rows, GRID_W)).reshape(T).astype(F32)
    n_freq = DIFF_DH // 4
    inv = ROPE_BASE ** (-jnp.arange(n_freq, dtype=F32) / n_freq)
    ar = r[:, None] * inv
    ac = col[:, None] * inv
    ang = jnp.concatenate([ar, ar, ac, ac], axis=-1)
    return jnp.cos(ang), jnp.sin(ang)


def apply_axial_rope(x, cos, sin):
    q = DIFF_DH // 4
    x1, x2, x3, x4 = x[..., :q], x[..., q:2 * q], x[..., 2 * q:3 * q], x[..., 3 * q:]
    rot = jnp.concatenate([-x2, x1, -x4, x3], axis=-1)
    T = x.shape[1]
    c = cos.reshape(T, 1, 1, DIFF_DH).astype(x.dtype)
    s = sin.reshape(T, 1, 1, DIFF_DH).astype(x.dtype)
    return x * c + rot * s


def diff_attn_block(q, k, v, lam):
    s = jnp.einsum("bqhcd,bkhcd->bhcqk", q, k, preferred_element_type=F32) * (DIFF_DH ** -0.5)
    p = jax.nn.softmax(s, axis=-1)
    a = p[:, :, 0] - lam * p[:, :, 1]
    return jnp.einsum("bhqk,bkhe->bqhe", a.astype(v.dtype), v)


def conformer_conv(u, conv_w, conv_b, ln_g, ln_b):
    a, g = jnp.split(u, 2, axis=-1)
    y = dwconv(a * jax.nn.sigmoid(g), conv_w, conv_b)
    return jax.nn.silu(layer_norm(y, ln_g, ln_b))


def hybrid_mixer(hl, hc, w_in, w_out, lq1, lk1, lq2, lk2, subln_g, conv_w, conv_b, ln_g, ln_b, lam_init, need_ctx):
    Bsz, T, _ = hl.shape
    Cn = hc.shape[1]
    lam = (jnp.exp(jnp.sum(lq1.astype(F32) * lk1.astype(F32)))
           - jnp.exp(jnp.sum(lq2.astype(F32) * lk2.astype(F32))) + lam_init)

    def split_qkvu(z, n):
        q = z[..., :DIFF_QK_W].reshape(Bsz, n, DIFF_HEADS, 2, DIFF_DH)
        k = z[..., DIFF_QK_W:2 * DIFF_QK_W].reshape(Bsz, n, DIFF_HEADS, 2, DIFF_DH)
        v = z[..., 2 * DIFF_QK_W:2 * DIFF_QK_W + DIFF_V_W].reshape(Bsz, n, DIFF_HEADS, DIFF_DV)
        u = z[..., 2 * DIFF_QK_W + DIFF_V_W:]
        return q, k, v, u

    def finish(attn, u):
        n = attn.shape[1]
        attn = (rms_norm(attn, subln_g) * (1.0 - lam_init)).reshape(Bsz, n, DIFF_V_W)
        conv = conformer_conv(u, conv_w, conv_b, ln_g, ln_b)
        return jnp.concatenate([attn, conv], axis=-1) @ w_out

    ql, kl, vl, ul = split_qkvu(hl @ w_in, T)
    cos, sin = axial_rope_tables(T)
    ql = apply_axial_rope(ql, cos, sin)
    kl = apply_axial_rope(kl, cos, sin)

    if need_ctx:
        qc, kc, vc, uc = split_qkvu(hc @ w_in, Cn)
    else:
        zkv = hc @ w_in[:, DIFF_QK_W:2 * DIFF_QK_W + DIFF_V_W]
        kc = zkv[..., :DIFF_QK_W].reshape(Bsz, Cn, DIFF_HEADS, 2, DIFF_DH)
        vc = zkv[..., DIFF_QK_W:].reshape(Bsz, Cn, DIFF_HEADS, DIFF_DV)

    k_all = jnp.concatenate([kc, kl], axis=1)
    v_all = jnp.concatenate([vc, vl], axis=1)
    nb = T // Q_BLOCK
    qb = jnp.moveaxis(ql.reshape(Bsz, nb, Q_BLOCK, DIFF_HEADS, 2, DIFF_DH), 1, 0)
    ob = lax.map(lambda q: diff_attn_block(q, k_all, v_all, lam), qb)
    attn_l = jnp.moveaxis(ob, 0, 1).reshape(Bsz, T, DIFF_HEADS, DIFF_DV)
    y_lat = finish(attn_l, ul)

    y_ctx = None
    if need_ctx:
        attn_c = diff_attn_block(qc, kc, vc, lam)
        y_ctx = finish(attn_c, uc)
    return y_lat, y_ctx


def _to_chunks(a):
    Bsz, T, H = a.shape[:3]
    a = a.reshape(Bsz, T // ML_CHUNK, ML_CHUNK, H, *a.shape[3:])
    return jnp.moveaxis(a, (1, 3), (0, 2))


def mlstm_scan(q, k, v, ig, lf, state0):
    Bsz, T, H, _ = q.shape
    dv = v.shape[-1]
    tril = jnp.tril(jnp.ones((ML_CHUNK, ML_CHUNK), dtype=bool))
    xs = tuple(_to_chunks(a.astype(F32)) for a in (q, k, v, ig, lf))

    def step(carry, inp):
        C, n, m = carry
        qc, kc, vc, ic, fc = inp
        b = jnp.cumsum(fc, axis=-1)
        g = b + m[..., None]
        dlog = b[..., :, None] - b[..., None, :] + ic[..., None, :]
        dlog = jnp.where(tril, dlog, -jnp.inf)
        m_row = jnp.maximum(g, jnp.max(dlog, axis=-1))
        inter = jnp.exp(g - m_row)
        s = jnp.einsum("bhid,bhjd->bhij", qc, kc) * jnp.exp(dlog - m_row[..., None])
        num = inter[..., None] * jnp.einsum("bhid,bhde->bhie", qc, C) + jnp.einsum("bhij,bhje->bhie", s, vc)
        den = inter * jnp.einsum("bhid,bhd->bhi", qc, n) + jnp.sum(s, axis=-1)
        h = num / jnp.maximum(jnp.abs(den), jnp.exp(-m_row))[..., None]
        b_last = b[..., -1]
        wlog = b_last[..., None] - b + ic
        m_new = jnp.maximum(b_last + m, jnp.max(wlog, axis=-1))
        decay = jnp.exp(b_last + m - m_new)
        w = jnp.exp(wlog - m_new[..., None])
        C_new = decay[..., None, None] * C + jnp.einsum("bhjd,bhje->bhde", kc * w[..., None], vc)
        n_new = decay[..., None] * n + jnp.einsum("bhj,bhjd->bhd", w, kc)
        return (C_new, n_new, m_new), h

    state, hs = lax.scan(step, state0, xs)
    h = jnp.moveaxis(hs, (0, 2), (1, 3)).reshape(Bsz, T, H, dv)
    return h, state


def mlstm_mixer(hl, hc, w_in, conv_w, conv_b, gate_b, norm_g, w_out, need_ctx):
    Bsz = hl.shape[0]

    def project(h, with_o):
        n = h.shape[1]
        z = h @ (w_in if with_o else w_in[:, :ML_IN - ML_V_W])
        qk = jax.nn.silu(dwconv(z[..., :2 * ML_QK_W], conv_w, conv_b))
        q = qk[..., :ML_QK_W].reshape(Bsz, n, ML_HEADS, ML_DK)
        k = qk[..., ML_QK_W:].reshape(Bsz, n, ML_HEADS, ML_DK) * (ML_DK ** -0.5)
        v = z[..., 2 * ML_QK_W:2 * ML_QK_W + ML_V_W].reshape(Bsz, n, ML_HEADS, ML_DV)
        gates = (z[..., 2 * ML_QK_W + ML_V_W:2 * ML_QK_W + ML_V_W + ML_GATES] + gate_b).astype(F32)
        i_f, f_f, i_b, f_b = jnp.split(gates, 4, axis=-1)
        o = z[..., ML_IN - ML_V_W:] if with_o else None
        return q, k, v, (i_f, jax.nn.log_sigmoid(f_f)), (i_b, jax.nn.log_sigmoid(f_b)), o

    def flip(a):
        return jnp.flip(a, axis=1)

    zero = (jnp.zeros((Bsz, ML_HEADS, ML_DK, ML_DV), F32), jnp.zeros((Bsz, ML_HEADS, ML_DK), F32),
            jnp.zeros((Bsz, ML_HEADS), F32))

    qc, kc, vc, gfc, gbc, oc = project(hc, need_ctx)
    hc_f, st_f = mlstm_scan(qc, kc, vc, gfc[0], gfc[1], zero)
    hc_b, st_b = mlstm_scan(flip(qc), flip(kc), flip(vc), flip(gbc[0]), flip(gbc[1]), zero)

    ql, kl, vl, gfl, gbl, ol = project(hl, True)
    hl_f, _ = mlstm_scan(ql, kl, vl, gfl[0], gfl[1], st_f)
    hl_b, _ = mlstm_scan(flip(ql), flip(kl), flip(vl), flip(gbl[0]), flip(gbl[1]), st_b)

    def finish(h, o):
        n = h.shape[1]
        h = rms_norm(h.astype(o.dtype), norm_g).reshape(Bsz, n, ML_V_W)
        return (h * jax.nn.sigmoid(o)) @ w_out

    y_lat = finish(hl_f + flip(hl_b), ol)
    y_ctx = finish(hc_f + flip(hc_b), oc) if need_ctx else None
    return y_lat, y_ctx


def sq_relu_mlp(h, w1, w2):
    return jnp.square(jax.nn.relu(h @ w1)) @ w2


def setup_inputs(seed: int = 0) -> dict:
    key = jax.random.key(seed)
    ks = iter(jax.random.split(key, 64))
    n_even = (DEPTH + 1) // 2
    n_odd = DEPTH // 2

    def nrm(shape, scale):
        return jax.random.normal(next(ks), shape, F32) * scale

    def gain(shape):
        return 1.0 + nrm(shape, 0.05)

    f_bias = jnp.broadcast_to(jnp.linspace(3.0, 6.0, ML_HEADS, dtype=F32), (n_odd, ML_HEADS))
    ml_gate_b = jnp.concatenate([nrm((n_odd, ML_HEADS), 0.1), f_bias + nrm((n_odd, ML_HEADS), 0.1),
                                 nrm((n_odd, ML_HEADS), 0.1), f_bias + nrm((n_odd, ML_HEADS), 0.1)], axis=-1)
    return {
        "x": nrm((BATCH, SEQ, D_MODEL), 1.0),
        "c": nrm((BATCH, D_MODEL), 1.0),
        "ctx": nrm((BATCH, CTX_LEN, D_MODEL), 1.0),
        "c_ctx": nrm((D_MODEL,), 1.0),
        "ada_w": nrm((DEPTH, D_MODEL, 6 * D_MODEL), 0.5 * D_MODEL ** -0.5),
        "ada_b": nrm((DEPTH, 6 * D_MODEL), 0.02),
        "g_pre_mix": gain((DEPTH, D_MODEL)),
        "g_post_mix": gain((DEPTH, D_MODEL)),
        "g_pre_ffn": gain((DEPTH, D_MODEL)),
        "g_post_ffn": gain((DEPTH, D_MODEL)),
        "ffn_w1": nrm((DEPTH, D_MODEL, D_FF), D_MODEL ** -0.5),
        "ffn_w2": nrm((DEPTH, D_FF, D_MODEL), D_FF ** -0.5),
        "hy_w_in": nrm((n_even, D_MODEL, HY_IN), D_MODEL ** -0.5),
        "hy_w_out": nrm((n_even, HY_OUT, D_MODEL), HY_OUT ** -0.5),
        "diff_lq1": nrm((n_even, DIFF_DH), 0.1),
        "diff_lk1": nrm((n_even, DIFF_DH), 0.1),
        "diff_lq2": nrm((n_even, DIFF_DH), 0.1),
        "diff_lk2": nrm((n_even, DIFF_DH), 0.1),
        "diff_subln_g": gain((n_even, DIFF_DV)),
        "conv_w": nrm((n_even, CONV_W, CONV_CH), CONV_W ** -0.5),
        "conv_b": nrm((n_even, CONV_CH), 0.02),
        "conv_ln_g": gain((n_even, CONV_CH)),
        "conv_ln_b": nrm((n_even, CONV_CH), 0.02),
        "ml_w_in": nrm((n_odd, D_MODEL, ML_IN), D_MODEL ** -0.5),
        "ml_conv_w": nrm((n_odd, ML_CONV_W, 2 * ML_QK_W), ML_CONV_W ** -0.5),
        "ml_conv_b": nrm((n_odd, 2 * ML_QK_W), 0.02),
        "ml_gate_b": ml_gate_b,
        "ml_norm_g": gain((n_odd, ML_HEADS, ML_DV)),
        "ml_w_out": nrm((n_odd, ML_V_W, D_MODEL), ML_V_W ** -0.5),
    }


def reference(x, c, ctx, c_ctx, ada_w, ada_b, g_pre_mix, g_post_mix, g_pre_ffn, g_post_ffn, ffn_w1, ffn_w2,
              hy_w_in, hy_w_out, diff_lq1, diff_lk1, diff_lq2, diff_lk2, diff_subln_g, conv_w, conv_b,
              conv_ln_g, conv_ln_b, ml_w_in, ml_conv_w, ml_conv_b, ml_gate_b, ml_norm_g, ml_w_out):
    xl, xc = x, ctx
    for l in range(DEPTH):
        need_ctx = l < DEPTH - 1
        mod_l = jnp.split((jax.nn.silu(c) @ ada_w[l] + ada_b[l])[:, None, :], 6, axis=-1)
        mod_c = jnp.split(jax.nn.silu(c_ctx) @ ada_w[l] + ada_b[l], 6, axis=-1)
        hl = modulate(rms_norm(xl, g_pre_mix[l]), mod_l[0], mod_l[1])
        hc = modulate(rms_norm(xc, g_pre_mix[l]), mod_c[0], mod_c[1])
        if l % 2 == 0:
            e = l // 2
            lam_init = 0.8 - 0.6 * math.exp(-0.3 * l)
            yl, yc = hybrid_mixer(hl, hc, hy_w_in[e], hy_w_out[e], diff_lq1[e], diff_lk1[e], diff_lq2[e],
                                  diff_lk2[e], diff_subln_g[e], conv_w[e], conv_b[e], conv_ln_g[e], conv_ln_b[e],
                                  lam_init, need_ctx)
        else:
            o = l // 2
            yl, yc = mlstm_mixer(hl, hc, ml_w_in[o], ml_conv_w[o], ml_conv_b[o], ml_gate_b[o], ml_norm_g[o],
                                 ml_w_out[o], need_ctx)
        xl = xl + mod_l[2] * rms_norm(yl, g_post_mix[l])
        hl = modulate(rms_norm(xl, g_pre_ffn[l]), mod_l[3], mod_l[4])
        xl = xl + mod_l[5] * rms_norm(sq_relu_mlp(hl, ffn_w1[l], ffn_w2[l]), g_post_ffn[l])
        if need_ctx:
            xc = xc + mod_c[2] * rms_norm(yc, g_post_mix[l])
            hc = modulate(rms_norm(xc, g_pre_ffn[l]), mod_c[3], mod_c[4])
            xc = xc + mod_c[5] * rms_norm(sq_relu_mlp(hc, ffn_w1[l], ffn_w2[l]), g_post_ffn[l])
    return xl
```

```python
import functools
import math

import jax
import jax.numpy as jnp
from jax import lax
from jax.experimental import pallas as pl
from jax.experimental.pallas import tpu as pltpu

F32 = jnp.float32
BF16 = jnp.bfloat16

EPS = 1e-6
ROPE_BASE = 10000.0
GRID_W = 64
LANES = 128
HALO = 16
DIFF_DH = 64
DIFF_HEADS = 8
DIFF_DV = 128
CONV_W = 31
ML_HEADS = 8
ML_DK = 128
ML_DV = 256
ML_CONV_W = 3
SCAN_CHUNK = 256
VMEM_LIMIT = 56 * 1024 * 1024


def _params(sem):
    return pltpu.CompilerParams(dimension_semantics=sem, vmem_limit_bytes=VMEM_LIMIT)


def _sigmoid(x):
    return 1.0 / (1.0 + jnp.exp(-x))


def _ada_kernel(c_ref, w_ref, b_ref, o_ref):
    c = c_ref[...]
    s = (c * _sigmoid(c)).astype(BF16)
    o_ref[0] = jnp.dot(s, w_ref[0].astype(BF16), preferred_element_type=F32) + b_ref[0]


def _ada(cvec, ada_w, ada_b):
    depth, d, n = ada_w.shape
    tn = 1024
    return pl.pallas_call(
        _ada_kernel,
        grid=(depth, n // tn),
        in_specs=[pl.BlockSpec((8, d), lambda l, j: (0, 0)),
                  pl.BlockSpec((1, d, tn), lambda l, j: (l, 0, j)),
                  pl.BlockSpec((1, 1, tn), lambda l, j: (l, 0, j))],
        out_specs=pl.BlockSpec((1, 8, tn), lambda l, j: (l, 0, j)),
        out_shape=jax.ShapeDtypeStruct((depth, 8, n), F32),
        compiler_params=_params(("parallel", "parallel")),
        name="ada_ln",
    )(cvec, ada_w, ada_b.reshape(depth, 1, n))


def _prenorm(x, g, shift, scale):
    r = lax.rsqrt(jnp.mean(x * x, axis=-1, keepdims=True) + EPS)
    return x * r * (g * (1.0 + scale)) + shift


def _prenorm_mm_kernel(*refs, rope_blocks, tn):
    if rope_blocks:
        x_ref, g_ref, sh_ref, sc_ref, w_ref, cos_ref, sina_ref, sinb_ref, o_ref, h_ref = refs
    else:
        x_ref, g_ref, sh_ref, sc_ref, w_ref, o_ref, h_ref = refs
    j = pl.program_id(1)

    @pl.when(j == 0)
    def _():
        h_ref[...] = _prenorm(x_ref[...], g_ref[...], sh_ref[0], sc_ref[0]).astype(BF16)

    acc = jnp.dot(h_ref[...], w_ref[...], preferred_element_type=F32)

    if rope_blocks:
        @pl.when(j < rope_blocks)
        def _():
            cos = cos_ref[...]
            sina = sina_ref[...]
            sinb = sinb_ref[...]
            for c in range(tn // LANES):
                xs = acc[:, c * LANES:(c + 1) * LANES]
                rot = pltpu.roll(xs, LANES - 16, 1) * sina + pltpu.roll(xs, 16, 1) * sinb
                o_ref[:, c * LANES:(c + 1) * LANES] = (xs * cos + rot).astype(o_ref.dtype)

        @pl.when(j >= rope_blocks)
        def _():
            o_ref[...] = acc.astype(o_ref.dtype)
    else:
        o_ref[...] = acc.astype(o_ref.dtype)


def _prenorm_mm(x, g, shift, scale, w, n_out, *, tm, tn, rows_per_mod, out_dtype, rope=None, rope_cols=0, name):
    m, d = x.shape
    mod_map = lambda i, j: ((i * tm) // rows_per_mod, 0, 0)
    in_specs = [pl.BlockSpec((tm, d), lambda i, j: (i, 0)),
                pl.BlockSpec((1, d), lambda i, j: (0, 0)),
                pl.BlockSpec((1, 1, d), mod_map),
                pl.BlockSpec((1, 1, d), mod_map),
                pl.BlockSpec((d, tn), lambda i, j: (0, j))]
    args = [x, g.reshape(1, d), shift, scale, w]
    rope_blocks = 0
    if rope is not None:
        assert rope_cols % tn == 0
        rope_blocks = rope_cols // tn
        t_len = rope[0].shape[0]
        assert t_len % tm == 0
        nblk = t_len // tm
        for tab in rope:
            in_specs.append(pl.BlockSpec((tm, LANES), lambda i, j: (i % nblk, 0)))
            args.append(tab)
    return pl.pallas_call(
        functools.partial(_prenorm_mm_kernel, rope_blocks=rope_blocks, tn=tn),
        grid=(m // tm, n_out // tn),
        in_specs=in_specs,
        out_specs=pl.BlockSpec((tm, tn), lambda i, j: (i, j)),
        out_shape=jax.ShapeDtypeStruct((m, n_out), out_dtype),
        scratch_shapes=[pltpu.VMEM((tm, d), BF16)],
        compiler_params=_params(("parallel", "arbitrary")),
        name=name,
    )(*args)


def _rope_tables(t_len):
    rows = t_len // GRID_W
    r = jnp.repeat(jnp.arange(rows, dtype=F32), GRID_W)
    col = jnp.tile(jnp.arange(GRID_W, dtype=F32), rows)
    n_freq = DIFF_DH // 4
    inv = ROPE_BASE ** (-jnp.arange(n_freq, dtype=F32) / n_freq)
    ar = r[:, None] * inv
    ac = col[:, None] * inv
    ang = jnp.concatenate([ar, ar, ac, ac], axis=-1)
    cos = jnp.tile(jnp.cos(ang), (1, LANES // DIFF_DH))
    sin = jnp.tile(jnp.sin(ang), (1, LANES // DIFF_DH))
    first = (jnp.arange(LANES) % (2 * n_freq)) < n_freq
    return cos, jnp.where(first, -sin, 0.0), jnp.where(first, 0.0, sin)


def _attn_kernel(*refs, lam_init, two_sources):
    if two_sources:
        q_ref, k_ref, v_ref, kc_ref, vc_ref, lq1_ref, lk1_ref, lq2_ref, lk2_ref, g_ref, o_ref = refs
    else:
        q_ref, k_ref, v_ref, lq1_ref, lk1_ref, lq2_ref, lk2_ref, g_ref, o_ref = refs
    lam = (jnp.exp(jnp.sum(lq1_ref[...] * lk1_ref[...], axis=-1, keepdims=True))
           - jnp.exp(jnp.sum(lq2_ref[...] * lk2_ref[...], axis=-1, keepdims=True)) + lam_init)
    q = q_ref[0] * (DIFF_DH ** -0.5)
    lane = lax.broadcasted_iota(jnp.int32, q.shape, 1)
    zero = jnp.zeros_like(q)
    qs = (jnp.where(lane < DIFF_DH, q, zero), jnp.where(lane >= DIFF_DH, q, zero))
    dn = (((1,), (1,)), ((), ()))
    keys = [k_ref[0]] + ([kc_ref[0]] if two_sources else [])
    vals = [v_ref[0]] + ([vc_ref[0]] if two_sources else [])

    probs = []
    for c in range(2):
        s = [lax.dot_general(qs[c], k, dn, preferred_element_type=F32) for k in keys]
        m = functools.reduce(jnp.maximum, [jnp.max(x, axis=-1, keepdims=True) for x in s])
        e = [jnp.exp(x - m) for x in s]
        l = functools.reduce(jnp.add, [jnp.sum(x, axis=-1, keepdims=True) for x in e])
        probs.append((e, l))
    w1 = 1.0 / probs[0][1]
    w2 = lam / probs[1][1]
    o = None
    for idx, v in enumerate(vals):
        a = (probs[0][0][idx] * w1 - probs[1][0][idx] * w2).astype(BF16)
        part = jnp.dot(a, v, preferred_element_type=F32)
        o = part if o is None else o + part
    r = lax.rsqrt(jnp.mean(o * o, axis=-1, keepdims=True) + EPS)
    o_ref[0] = (o * r * g_ref[...] * (1.0 - lam_init)).astype(o_ref.dtype)


def _diff_attn(zq, zkv_extra, lams, subln_g, lam_init, *, tq):
    bsz, t_len, _ = zq.shape
    nh = DIFF_HEADS
    two = zkv_extra is not None
    in_specs = [pl.BlockSpec((1, tq, LANES), lambda b, h, i: (b, i, h)),
                pl.BlockSpec((1, t_len, LANES), lambda b, h, i: (b, 0, nh + h)),
                pl.BlockSpec((1, t_len, LANES), lambda b, h, i: (b, 0, 2 * nh + h))]
    args = [zq, zq, zq]
    if two:
        c_len = zkv_extra.shape[1]
        in_specs += [pl.BlockSpec((1, c_len, LANES), lambda b, h, i: (b, 0, nh + h)),
                     pl.BlockSpec((1, c_len, LANES), lambda b, h, i: (b, 0, 2 * nh + h))]
        args += [zkv_extra, zkv_extra]
    for v in lams:
        in_specs.append(pl.BlockSpec((1, DIFF_DH), lambda b, h, i: (0, 0)))
        args.append(v.reshape(1, DIFF_DH))
    in_specs.append(pl.BlockSpec((1, DIFF_DV), lambda b, h, i: (0, 0)))
    args.append(subln_g.reshape(1, DIFF_DV))
    return pl.pallas_call(
        functools.partial(_attn_kernel, lam_init=lam_init, two_sources=two),
        grid=(bsz, nh, t_len // tq),
        in_specs=in_specs,
        out_specs=pl.BlockSpec((1, tq, LANES), lambda b, h, i: (b, i, h)),
        out_shape=jax.ShapeDtypeStruct((bsz, t_len, nh * DIFF_DV), BF16),
        compiler_params=_params(("parallel", "parallel", "arbitrary")),
        name="diff_attn" + ("_lat" if two else "_ctx"),
    )(*args)


def _dwconv_into(y_ref, w_ref, b_ref, out_ref, *, tt, width, nch):
    pad = width // 2
    rc = 64
    for c0 in range(0, nch, LANES):
        for r0 in range(0, tt, rc):
            acc = jnp.broadcast_to(b_ref[:, c0:c0 + LANES], (rc, LANES))
            for w in range(width):
                start = r0 + HALO - pad + w
                acc = acc + y_ref[start:start + rc, c0:c0 + LANES] * w_ref[w:w + 1, c0:c0 + LANES]
            out_ref[r0:r0 + rc, c0:c0 + LANES] = acc


def _convmod_kernel(ap_ref, a_ref, an_ref, gp_ref, g_ref, gn_ref, cw_ref, cb_ref, lng_ref, lnb_ref, o_ref,
                    y_ref, c_ref, *, tt, nch):
    i = pl.program_id(1)
    n = pl.num_programs(1)

    def glu(a, g):
        return a.astype(F32) * _sigmoid(g.astype(F32))

    y_ref[0:HALO] = jnp.where(i > 0, glu(ap_ref[0], gp_ref[0]), 0.0)
    y_ref[HALO:HALO + tt] = glu(a_ref[0], g_ref[0])
    y_ref[HALO + tt:2 * HALO + tt] = jnp.where(i < n - 1, glu(an_ref[0], gn_ref[0]), 0.0)
    _dwconv_into(y_ref, cw_ref, cb_ref, c_ref, tt=tt, width=CONV_W, nch=nch)
    c = c_ref[...]
    mu = jnp.mean(c, axis=-1, keepdims=True)
    xc = c - mu
    var = jnp.mean(xc * xc, axis=-1, keepdims=True)
    y = xc * lax.rsqrt(var + EPS) * lng_ref[...] + lnb_ref[...]
    o_ref[0] = (y * _sigmoid(y)).astype(o_ref.dtype)


def _halo_specs(tt, t_len, width, colblk):
    per = tt // HALO
    last = t_len // HALO - 1
    return (pl.BlockSpec((1, HALO, width), lambda b, i: (b, jnp.maximum(i * per - 1, 0), colblk)),
            pl.BlockSpec((1, tt, width), lambda b, i: (b, i, colblk)),
            pl.BlockSpec((1, HALO, width), lambda b, i: (b, jnp.minimum((i + 1) * per, last), colblk)))


def _conv_module(z, conv_w, conv_b, ln_g, ln_b, *, tt):
    bsz, t_len, n = z.shape
    nch = conv_w.shape[1]
    a_blk = (n - 2 * nch) // nch
    vec = lambda: pl.BlockSpec((1, nch), lambda b, i: (0, 0))
    return pl.pallas_call(
        functools.partial(_convmod_kernel, tt=tt, nch=nch),
        grid=(bsz, t_len // tt),
        in_specs=[*_halo_specs(tt, t_len, nch, a_blk), *_halo_specs(tt, t_len, nch, a_blk + 1),
                  pl.BlockSpec((CONV_W, nch), lambda b, i: (0, 0)), vec(), vec(), vec()],
        out_specs=pl.BlockSpec((1, tt, nch), lambda b, i: (b, i, 0)),
        out_shape=jax.ShapeDtypeStruct((bsz, t_len, nch), BF16),
        scratch_shapes=[pltpu.VMEM((tt + 2 * HALO, nch), F32), pltpu.VMEM((tt, nch), F32)],
        compiler_params=_params(("parallel", "arbitrary")),
        name="conformer_conv",
    )(z, z, z, z, z, z, conv_w, conv_b.reshape(1, nch), ln_g.reshape(1, nch), ln_b.reshape(1, nch))


def _conv3_kernel(zp_ref, z_ref, zn_ref, cw_ref, cb_ref, o_ref, y_ref, c_ref, *, tt, nch, kscale):
    i = pl.program_id(1)
    n = pl.num_programs(1)
    y_ref[0:HALO] = jnp.where(i > 0, zp_ref[0].astype(F32), 0.0)
    y_ref[HALO:HALO + tt] = z_ref[0].astype(F32)
    y_ref[HALO + tt:2 * HALO + tt] = jnp.where(i < n - 1, zn_ref[0].astype(F32), 0.0)
    _dwconv_into(y_ref, cw_ref, cb_ref, c_ref, tt=tt, width=ML_CONV_W, nch=nch)
    c = c_ref[...]
    s = c * _sigmoid(c)
    half = nch // 2
    o_ref[0, :, 0:half] = s[:, 0:half].astype(o_ref.dtype)
    o_ref[0, :, half:nch] = (s[:, half:nch] * kscale).astype(o_ref.dtype)


def _conv3_silu(z, conv_w, conv_b, *, tt):
    bsz, t_len, _ = z.shape
    nch = conv_w.shape[1]
    return pl.pallas_call(
        functools.partial(_conv3_kernel, tt=tt, nch=nch, kscale=ML_DK ** -0.5),
        grid=(bsz, t_len // tt),
        in_specs=[*_halo_specs(tt, t_len, nch, 0),
                  pl.BlockSpec((ML_CONV_W, nch), lambda b, i: (0, 0)),
                  pl.BlockSpec((1, nch), lambda b, i: (0, 0))],
        out_specs=pl.BlockSpec((1, tt, nch), lambda b, i: (b, i, 0)),
        out_shape=jax.ShapeDtypeStruct((bsz, t_len, nch), BF16),
        scratch_shapes=[pltpu.VMEM((tt + 2 * HALO, nch), F32), pltpu.VMEM((tt, nch), F32)],
        compiler_params=_params(("parallel", "arbitrary")),
        name="ml_conv3_silu",
    )(z, z, z, conv_w, conv_b.reshape(1, nch))


def _post_residual(y, x, gate, gpost):
    r = lax.rsqrt(jnp.mean(y * y, axis=-1, keepdims=True) + EPS)
    return x + gate * (y * r * gpost)


def _hy_out_kernel(a_ref, c_ref, x_ref, gate_ref, gpost_ref, w_ref, o_ref):
    ka = a_ref.shape[1]
    y = (jnp.dot(a_ref[...], w_ref[0:ka, :], preferred_element_type=F32)
         + jnp.dot(c_ref[...], w_ref[ka:, :], preferred_element_type=F32))
    o_ref[...] = _post_residual(y, x_ref[...], gate_ref[0], gpost_ref[...])


def _hy_out(attn, conv, x, gate, gpost, w, *, tm, rows_per_mod):
    m, d = x.shape
    ka, kc = attn.shape[1], conv.shape[1]
    return pl.pallas_call(
        _hy_out_kernel,
        grid=(m // tm,),
        in_specs=[pl.BlockSpec((tm, ka), lambda i: (i, 0)),
                  pl.BlockSpec((tm, kc), lambda i: (i, 0)),
                  pl.BlockSpec((tm, d), lambda i: (i, 0)),
                  pl.BlockSpec((1, 1, d), lambda i: ((i * tm) // rows_per_mod, 0, 0)),
                  pl.BlockSpec((1, d), lambda i: (0, 0)),
                  pl.BlockSpec((ka + kc, d), lambda i: (0, 0))],
        out_specs=pl.BlockSpec((tm, d), lambda i: (i, 0)),
        out_shape=jax.ShapeDtypeStruct((m, d), F32),
        compiler_params=_params(("parallel",)),
        name="hybrid_out_proj",
    )(attn, conv, x, gate, gpost.reshape(1, d), w)


def _ml_out_kernel(hf_ref, hb_ref, og_ref, x_ref, ng_ref, gate_ref, gpost_ref, w_ref, o_ref):
    hsum = hf_ref[0] + hb_ref[0]
    parts = []
    for h in range(ML_HEADS):
        hh = hsum[:, h * ML_DV:(h + 1) * ML_DV]
        r = lax.rsqrt(jnp.mean(hh * hh, axis=-1, keepdims=True) + EPS)
        parts.append(hh * r * ng_ref[:, h * ML_DV:(h + 1) * ML_DV])
    hn = jnp.concatenate(parts, axis=1)
    a = (hn * _sigmoid(og_ref[...].astype(F32))).astype(BF16)
    y = jnp.dot(a, w_ref[...], preferred_element_type=F32)
    o_ref[...] = _post_residual(y, x_ref[...], gate_ref[0], gpost_ref[...])


def _ml_out(h2, z, o_blk, x, norm_g, gate, gpost, w, *, tm, rows_per_mod):
    m, d = x.shape
    kv = h2.shape[2]
    return pl.pallas_call(
        _ml_out_kernel,
        grid=(m // tm,),
        in_specs=[pl.BlockSpec((1, tm, kv), lambda i: (0, i, 0)),
                  pl.BlockSpec((1, tm, kv), lambda i: (1, i, 0)),
                  pl.BlockSpec((tm, kv), lambda i: (i, o_blk)),
                  pl.BlockSpec((tm, d), lambda i: (i, 0)),
                  pl.BlockSpec((1, kv), lambda i: (0, 0)),
                  pl.BlockSpec((1, 1, d), lambda i: ((i * tm) // rows_per_mod, 0, 0)),
                  pl.BlockSpec((1, d), lambda i: (0, 0)),
                  pl.BlockSpec((kv, d), lambda i: (0, 0))],
        out_specs=pl.BlockSpec((tm, d), lambda i: (i, 0)),
        out_shape=jax.ShapeDtypeStruct((m, d), F32),
        compiler_params=_params(("parallel",)),
        name="mlstm_out_proj",
    )(h2, h2, z, x, norm_g.reshape(1, kv), gate, gpost.reshape(1, d), w)


def _ffn_kernel(x_ref, g_ref, sh_ref, sc_ref, gate_ref, gpost_ref, w1_ref, w2_ref, o_ref, h_ref, acc_ref):
    f = pl.program_id(1)
    nf = pl.num_programs(1)

    @pl.when(f == 0)
    def _():
        h_ref[...] = _prenorm(x_ref[...], g_ref[...], sh_ref[0], sc_ref[0]).astype(BF16)

    t = jnp.maximum(jnp.dot(h_ref[...], w1_ref[...], preferred_element_type=F32), 0.0)
    p = jnp.dot((t * t).astype(BF16), w2_ref[...], preferred_element_type=F32)

    @pl.when(f == 0)
    def _():
        acc_ref[...] = p

    @pl.when(f > 0)
    def _():
        acc_ref[...] += p

    @pl.when(f == nf - 1)
    def _():
        o_ref[...] = _post_residual(acc_ref[...], x_ref[...], gate_ref[0], gpost_ref[...])


def _ffn(x, g, shift, scale, gate, gpost, w1, w2, *, tm, tf, rows_per_mod):
    m, d = x.shape
    dff = w1.shape[1]
    mod_map = lambda i, f: ((i * tm) // rows_per_mod, 0, 0)
    vec = lambda: pl.BlockSpec((1, d), lambda i, f: (0, 0))
    return pl.pallas_call(
        _ffn_kernel,
        grid=(m // tm, dff // tf),
        in_specs=[pl.BlockSpec((tm, d), lambda i, f: (i, 0)), vec(),
                  pl.BlockSpec((1, 1, d), mod_map), pl.BlockSpec((1, 1, d), mod_map),
                  pl.BlockSpec((1, 1, d), mod_map), vec(),
                  pl.BlockSpec((d, tf), lambda i, f: (0, f)),
                  pl.BlockSpec((tf, d), lambda i, f: (f, 0))],
        out_specs=pl.BlockSpec((tm, d), lambda i, f: (i, 0)),
        out_shape=jax.ShapeDtypeStruct((m, d), F32),
        scratch_shapes=[pltpu.VMEM((tm, d), BF16), pltpu.VMEM((tm, d), F32)],
        compiler_params=_params(("parallel", "arbitrary")),
        name="ffn",
    )(x, g.reshape(1, d), shift, scale, gate, gpost.reshape(1, d), w1, w2)


def _scan_kernel(ql_ref, kl_ref, vl_ref, gil_ref, gfl_ref, qc_ref, kc_ref, vc_ref, gic_ref, gfc_ref,
                 bi_ref, bf_ref, o_ref, c_ref, m_ref):
    L, H, DK, DV = SCAN_CHUNK, ML_HEADS, ML_DK, ML_DV
    fwd = pl.program_id(0) == 0
    s = pl.program_id(2)

    @pl.when(s == 0)
    def _():
        c_ref[...] = jnp.zeros_like(c_ref)
        m_ref[...] = jnp.zeros_like(m_ref)

    is_ctx = s == 0
    q = jnp.where(is_ctx, qc_ref[0], ql_ref[0])
    k = jnp.where(is_ctx, kc_ref[0], kl_ref[0])
    v = jnp.where(is_ctx, vc_ref[0], vl_ref[0])
    ipre = jnp.where(is_ctx, gic_ref[0, 0], gil_ref[0, 0]) + bi_ref[0]
    fpre = jnp.where(is_ctx, gfc_ref[0, 0], gfl_ref[0, 0]) + bf_ref[0]
    lf = jnp.minimum(fpre, 0.0) - jnp.log1p(jnp.exp(-jnp.abs(fpre)))

    row = lax.broadcasted_iota(jnp.int32, (L, L), 0)
    col = lax.broadcasted_iota(jnp.int32, (L, L), 1)
    delta = (row - col) * jnp.where(fwd, 1, -1)
    tri = jnp.where(delta <= 0, 1.0, 0.0).astype(F32)
    b_all = jnp.dot(lf, tri, precision=lax.Precision.HIGHEST, preferred_element_type=F32)
    r_all = ipre - b_all
    b_last_all = jnp.where(fwd, b_all[:, L - 1:L], b_all[:, 0:1])
    mask = delta >= 0
    ones_blk = (lax.broadcasted_iota(jnp.int32, (L, LANES), 1) == 0).astype(BF16)
    dn_t = (((1,), (1,)), ((), ()))

    for h in range(H):
        b_row = b_all[h:h + 1, :]
        r_row = r_all[h:h + 1, :]
        bcol = jnp.transpose(jnp.broadcast_to(b_row, (LANES, L)))
        bcol_l = jnp.concatenate([bcol] * (L // LANES), axis=1)
        dm = jnp.where(mask, bcol_l + r_row, -jnp.inf)
        m_loc = jnp.max(dm, axis=1, keepdims=True)
        m_prev = m_ref[h:h + 1, 0:1]
        g = bcol[:, 0:1] + m_prev
        m_row = jnp.maximum(g, m_loc)
        inter = jnp.exp(g - m_row)
        p = jnp.exp(dm - m_row)
        qh = q[:, h * DK:(h + 1) * DK]
        kh = k[:, h * DK:(h + 1) * DK]
        vext = jnp.concatenate([v[:, h * DV:(h + 1) * DV], ones_blk], axis=1)
        sm = (lax.dot_general(qh, kh, dn_t, preferred_element_type=F32) * p).astype(BF16)
        cst = c_ref[h]
        nd = (inter * jnp.dot(qh, cst.astype(BF16), preferred_element_type=F32)
              + jnp.dot(sm, vext, preferred_element_type=F32))
        den = jnp.maximum(jnp.abs(nd[:, DV:DV + 1]), jnp.exp(-m_row))
        o_ref[0, 0, :, h * DV:(h + 1) * DV] = nd[:, 0:DV] * (1.0 / den)

        b_last = b_last_all[h:h + 1, :]
        wlog = b_last + r_row
        m_new = jnp.maximum(b_last + m_prev, jnp.max(wlog, axis=1, keepdims=True))
        decay = jnp.exp(b_last + m_prev - m_new)
        kt = jnp.transpose(kh.astype(F32))
        ktw = (kt * jnp.exp(wlog - m_new)).astype(BF16)
        c_ref[h] = decay * cst + jnp.dot(ktw, vext, preferred_element_type=F32)
        m_ref[h:h + 1, :] = jnp.broadcast_to(m_new, (1, LANES))


def _mlstm_scan(qk_l, z_l, gates_l, qk_c, z_c, gates_c, gate_b):
    L, H = SCAN_CHUNK, ML_HEADS
    bsz, t_len, _ = qk_l.shape
    assert qk_c.shape[1] == L
    n_lat = t_len // L
    hk, hv = H * ML_DK, H * ML_DV

    def lat(d, s):
        return jnp.where(d == 0, jnp.maximum(s - 1, 0), n_lat - jnp.maximum(s, 1))

    in_specs = [
        pl.BlockSpec((1, L, hk), lambda d, b, s: (b, lat(d, s), 0)),
        pl.BlockSpec((1, L, hk), lambda d, b, s: (b, lat(d, s), 1)),
        pl.BlockSpec((1, L, hv), lambda d, b, s: (b, lat(d, s), 1)),
        pl.BlockSpec((1, 1, H, L), lambda d, b, s: (b, 2 * d, 0, lat(d, s))),
        pl.BlockSpec((1, 1, H, L), lambda d, b, s: (b, 2 * d + 1, 0, lat(d, s))),
        pl.BlockSpec((1, L, hk), lambda d, b, s: (b, 0, 0)),
        pl.BlockSpec((1, L, hk), lambda d, b, s: (b, 0, 1)),
        pl.BlockSpec((1, L, hv), lambda d, b, s: (b, 0, 1)),
        pl.BlockSpec((1, 1, H, L), lambda d, b, s: (b, 2 * d, 0, 0)),
        pl.BlockSpec((1, 1, H, L), lambda d, b, s: (b, 2 * d + 1, 0, 0)),
        pl.BlockSpec((1, H, 1), lambda d, b, s: (2 * d, 0, 0)),
        pl.BlockSpec((1, H, 1), lambda d, b, s: (2 * d + 1, 0, 0)),
    ]
    return pl.pallas_call(
        _scan_kernel,
        grid=(2, bsz, n_lat + 1),
        in_specs=in_specs,
        out_specs=pl.BlockSpec((1, 1, L, hv), lambda d, b, s: (d, b, lat(d, s), 0)),
        out_shape=jax.ShapeDtypeStruct((2, bsz, t_len, hv), F32),
        scratch_shapes=[pltpu.VMEM((H, ML_DK, ML_DV + LANES), F32), pltpu.VMEM((H, LANES), F32)],
        compiler_params=_params(("parallel", "parallel", "arbitrary")),
        name="mlstm_scan",
    )(qk_l, qk_l, z_l, gates_l, gates_l, qk_c, qk_c, z_c, gates_c, gates_c,
      gate_b.reshape(4, H, 1), gate_b.reshape(4, H, 1))


def kernel(x, c, ctx, c_ctx, ada_w, ada_b, g_pre_mix, g_post_mix, g_pre_ffn, g_post_ffn, ffn_w1, ffn_w2, hy_w_in, hy_w_out, diff_lq1, diff_lk1, diff_lq2, diff_lk2, diff_subln_g, conv_w, conv_b, conv_ln_g, conv_ln_b, ml_w_in, ml_conv_w, ml_conv_b, ml_gate_b, ml_norm_g, ml_w_out):
    bsz, t_len, d = x.shape
    c_len = ctx.shape[1]
    depth = ada_w.shape[0]
    assert depth == 2 and bsz <= 7 and c_len == SCAN_CHUNK
    ml = t_len * bsz
    mc = c_len * bsz

    cvec = jnp.concatenate([c, c_ctx[None, :], jnp.zeros((8 - bsz - 1, d), F32)], axis=0)
    mods = _ada(cvec, ada_w, ada_b)

    def mod_lat(l, k):
        return mods[l, :bsz, k * d:(k + 1) * d].reshape(bsz, 1, d)

    def mod_ctx(l, k):
        return mods[l, bsz:bsz + 1, k * d:(k + 1) * d].reshape(1, 1, d)

    xl = x.reshape(ml, d)
    xc = ctx.reshape(mc, d)
    w1 = ffn_w1.astype(BF16)
    w2 = ffn_w2.astype(BF16)

    l = 0
    lam_init = 0.8 - 0.6 * math.exp(-0.3 * l)
    w_in = hy_w_in[0].astype(BF16)
    n_in = w_in.shape[1]
    qk_w = 2 * DIFF_HEADS * 2 * DIFF_DH
    rope = _rope_tables(t_len)
    zl = _prenorm_mm(xl, g_pre_mix[l], mod_lat(l, 0), mod_lat(l, 1), w_in, n_in, tm=1024, tn=1024,
                     rows_per_mod=t_len, out_dtype=BF16, rope=rope, rope_cols=qk_w, name="hy_in_proj_lat")
    zc = _prenorm_mm(xc, g_pre_mix[l], mod_ctx(l, 0), mod_ctx(l, 1), w_in, n_in, tm=mc, tn=1024,
                     rows_per_mod=mc, out_dtype=BF16, name="hy_in_proj_ctx")
    zl3 = zl.reshape(bsz, t_len, n_in)
    zc3 = zc.reshape(bsz, c_len, n_in)
    lams = (diff_lq1[0], diff_lk1[0], diff_lq2[0], diff_lk2[0])
    attn_l = _diff_attn(zl3, zc3, lams, diff_subln_g[0], lam_init, tq=256)
    attn_c = _diff_attn(zc3, None, lams, diff_subln_g[0], lam_init, tq=c_len)
    conv_l = _conv_module(zl3, conv_w[0], conv_b[0], conv_ln_g[0], conv_ln_b[0], tt=256)
    conv_c = _conv_module(zc3, conv_w[0], conv_b[0], conv_ln_g[0], conv_ln_b[0], tt=c_len)
    w_out = hy_w_out[0].astype(BF16)
    xl = _hy_out(attn_l.reshape(ml, -1), conv_l.reshape(ml, -1), xl, mod_lat(l, 2), g_post_mix[l], w_out,
                 tm=512, rows_per_mod=t_len)
    xc = _hy_out(attn_c.reshape(mc, -1), conv_c.reshape(mc, -1), xc, mod_ctx(l, 2), g_post_mix[l], w_out,
                 tm=512, rows_per_mod=mc)
    xl = _ffn(xl, g_pre_ffn[l], mod_lat(l, 3), mod_lat(l, 4), mod_lat(l, 5), g_post_ffn[l], w1[l], w2[l],
              tm=512, tf=1024, rows_per_mod=t_len)
    xc = _ffn(xc, g_pre_ffn[l], mod_ctx(l, 3), mod_ctx(l, 4), mod_ctx(l, 5), g_post_ffn[l], w1[l], w2[l],
              tm=512, tf=1024, rows_per_mod=mc)

    l = 1
    qkv_w = 2 * ML_HEADS * ML_DK + ML_HEADS * ML_DV
    n_gates = 4 * ML_HEADS
    wm = ml_w_in[0]
    w_main = jnp.concatenate([wm[:, :qkv_w], wm[:, qkv_w + n_gates:]], axis=1).astype(BF16)
    w_gate = jnp.pad(wm[:, qkv_w:qkv_w + n_gates], ((0, 0), (0, LANES - n_gates))).astype(BF16)
    n_main = w_main.shape[1]
    zl = _prenorm_mm(xl, g_pre_mix[l], mod_lat(l, 0), mod_lat(l, 1), w_main, n_main, tm=1024, tn=1024,
                     rows_per_mod=t_len, out_dtype=BF16, name="ml_in_proj_lat")
    gl = _prenorm_mm(xl, g_pre_mix[l], mod_lat(l, 0), mod_lat(l, 1), w_gate, LANES, tm=1024, tn=LANES,
                     rows_per_mod=t_len, out_dtype=F32, name="ml_gate_proj_lat")
    zc = _prenorm_mm(xc, g_pre_mix[l], mod_ctx(l, 0), mod_ctx(l, 1), w_main, qkv_w, tm=mc, tn=1024,
                     rows_per_mod=mc, out_dtype=BF16, name="ml_in_proj_ctx")
    gc = _prenorm_mm(xc, g_pre_mix[l], mod_ctx(l, 0), mod_ctx(l, 1), w_gate, LANES, tm=mc, tn=LANES,
                     rows_per_mod=mc, out_dtype=F32, name="ml_gate_proj_ctx")
    zl3 = zl.reshape(bsz, t_len, n_main)
    zc3 = zc.reshape(bsz, c_len, qkv_w)
    qk_l = _conv3_silu(zl3, ml_conv_w[0], ml_conv_b[0], tt=256)
    qk_c = _conv3_silu(zc3, ml_conv_w[0], ml_conv_b[0], tt=c_len)

    def gates_t(gm, n):
        return gm[:, :n_gates].reshape(bsz, n, 4, ML_HEADS).transpose(0, 2, 3, 1)

    h2 = _mlstm_scan(qk_l, zl3, gates_t(gl, t_len), qk_c, zc3, gates_t(gc, c_len), ml_gate_b[0])
    xl = _ml_out(h2.reshape(2, ml, -1), zl, qkv_w // (ML_HEADS * ML_DV), xl, ml_norm_g[0], mod_lat(l, 2),
                 g_post_mix[l], ml_w_out[0].astype(BF16), tm=256, rows_per_mod=t_len)
    xl = _ffn(xl, g_pre_ffn[l], mod_lat(l, 3), mod_lat(l, 4), mod_lat(l, 5), g_post_ffn[l], w1[l], w2[l],
              tm=512, tf=1024, rows_per_mod=t_len)
    return xl.reshape(bsz, t_len, d)
```

```python
import functools
import math

import jax
import jax.numpy as jnp
from jax import lax
from jax.experimental import pallas as pl
from jax.experimental.pallas import tpu as pltpu

F32 = jnp.float32
BF16 = jnp.bfloat16

EPS = 1e-6
ROPE_BASE = 10000.0
GRID_W = 64
LANES = 128
HALO = 16
DIFF_DH = 64
DIFF_HEADS = 8
DIFF_DV = 128
CONV_W = 31
ML_HEADS = 8
ML_DK = 128
ML_DV = 256
ML_CONV_W = 3
SCAN_CHUNK = 256
ATTN_QSCALE = DIFF_DH ** -0.5 * math.log2(math.e)
VMEM_LIMIT = 56 * 1024 * 1024


def _params(sem):
    return pltpu.CompilerParams(dimension_semantics=sem, vmem_limit_bytes=VMEM_LIMIT)


def _sigmoid(x):
    return 1.0 / (1.0 + jnp.exp(-x))


def _ada_kernel(c_ref, w_ref, b_ref, o_ref):
    c = c_ref[...]
    s = (c * _sigmoid(c)).astype(BF16)
    o_ref[0] = jnp.dot(s, w_ref[0].astype(BF16), preferred_element_type=F32) + b_ref[0]


def _ada(cvec, ada_w, ada_b):
    depth, d, n = ada_w.shape
    tn = 1024
    return pl.pallas_call(
        _ada_kernel,
        grid=(depth, n // tn),
        in_specs=[pl.BlockSpec((8, d), lambda l, j: (0, 0)),
                  pl.BlockSpec((1, d, tn), lambda l, j: (l, 0, j)),
                  pl.BlockSpec((1, 1, tn), lambda l, j: (l, 0, j))],
        out_specs=pl.BlockSpec((1, 8, tn), lambda l, j: (l, 0, j)),
        out_shape=jax.ShapeDtypeStruct((depth, 8, n), F32),
        compiler_params=_params(("parallel", "parallel")),
        name="ada_ln",
    )(cvec, ada_w, ada_b.reshape(depth, 1, n))


def _prenorm(x, g, shift, scale):
    r = lax.rsqrt(jnp.mean(x * x, axis=-1, keepdims=True) + EPS)
    return x * r * (g * (1.0 + scale)) + shift


def _prenorm_mm_kernel(*refs, rope_blocks, tn, qscale_blocks, qscale, has_side):
    x_ref, g_ref, sh_ref, sc_ref, w_ref = refs[:5]
    rest = list(refs[5:])
    if rope_blocks:
        cos_ref, sina_ref, sinb_ref = rest[:3]
        rest = rest[3:]
    if has_side:
        ws_ref, o_ref, os_ref, h_ref = rest
    else:
        o_ref, h_ref = rest
    j = pl.program_id(1)

    @pl.when(j == 0)
    def _():
        h_ref[...] = _prenorm(x_ref[...], g_ref[...], sh_ref[0], sc_ref[0]).astype(BF16)
        if has_side:
            os_ref[...] = jnp.dot(h_ref[...], ws_ref[...], preferred_element_type=F32)

    acc = jnp.dot(h_ref[...], w_ref[...], preferred_element_type=F32)
    if qscale_blocks:
        acc = acc * jnp.where(j < qscale_blocks, qscale, 1.0)

    if rope_blocks:
        @pl.when(j < rope_blocks)
        def _():
            cos = cos_ref[...]
            sina = sina_ref[...]
            sinb = sinb_ref[...]
            for c in range(tn // LANES):
                xs = acc[:, c * LANES:(c + 1) * LANES]
                rot = pltpu.roll(xs, LANES - 16, 1) * sina + pltpu.roll(xs, 16, 1) * sinb
                o_ref[:, c * LANES:(c + 1) * LANES] = (xs * cos + rot).astype(o_ref.dtype)

        @pl.when(j >= rope_blocks)
        def _():
            o_ref[...] = acc.astype(o_ref.dtype)
    else:
        o_ref[...] = acc.astype(o_ref.dtype)


def _prenorm_mm(x, g, shift, scale, w, n_out, *, tm, tn, rows_per_mod, out_dtype, rope=None, rope_cols=0, qscale_cols=0,
                qscale=1.0, w_side=None, name):
    m, d = x.shape
    mod_map = lambda i, j: ((i * tm) // rows_per_mod, 0, 0)
    in_specs = [pl.BlockSpec((tm, d), lambda i, j: (i, 0)),
                pl.BlockSpec((1, d), lambda i, j: (0, 0)),
                pl.BlockSpec((1, 1, d), mod_map),
                pl.BlockSpec((1, 1, d), mod_map),
                pl.BlockSpec((d, tn), lambda i, j: (0, j))]
    args = [x, g.reshape(1, d), shift, scale, w]
    rope_blocks = 0
    if rope is not None:
        assert rope_cols % tn == 0
        rope_blocks = rope_cols // tn
        t_len = rope[0].shape[0]
        assert t_len % tm == 0
        nblk = t_len // tm
        for tab in rope:
            in_specs.append(pl.BlockSpec((tm, LANES), lambda i, j: (i % nblk, 0)))
            args.append(tab)
    out_specs = pl.BlockSpec((tm, tn), lambda i, j: (i, j))
    out_shape = jax.ShapeDtypeStruct((m, n_out), out_dtype)
    if w_side is not None:
        ns = w_side.shape[1]
        in_specs.append(pl.BlockSpec((d, ns), lambda i, j: (0, 0)))
        args.append(w_side)
        out_specs = (out_specs, pl.BlockSpec((tm, ns), lambda i, j: (i, 0)))
        out_shape = (out_shape, jax.ShapeDtypeStruct((m, ns), F32))
    return pl.pallas_call(
        functools.partial(_prenorm_mm_kernel, rope_blocks=rope_blocks, tn=tn, qscale_blocks=qscale_cols // tn,
                          qscale=qscale, has_side=w_side is not None),
        grid=(m // tm, n_out // tn),
        in_specs=in_specs,
        out_specs=out_specs,
        out_shape=out_shape,
        scratch_shapes=[pltpu.VMEM((tm, d), BF16)],
        compiler_params=_params(("parallel", "arbitrary")),
        name=name,
    )(*args)


def _rope_tables(t_len):
    rows = t_len // GRID_W
    r = jnp.repeat(jnp.arange(rows, dtype=F32), GRID_W)
    col = jnp.tile(jnp.arange(GRID_W, dtype=F32), rows)
    n_freq = DIFF_DH // 4
    inv = ROPE_BASE ** (-jnp.arange(n_freq, dtype=F32) / n_freq)
    ar = r[:, None] * inv
    ac = col[:, None] * inv
    ang = jnp.concatenate([ar, ar, ac, ac], axis=-1)
    cos = jnp.tile(jnp.cos(ang), (1, LANES // DIFF_DH))
    sin = jnp.tile(jnp.sin(ang), (1, LANES // DIFF_DH))
    first = (jnp.arange(LANES) % (2 * n_freq)) < n_freq
    return cos, jnp.where(first, -sin, 0.0), jnp.where(first, 0.0, sin)


def _attn_kernel(*refs, lam_init, two_sources, sub, kchunk):
    if two_sources:
        q_ref, k_ref, v_ref, kc_ref, vc_ref, lq1_ref, lk1_ref, lq2_ref, lk2_ref, g_ref, o_ref, vx_ref, ss_ref = refs
    else:
        q_ref, k_ref, v_ref, lq1_ref, lk1_ref, lq2_ref, lk2_ref, g_ref, o_ref, vx_ref, ss_ref = refs
    lam = (jnp.exp(jnp.sum(lq1_ref[...] * lk1_ref[...], axis=-1, keepdims=True))
           - jnp.exp(jnp.sum(lq2_ref[...] * lk2_ref[...], axis=-1, keepdims=True)) + lam_init)
    dn = (((1,), (1,)), ((), ()))
    tq = q_ref.shape[1]
    n_keys = k_ref.shape[1]
    chunks = [(k_ref, c0, min(kchunk, n_keys - c0), c0) for c0 in range(0, n_keys, kchunk)]
    if two_sources:
        chunks.append((kc_ref, 0, kc_ref.shape[1], n_keys))

    @pl.when(pl.program_id(2) == 0)
    def _():
        srcs = [(v_ref, 0)] + ([(vc_ref, n_keys)] if two_sources else [])
        for vr, off in srcs:
            n = vr.shape[1]
            vx_ref[off:off + n, 0:DIFF_DV] = vr[0]
            vx_ref[off:off + n, DIFF_DV:2 * DIFF_DV] = (
                lax.broadcasted_iota(jnp.int32, (n, DIFF_DV), 1) == 0).astype(BF16)

    nblk = tq // sub
    st = [dict() for _ in range(2 * nblk)]

    def qk(u, j):
        x, c = divmod(u, 2)
        d = st[u]
        if j == 0:
            q = q_ref[0, x * sub:(x + 1) * sub, :]
            lane = lax.broadcasted_iota(jnp.int32, q.shape, 1)
            keep = (lane < DIFF_DH) if c == 0 else (lane >= DIFF_DH)
            d["q"] = jnp.where(keep, q, jnp.zeros_like(q))
        kr, c0, n, off = chunks[j]
        s = lax.dot_general(d["q"], kr[0, c0:c0 + n, :], dn, preferred_element_type=F32)
        ss_ref[u % 2, :, off:off + n] = s
        mj = jnp.max(s, axis=-1, keepdims=True)
        d["m"] = mj if j == 0 else jnp.maximum(d["m"], mj)

    def ev(u, j):
        x, c = divmod(u, 2)
        d = st[u]
        _, _, n, voff = chunks[j]
        p = jnp.exp2(ss_ref[u % 2, :, voff:voff + n] - d["m"]).astype(BF16)
        part = jnp.dot(p, vx_ref[voff:voff + n, :], preferred_element_type=F32)
        d["acc"] = part if j == 0 else d["acc"] + part
        if j == len(chunks) - 1:
            acc = d["acc"]
            on = acc[:, 0:DIFF_DV] * (1.0 / acc[:, DIFF_DV:DIFF_DV + 1])
            o1 = d.get("o1")
            d.clear()
            if c == 0:
                st[u + 1]["o1"] = on
            else:
                o = o1 - lam * on
                r = lax.rsqrt(jnp.mean(o * o, axis=-1, keepdims=True) + EPS)
                o_ref[0, x * sub:(x + 1) * sub, :] = (o * r * g_ref[...] * (1.0 - lam_init)).astype(o_ref.dtype)

    nch = len(chunks)
    for u in range(2 * nblk + 1):
        for j in range(nch):
            if u < 2 * nblk:
                qk(u, j)
            if u >= 1:
                ev(u - 1, j)


def _diff_attn(zq, zkv_extra, lams, subln_g, lam_init, *, tq, sub, kchunk):
    bsz, t_len, _ = zq.shape
    nh = DIFF_HEADS
    two = zkv_extra is not None
    n_all = t_len + (zkv_extra.shape[1] if two else 0)
    in_specs = [pl.BlockSpec((1, tq, LANES), lambda b, h, i: (b, i, h)),
                pl.BlockSpec((1, t_len, LANES), lambda b, h, i: (b, 0, nh + h)),
                pl.BlockSpec((1, t_len, LANES), lambda b, h, i: (b, 0, 2 * nh + h))]
    args = [zq, zq, zq]
    if two:
        c_len = zkv_extra.shape[1]
        in_specs += [pl.BlockSpec((1, c_len, LANES), lambda b, h, i: (b, 0, nh + h)),
                     pl.BlockSpec((1, c_len, LANES), lambda b, h, i: (b, 0, 2 * nh + h))]
        args += [zkv_extra, zkv_extra]
    for v in lams:
        in_specs.append(pl.BlockSpec((1, DIFF_DH), lambda b, h, i: (0, 0)))
        args.append(v.reshape(1, DIFF_DH))
    in_specs.append(pl.BlockSpec((1, DIFF_DV), lambda b, h, i: (0, 0)))
    args.append(subln_g.reshape(1, DIFF_DV))
    return pl.pallas_call(
        functools.partial(_attn_kernel, lam_init=lam_init, two_sources=two, sub=sub, kchunk=kchunk),
        grid=(bsz, nh, t_len // tq),
        in_specs=in_specs,
        out_specs=pl.BlockSpec((1, tq, LANES), lambda b, h, i: (b, i, h)),
        out_shape=jax.ShapeDtypeStruct((bsz, t_len, nh * DIFF_DV), BF16),
        scratch_shapes=[pltpu.VMEM((n_all, 2 * DIFF_DV), BF16), pltpu.VMEM((2, sub, n_all), F32)],
        compiler_params=_params(("parallel", "parallel", "arbitrary")),
        name="diff_attn" + ("_lat" if two else "_ctx"),
    )(*args)


SUBLANES = 8


def _dwconv_phases(width):
    offs = [HALO - width // 2 + w for w in range(width)]
    return sorted({o % SUBLANES for o in offs} - {0}), max(offs) // SUBLANES * SUBLANES


def _dwconv_scratch(tt, width, nch):
    phases, reach = _dwconv_phases(width)
    return pltpu.VMEM((max(len(phases), 1), tt + reach, nch), F32)


def _dwconv_into(y_ref, ysh_ref, w_ref, b_ref, out_ref, *, tt, width, nch):
    phases, reach = _dwconv_phases(width)
    for idx, p in enumerate(phases):
        ysh_ref[idx] = y_ref[p:p + tt + reach, :]
    rc = 64
    for c0 in range(0, nch, LANES):
        for r0 in range(0, tt, rc):
            acc = jnp.broadcast_to(b_ref[:, c0:c0 + LANES], (rc, LANES))
            for w in range(width):
                off = HALO - width // 2 + w
                p, start = off % SUBLANES, r0 + off // SUBLANES * SUBLANES
                if p == 0:
                    tap = y_ref[start:start + rc, c0:c0 + LANES]
                else:
                    tap = ysh_ref[phases.index(p), start:start + rc, c0:c0 + LANES]
                acc = acc + tap * w_ref[w:w + 1, c0:c0 + LANES]
            out_ref[r0:r0 + rc, c0:c0 + LANES] = acc


def _convmod_kernel(ap_ref, a_ref, an_ref, gp_ref, g_ref, gn_ref, cw_ref, cb_ref, lng_ref, lnb_ref, o_ref,
                    y_ref, c_ref, ysh_ref, *, tt, nch):
    i = pl.program_id(1)
    n = pl.num_programs(1)

    def glu(a, g):
        return a.astype(F32) * _sigmoid(g.astype(F32))

    y_ref[0:HALO] = jnp.where(i > 0, glu(ap_ref[0], gp_ref[0]), 0.0)
    y_ref[HALO:HALO + tt] = glu(a_ref[0], g_ref[0])
    y_ref[HALO + tt:2 * HALO + tt] = jnp.where(i < n - 1, glu(an_ref[0], gn_ref[0]), 0.0)
    _dwconv_into(y_ref, ysh_ref, cw_ref, cb_ref, c_ref, tt=tt, width=CONV_W, nch=nch)
    c = c_ref[...]
    mu = jnp.mean(c, axis=-1, keepdims=True)
    xc = c - mu
    var = jnp.mean(xc * xc, axis=-1, keepdims=True)
    y = xc * lax.rsqrt(var + EPS) * lng_ref[...] + lnb_ref[...]
    o_ref[0] = (y * _sigmoid(y)).astype(o_ref.dtype)


def _halo_specs(tt, t_len, width, colblk):
    per = tt // HALO
    last = t_len // HALO - 1
    return (pl.BlockSpec((1, HALO, width), lambda b, i: (b, jnp.maximum(i * per - 1, 0), colblk)),
            pl.BlockSpec((1, tt, width), lambda b, i: (b, i, colblk)),
            pl.BlockSpec((1, HALO, width), lambda b, i: (b, jnp.minimum((i + 1) * per, last), colblk)))


def _conv_module(z, conv_w, conv_b, ln_g, ln_b, *, tt):
    bsz, t_len, n = z.shape
    nch = conv_w.shape[1]
    a_blk = (n - 2 * nch) // nch
    vec = lambda: pl.BlockSpec((1, nch), lambda b, i: (0, 0))
    return pl.pallas_call(
        functools.partial(_convmod_kernel, tt=tt, nch=nch),
        grid=(bsz, t_len // tt),
        in_specs=[*_halo_specs(tt, t_len, nch, a_blk), *_halo_specs(tt, t_len, nch, a_blk + 1),
                  pl.BlockSpec((CONV_W, nch), lambda b, i: (0, 0)), vec(), vec(), vec()],
        out_specs=pl.BlockSpec((1, tt, nch), lambda b, i: (b, i, 0)),
        out_shape=jax.ShapeDtypeStruct((bsz, t_len, nch), BF16),
        scratch_shapes=[pltpu.VMEM((tt + 2 * HALO, nch), F32), pltpu.VMEM((tt, nch), F32),
                        _dwconv_scratch(tt, CONV_W, nch)],
        compiler_params=_params(("parallel", "arbitrary")),
        name="conformer_conv",
    )(z, z, z, z, z, z, conv_w, conv_b.reshape(1, nch), ln_g.reshape(1, nch), ln_b.reshape(1, nch))


def _conv3_kernel(zp_ref, z_ref, zn_ref, cw_ref, cb_ref, o_ref, y_ref, c_ref, ysh_ref, *, tt, nch, kscale):
    i = pl.program_id(1)
    n = pl.num_programs(1)
    y_ref[0:HALO] = jnp.where(i > 0, zp_ref[0].astype(F32), 0.0)
    y_ref[HALO:HALO + tt] = z_ref[0].astype(F32)
    y_ref[HALO + tt:2 * HALO + tt] = jnp.where(i < n - 1, zn_ref[0].astype(F32), 0.0)
    _dwconv_into(y_ref, ysh_ref, cw_ref, cb_ref, c_ref, tt=tt, width=ML_CONV_W, nch=nch)
    c = c_ref[...]
    s = c * _sigmoid(c)
    half = nch // 2
    o_ref[0, :, 0:half] = s[:, 0:half].astype(o_ref.dtype)
    o_ref[0, :, half:nch] = (s[:, half:nch] * kscale).astype(o_ref.dtype)


def _conv3_silu(z, conv_w, conv_b, *, tt):
    bsz, t_len, _ = z.shape
    nch = conv_w.shape[1]
    return pl.pallas_call(
        functools.partial(_conv3_kernel, tt=tt, nch=nch, kscale=ML_DK ** -0.5),
        grid=(bsz, t_len // tt),
        in_specs=[*_halo_specs(tt, t_len, nch, 0),
                  pl.BlockSpec((ML_CONV_W, nch), lambda b, i: (0, 0)),
                  pl.BlockSpec((1, nch), lambda b, i: (0, 0))],
        out_specs=pl.BlockSpec((1, tt, nch), lambda b, i: (b, i, 0)),
        out_shape=jax.ShapeDtypeStruct((bsz, t_len, nch), BF16),
        scratch_shapes=[pltpu.VMEM((tt + 2 * HALO, nch), F32), pltpu.VMEM((tt, nch), F32),
                        _dwconv_scratch(tt, ML_CONV_W, nch)],
        compiler_params=_params(("parallel", "arbitrary")),
        name="ml_conv3_silu",
    )(z, z, z, conv_w, conv_b.reshape(1, nch))


def _post_residual(y, x, gate, gpost):
    r = lax.rsqrt(jnp.mean(y * y, axis=-1, keepdims=True) + EPS)
    return x + gate * (y * r * gpost)


def _hy_out_kernel(a_ref, c_ref, x_ref, gate_ref, gpost_ref, w_ref, o_ref):
    ka = a_ref.shape[1]
    y = (jnp.dot(a_ref[...], w_ref[0:ka, :], preferred_element_type=F32)
         + jnp.dot(c_ref[...], w_ref[ka:, :], preferred_element_type=F32))
    o_ref[...] = _post_residual(y, x_ref[...], gate_ref[0], gpost_ref[...])


def _hy_out(attn, conv, x, gate, gpost, w, *, tm, rows_per_mod):
    m, d = x.shape
    ka, kc = attn.shape[1], conv.shape[1]
    return pl.pallas_call(
        _hy_out_kernel,
        grid=(m // tm,),
        in_specs=[pl.BlockSpec((tm, ka), lambda i: (i, 0)),
                  pl.BlockSpec((tm, kc), lambda i: (i, 0)),
                  pl.BlockSpec((tm, d), lambda i: (i, 0)),
                  pl.BlockSpec((1, 1, d), lambda i: ((i * tm) // rows_per_mod, 0, 0)),
                  pl.BlockSpec((1, d), lambda i: (0, 0)),
                  pl.BlockSpec((ka + kc, d), lambda i: (0, 0), pipeline_mode=pl.Buffered(1))],
        out_specs=pl.BlockSpec((tm, d), lambda i: (i, 0)),
        out_shape=jax.ShapeDtypeStruct((m, d), F32),
        compiler_params=_params(("parallel",)),
        name="hybrid_out_proj",
    )(attn, conv, x, gate, gpost.reshape(1, d), w)


def _ml_out_kernel(hf_ref, hb_ref, og_ref, x_ref, ng_ref, gate_ref, gpost_ref, w_ref, o_ref):
    hsum = hf_ref[0].astype(F32) + hb_ref[0].astype(F32)
    parts = []
    for h in range(ML_HEADS):
        hh = hsum[:, h * ML_DV:(h + 1) * ML_DV]
        r = lax.rsqrt(jnp.mean(hh * hh, axis=-1, keepdims=True) + EPS)
        parts.append(hh * r * ng_ref[:, h * ML_DV:(h + 1) * ML_DV])
    hn = jnp.concatenate(parts, axis=1)
    a = (hn * _sigmoid(og_ref[...].astype(F32))).astype(BF16)
    y = jnp.dot(a, w_ref[...], preferred_element_type=F32)
    o_ref[...] = _post_residual(y, x_ref[...], gate_ref[0], gpost_ref[...])


def _ml_out(h2, z, o_blk, x, norm_g, gate, gpost, w, *, tm, rows_per_mod):
    m, d = x.shape
    kv = h2.shape[2]
    return pl.pallas_call(
        _ml_out_kernel,
        grid=(m // tm,),
        in_specs=[pl.BlockSpec((1, tm, kv), lambda i: (0, i, 0)),
                  pl.BlockSpec((1, tm, kv), lambda i: (1, i, 0)),
                  pl.BlockSpec((tm, kv), lambda i: (i, o_blk)),
                  pl.BlockSpec((tm, d), lambda i: (i, 0)),
                  pl.BlockSpec((1, kv), lambda i: (0, 0)),
                  pl.BlockSpec((1, 1, d), lambda i: ((i * tm) // rows_per_mod, 0, 0)),
                  pl.BlockSpec((1, d), lambda i: (0, 0)),
                  pl.BlockSpec((kv, d), lambda i: (0, 0), pipeline_mode=pl.Buffered(1))],
        out_specs=pl.BlockSpec((tm, d), lambda i: (i, 0)),
        out_shape=jax.ShapeDtypeStruct((m, d), F32),
        compiler_params=_params(("parallel",)),
        name="mlstm_out_proj",
    )(h2, h2, z, x, norm_g.reshape(1, kv), gate, gpost.reshape(1, d), w)


def _ffn_kernel(x_ref, g_ref, sh_ref, sc_ref, gate_ref, gpost_ref, w1_ref, w2_ref, o_ref, h_ref, acc_ref):
    f = pl.program_id(1)
    nf = pl.num_programs(1)

    @pl.when(f == 0)
    def _():
        h_ref[...] = _prenorm(x_ref[...], g_ref[...], sh_ref[0], sc_ref[0]).astype(BF16)

    t = jnp.maximum(jnp.dot(h_ref[...], w1_ref[...], preferred_element_type=F32), 0.0)
    p = jnp.dot((t * t).astype(BF16), w2_ref[...], preferred_element_type=F32)

    @pl.when(f == 0)
    def _():
        acc_ref[...] = p

    @pl.when(f > 0)
    def _():
        acc_ref[...] += p

    @pl.when(f == nf - 1)
    def _():
        o_ref[...] = _post_residual(acc_ref[...], x_ref[...], gate_ref[0], gpost_ref[...])


def _ffn(x, g, shift, scale, gate, gpost, w1, w2, *, tm, tf, rows_per_mod):
    m, d = x.shape
    dff = w1.shape[1]
    mod_map = lambda i, f: ((i * tm) // rows_per_mod, 0, 0)
    vec = lambda: pl.BlockSpec((1, d), lambda i, f: (0, 0))
    return pl.pallas_call(
        _ffn_kernel,
        grid=(m // tm, dff // tf),
        in_specs=[pl.BlockSpec((tm, d), lambda i, f: (i, 0)), vec(),
                  pl.BlockSpec((1, 1, d), mod_map), pl.BlockSpec((1, 1, d), mod_map),
                  pl.BlockSpec((1, 1, d), mod_map), vec(),
                  pl.BlockSpec((d, tf), lambda i, f: (0, f)),
                  pl.BlockSpec((tf, d), lambda i, f: (f, 0))],
        out_specs=pl.BlockSpec((tm, d), lambda i, f: (i, 0)),
        out_shape=jax.ShapeDtypeStruct((m, d), F32),
        scratch_shapes=[pltpu.VMEM((tm, d), BF16), pltpu.VMEM((tm, d), F32)],
        compiler_params=_params(("parallel", "arbitrary")),
        name="ffn",
    )(x, g.reshape(1, d), shift, scale, gate, gpost.reshape(1, d), w1, w2)


def _scan_kernel(ql_ref, kl_ref, vl_ref, gil_ref, gfl_ref, qc_ref, kc_ref, vc_ref, gic_ref, gfc_ref,
                 bi_ref, bf_ref, o_ref, c_ref, m_ref):
    L, H, DK, DV = SCAN_CHUNK, ML_HEADS, ML_DK, ML_DV
    fwd = pl.program_id(0) == 0
    s = pl.program_id(2)

    @pl.when(s == 0)
    def _():
        c_ref[...] = jnp.zeros_like(c_ref)
        m_ref[...] = jnp.zeros_like(m_ref)

    is_ctx = s == 0
    q = jnp.where(is_ctx, qc_ref[0], ql_ref[0])
    k = jnp.where(is_ctx, kc_ref[0], kl_ref[0])
    v = jnp.where(is_ctx, vc_ref[0], vl_ref[0])
    ipre = jnp.where(is_ctx, gic_ref[0, 0], gil_ref[0, 0]) + bi_ref[0]
    fpre = jnp.where(is_ctx, gfc_ref[0, 0], gfl_ref[0, 0]) + bf_ref[0]
    lf = jnp.minimum(fpre, 0.0) - jnp.log1p(jnp.exp(-jnp.abs(fpre)))

    row = lax.broadcasted_iota(jnp.int32, (L, L), 0)
    col = lax.broadcasted_iota(jnp.int32, (L, L), 1)
    delta = (row - col) * jnp.where(fwd, 1, -1)
    tri = jnp.where(delta <= 0, 1.0, 0.0).astype(F32)
    b_all = jnp.dot(lf, tri, precision=lax.Precision.HIGHEST, preferred_element_type=F32)
    r_all = ipre - b_all
    b_last_all = jnp.where(fwd, b_all[:, L - 1:L], b_all[:, 0:1])
    mask = delta >= 0
    ones_blk = (lax.broadcasted_iota(jnp.int32, (L, LANES), 1) == 0).astype(BF16)
    dn_t = (((1,), (1,)), ((), ()))

    for h in range(H):
        b_row = b_all[h:h + 1, :]
        r_row = r_all[h:h + 1, :]
        bcol = jnp.transpose(jnp.broadcast_to(b_row, (LANES, L)))
        bcol_l = jnp.concatenate([bcol] * (L // LANES), axis=1)
        dm = jnp.where(mask, bcol_l + r_row, -jnp.inf)
        m_loc = jnp.max(dm, axis=1, keepdims=True)
        m_prev = m_ref[h:h + 1, 0:1]
        g = bcol[:, 0:1] + m_prev
        m_row = jnp.maximum(g, m_loc)
        inter = jnp.exp(g - m_row)
        p = jnp.exp(dm - m_row)
        qh = q[:, h * DK:(h + 1) * DK]
        kh = k[:, h * DK:(h + 1) * DK]
        vext = jnp.concatenate([v[:, h * DV:(h + 1) * DV], ones_blk], axis=1)
        sm = (lax.dot_general(qh, kh, dn_t, preferred_element_type=F32) * p).astype(BF16)
        cst = c_ref[h]
        nd = (inter * jnp.dot(qh, cst.astype(BF16), preferred_element_type=F32)
              + jnp.dot(sm, vext, preferred_element_type=F32))
        den = jnp.maximum(jnp.abs(nd[:, DV:DV + 1]), jnp.exp(-m_row))
        o_ref[0, 0, :, h * DV:(h + 1) * DV] = (nd[:, 0:DV] * (1.0 / den)).astype(o_ref.dtype)

        b_last = b_last_all[h:h + 1, :]
        wlog = b_last + r_row
        m_new = jnp.maximum(b_last + m_prev, jnp.max(wlog, axis=1, keepdims=True))
        decay = jnp.exp(b_last + m_prev - m_new)
        kt = jnp.transpose(kh.astype(F32))
        ktw = (kt * jnp.exp(wlog - m_new)).astype(BF16)
        c_ref[h] = decay * cst + jnp.dot(ktw, vext, preferred_element_type=F32)
        m_ref[h:h + 1, :] = jnp.broadcast_to(m_new, (1, LANES))


def _mlstm_scan(qk_l, z_l, gates_l, qk_c, z_c, gates_c, gate_b):
    L, H = SCAN_CHUNK, ML_HEADS
    bsz, t_len, _ = qk_l.shape
    assert qk_c.shape[1] == L
    n_lat = t_len // L
    hk, hv = H * ML_DK, H * ML_DV

    def lat(d, s):
        return jnp.where(d == 0, jnp.maximum(s - 1, 0), n_lat - jnp.maximum(s, 1))

    in_specs = [
        pl.BlockSpec((1, L, hk), lambda d, b, s: (b, lat(d, s), 0)),
        pl.BlockSpec((1, L, hk), lambda d, b, s: (b, lat(d, s), 1)),
        pl.BlockSpec((1, L, hv), lambda d, b, s: (b, lat(d, s), 1)),
        pl.BlockSpec((1, 1, H, L), lambda d, b, s: (b, 2 * d, 0, lat(d, s))),
        pl.BlockSpec((1, 1, H, L), lambda d, b, s: (b, 2 * d + 1, 0, lat(d, s))),
        pl.BlockSpec((1, L, hk), lambda d, b, s: (b, 0, 0)),
        pl.BlockSpec((1, L, hk), lambda d, b, s: (b, 0, 1)),
        pl.BlockSpec((1, L, hv), lambda d, b, s: (b, 0, 1)),
        pl.BlockSpec((1, 1, H, L), lambda d, b, s: (b, 2 * d, 0, 0)),
        pl.BlockSpec((1, 1, H, L), lambda d, b, s: (b, 2 * d + 1, 0, 0)),
        pl.BlockSpec((1, H, 1), lambda d, b, s: (2 * d, 0, 0)),
        pl.BlockSpec((1, H, 1), lambda d, b, s: (2 * d + 1, 0, 0)),
    ]
    return pl.pallas_call(
        _scan_kernel,
        grid=(2, bsz, n_lat + 1),
        in_specs=in_specs,
        out_specs=pl.BlockSpec((1, 1, L, hv), lambda d, b, s: (d, b, lat(d, s), 0)),
        out_shape=jax.ShapeDtypeStruct((2, bsz, t_len, hv), BF16),
        scratch_shapes=[pltpu.VMEM((H, ML_DK, ML_DV + LANES), F32), pltpu.VMEM((H, LANES), F32)],
        compiler_params=_params(("parallel", "parallel", "arbitrary")),
        name="mlstm_scan",
    )(qk_l, qk_l, z_l, gates_l, gates_l, qk_c, qk_c, z_c, gates_c, gates_c,
      gate_b.reshape(4, H, 1), gate_b.reshape(4, H, 1))


def kernel(x, c, ctx, c_ctx, ada_w, ada_b, g_pre_mix, g_post_mix, g_pre_ffn, g_post_ffn, ffn_w1, ffn_w2, hy_w_in, hy_w_out, diff_lq1, diff_lk1, diff_lq2, diff_lk2, diff_subln_g, conv_w, conv_b, conv_ln_g, conv_ln_b, ml_w_in, ml_conv_w, ml_conv_b, ml_gate_b, ml_norm_g, ml_w_out):
    bsz, t_len, d = x.shape
    c_len = ctx.shape[1]
    depth = ada_w.shape[0]
    assert depth == 2 and bsz <= 7 and c_len == SCAN_CHUNK
    ml = t_len * bsz
    mc = c_len * bsz

    cvec = jnp.concatenate([c, c_ctx[None, :], jnp.zeros((8 - bsz - 1, d), F32)], axis=0)
    mods = _ada(cvec, ada_w, ada_b)

    def mod_lat(l, k):
        return mods[l, :bsz, k * d:(k + 1) * d].reshape(bsz, 1, d)

    def mod_ctx(l, k):
        return mods[l, bsz:bsz + 1, k * d:(k + 1) * d].reshape(1, 1, d)

    xl = x.reshape(ml, d)
    xc = ctx.reshape(mc, d)
    w1 = ffn_w1.astype(BF16)
    w2 = ffn_w2.astype(BF16)

    l = 0
    lam_init = 0.8 - 0.6 * math.exp(-0.3 * l)
    w_in = hy_w_in[0].astype(BF16)
    n_in = w_in.shape[1]
    qk_w = 2 * DIFF_HEADS * 2 * DIFF_DH
    rope = _rope_tables(t_len)
    zl = _prenorm_mm(xl, g_pre_mix[l], mod_lat(l, 0), mod_lat(l, 1), w_in, n_in, tm=1024, tn=1024,
                     rows_per_mod=t_len, out_dtype=BF16, rope=rope, rope_cols=qk_w, qscale_cols=qk_w // 2,
                     qscale=ATTN_QSCALE, name="hy_in_proj_lat")
    zc = _prenorm_mm(xc, g_pre_mix[l], mod_ctx(l, 0), mod_ctx(l, 1), w_in, n_in, tm=mc, tn=1024,
                     rows_per_mod=mc, out_dtype=BF16, qscale_cols=qk_w // 2, qscale=ATTN_QSCALE,
                     name="hy_in_proj_ctx")
    zl3 = zl.reshape(bsz, t_len, n_in)
    zc3 = zc.reshape(bsz, c_len, n_in)
    lams = (diff_lq1[0], diff_lk1[0], diff_lq2[0], diff_lk2[0])
    attn_l = _diff_attn(zl3, zc3, lams, diff_subln_g[0], lam_init, tq=1024, sub=256, kchunk=512)
    attn_c = _diff_attn(zc3, None, lams, diff_subln_g[0], lam_init, tq=c_len, sub=c_len, kchunk=c_len)
    conv_l = _conv_module(zl3, conv_w[0], conv_b[0], conv_ln_g[0], conv_ln_b[0], tt=256)
    conv_c = _conv_module(zc3, conv_w[0], conv_b[0], conv_ln_g[0], conv_ln_b[0], tt=c_len)
    w_out = hy_w_out[0].astype(BF16)
    xl = _hy_out(attn_l.reshape(ml, -1), conv_l.reshape(ml, -1), xl, mod_lat(l, 2), g_post_mix[l], w_out,
                 tm=512, rows_per_mod=t_len)
    xc = _hy_out(attn_c.reshape(mc, -1), conv_c.reshape(mc, -1), xc, mod_ctx(l, 2), g_post_mix[l], w_out,
                 tm=512, rows_per_mod=mc)
    xl = _ffn(xl, g_pre_ffn[l], mod_lat(l, 3), mod_lat(l, 4), mod_lat(l, 5), g_post_ffn[l], w1[l], w2[l],
              tm=512, tf=1024, rows_per_mod=t_len)
    xc = _ffn(xc, g_pre_ffn[l], mod_ctx(l, 3), mod_ctx(l, 4), mod_ctx(l, 5), g_post_ffn[l], w1[l], w2[l],
              tm=512, tf=1024, rows_per_mod=mc)

    l = 1
    qkv_w = 2 * ML_HEADS * ML_DK + ML_HEADS * ML_DV
    n_gates = 4 * ML_HEADS
    wm = ml_w_in[0].astype(BF16)
    w_main = jnp.concatenate([wm[:, :qkv_w], wm[:, qkv_w + n_gates:]], axis=1)
    w_gate = jnp.pad(wm[:, qkv_w:qkv_w + n_gates], ((0, 0), (0, LANES - n_gates)))
    n_main = w_main.shape[1]
    zl, gl = _prenorm_mm(xl, g_pre_mix[l], mod_lat(l, 0), mod_lat(l, 1), w_main, n_main, tm=1024, tn=1024,
                         rows_per_mod=t_len, out_dtype=BF16, w_side=w_gate, name="ml_in_proj_lat")
    zc, gc = _prenorm_mm(xc, g_pre_mix[l], mod_ctx(l, 0), mod_ctx(l, 1), w_main, qkv_w, tm=mc, tn=1024,
                         rows_per_mod=mc, out_dtype=BF16, w_side=w_gate, name="ml_in_proj_ctx")
    zl3 = zl.reshape(bsz, t_len, n_main)
    zc3 = zc.reshape(bsz, c_len, qkv_w)
    qk_l = _conv3_silu(zl3, ml_conv_w[0], ml_conv_b[0], tt=256)
    qk_c = _conv3_silu(zc3, ml_conv_w[0], ml_conv_b[0], tt=c_len)

    def gates_t(gm, n):
        return gm[:, :n_gates].reshape(bsz, n, 4, ML_HEADS).transpose(0, 2, 3, 1)

    h2 = _mlstm_scan(qk_l, zl3, gates_t(gl, t_len), qk_c, zc3, gates_t(gc, c_len), ml_gate_b[0])
    xl = _ml_out(h2.reshape(2, ml, -1), zl, qkv_w // (ML_HEADS * ML_DV), xl, ml_norm_g[0], mod_lat(l, 2),
                 g_post_mix[l], ml_w_out[0].astype(BF16), tm=512, rows_per_mod=t_len)
    xl = _ffn(xl, g_pre_ffn[l], mod_lat(l, 3), mod_lat(l, 4), mod_lat(l, 5), g_post_ffn[l], w1[l], w2[l],
              tm=512, tf=1024, rows_per_mod=t_len)
    return xl.reshape(bsz, t_len, d)
```

```python
import functools
import math

import jax
import jax.numpy as jnp
from jax import lax
from jax.experimental import pallas as pl
from jax.experimental.pallas import tpu as pltpu

F32 = jnp.float32
BF16 = jnp.bfloat16

EPS = 1e-6
ROPE_BASE = 10000.0
GRID_W = 64
LANES = 128
HALO = 16
DIFF_DH = 64
DIFF_HEADS = 8
DIFF_DV = 128
CONV_W = 31
ML_HEADS = 8
ML_DK = 128
ML_DV = 256
ML_CONV_W = 3
SCAN_CHUNK = 256
ATTN_QSCALE = DIFF_DH ** -0.5 * math.log2(math.e)
VMEM_LIMIT = 56 * 1024 * 1024


def _params(sem, flags=None):
    return pltpu.CompilerParams(dimension_semantics=sem, vmem_limit_bytes=VMEM_LIMIT, flags=flags)


def _sigmoid(x):
    return 1.0 / (1.0 + jnp.exp(-x))


def _runtime_zero():
    return jnp.minimum(pl.program_id(0), 0)


def _ada_kernel(c_ref, w_ref, b_ref, o_ref):
    c = c_ref[...]
    s = (c * _sigmoid(c)).astype(BF16)
    o_ref[0] = jnp.dot(s, w_ref[0].astype(BF16), preferred_element_type=F32) + b_ref[0]


def _ada(cvec, ada_w, ada_b):
    depth, d, n = ada_w.shape
    tn = 1024
    return pl.pallas_call(
        _ada_kernel,
        grid=(depth, n // tn),
        in_specs=[pl.BlockSpec((8, d), lambda l, j: (0, 0)),
                  pl.BlockSpec((1, d, tn), lambda l, j: (l, 0, j)),
                  pl.BlockSpec((1, 1, tn), lambda l, j: (l, 0, j))],
        out_specs=pl.BlockSpec((1, 8, tn), lambda l, j: (l, 0, j)),
        out_shape=jax.ShapeDtypeStruct((depth, 8, n), F32),
        compiler_params=_params(("parallel", "parallel")),
        name="ada_ln",
    )(cvec, ada_w, ada_b.reshape(depth, 1, n))


def _prenorm(x, g, shift, scale):
    r = lax.rsqrt(jnp.mean(x * x, axis=-1, keepdims=True) + EPS)
    return x * r * (g * (1.0 + scale)) + shift


def _prenorm_mm_kernel(*refs, rope_blocks, tn, qscale_blocks, qscale, has_side):
    x_ref, g_ref, sh_ref, sc_ref, w_ref = refs[:5]
    rest = list(refs[5:])
    if rope_blocks:
        cos_ref, sina_ref, sinb_ref = rest[:3]
        rest = rest[3:]
    if has_side:
        ws_ref, o_ref, os_ref, h_ref = rest
    else:
        o_ref, h_ref = rest
    j = pl.program_id(1)

    @pl.when(j == 0)
    def _():
        h_ref[...] = _prenorm(x_ref[...], g_ref[...], sh_ref[0], sc_ref[0]).astype(BF16)
        if has_side:
            os_ref[...] = jnp.dot(h_ref[...], ws_ref[...], preferred_element_type=F32)

    acc = jnp.dot(h_ref[...], w_ref[...], preferred_element_type=F32)
    if qscale_blocks:
        acc = acc * jnp.where(j < qscale_blocks, qscale, 1.0)

    if rope_blocks:
        cos = cos_ref[0]
        sina = sina_ref[0]
        sinb = sinb_ref[0]
        for c in range(tn // LANES):
            xs = acc[:, c * LANES:(c + 1) * LANES]
            rot = pltpu.roll(xs, LANES - 16, 1) * sina + pltpu.roll(xs, 16, 1) * sinb
            o_ref[:, c * LANES:(c + 1) * LANES] = (xs * cos + rot).astype(o_ref.dtype)
    else:
        o_ref[...] = acc.astype(o_ref.dtype)


def _prenorm_mm(x, g, shift, scale, w, n_out, *, tm, tn, rows_per_mod, out_dtype, rope=None, rope_cols=0, qscale_cols=0,
                qscale=1.0, w_side=None, name):
    m, d = x.shape
    mod_map = lambda i, j: ((i * tm) // rows_per_mod, 0, 0)
    in_specs = [pl.BlockSpec((tm, d), lambda i, j: (i, 0)),
                pl.BlockSpec((1, d), lambda i, j: (0, 0)),
                pl.BlockSpec((1, 1, d), mod_map),
                pl.BlockSpec((1, 1, d), mod_map),
                pl.BlockSpec((d, tn), lambda i, j: (0, j))]
    args = [x, g.reshape(1, d), shift, scale, w]
    rope_blocks = 0
    if rope is not None:
        assert rope_cols % tn == 0
        rope_blocks = rope_cols // tn
        t_len = rope[0].shape[0]
        assert t_len % tm == 0
        nblk = t_len // tm
        for tab, ident in zip(rope, (1.0, 0.0, 0.0)):
            in_specs.append(pl.BlockSpec((1, tm, LANES), lambda i, j: (jnp.where(j < rope_blocks, 0, 1), i % nblk, 0)))
            args.append(jnp.stack([tab, jnp.full_like(tab, ident)]))
    out_specs = pl.BlockSpec((tm, tn), lambda i, j: (i, j))
    out_shape = jax.ShapeDtypeStruct((m, n_out), out_dtype)
    if w_side is not None:
        ns = w_side.shape[1]
        in_specs.append(pl.BlockSpec((d, ns), lambda i, j: (0, 0)))
        args.append(w_side)
        out_specs = (out_specs, pl.BlockSpec((tm, ns), lambda i, j: (i, 0)))
        out_shape = (out_shape, jax.ShapeDtypeStruct((m, ns), F32))
    return pl.pallas_call(
        functools.partial(_prenorm_mm_kernel, rope_blocks=rope_blocks, tn=tn, qscale_blocks=qscale_cols // tn,
                          qscale=qscale, has_side=w_side is not None),
        grid=(m // tm, n_out // tn),
        in_specs=in_specs,
        out_specs=out_specs,
        out_shape=out_shape,
        scratch_shapes=[pltpu.VMEM((tm, d), BF16)],
        compiler_params=_params(("parallel", "arbitrary")),
        name=name,
    )(*args)


def _rope_tables(t_len):
    rows = t_len // GRID_W
    r = jnp.repeat(jnp.arange(rows, dtype=F32), GRID_W)
    col = jnp.tile(jnp.arange(GRID_W, dtype=F32), rows)
    n_freq = DIFF_DH // 4
    inv = ROPE_BASE ** (-jnp.arange(n_freq, dtype=F32) / n_freq)
    ar = r[:, None] * inv
    ac = col[:, None] * inv
    ang = jnp.concatenate([ar, ar, ac, ac], axis=-1)
    cos = jnp.tile(jnp.cos(ang), (1, LANES // DIFF_DH))
    sin = jnp.tile(jnp.sin(ang), (1, LANES // DIFF_DH))
    first = (jnp.arange(LANES) % (2 * n_freq)) < n_freq
    return cos, jnp.where(first, -sin, 0.0), jnp.where(first, 0.0, sin)


def _attn_kernel(*refs, lam_init, two_sources, sub, kchunk):
    if two_sources:
        q_ref, k_ref, v_ref, kc_ref, vc_ref, lq1_ref, lk1_ref, lq2_ref, lk2_ref, g_ref, o_ref, vx_ref, *ss_refs = refs
    else:
        q_ref, k_ref, v_ref, lq1_ref, lk1_ref, lq2_ref, lk2_ref, g_ref, o_ref, vx_ref, *ss_refs = refs
    z = _runtime_zero()
    lam = (jnp.exp(jnp.sum(lq1_ref[...] * lk1_ref[...], axis=-1, keepdims=True))
           - jnp.exp(jnp.sum(lq2_ref[...] * lk2_ref[...], axis=-1, keepdims=True)) + lam_init)
    dn = (((1,), (1,)), ((), ()))
    tq = q_ref.shape[1]
    n_keys = k_ref.shape[1]
    chunks = [(k_ref, c0, min(kchunk, n_keys - c0), c0) for c0 in range(0, n_keys, kchunk)]
    if two_sources:
        chunks.append((kc_ref, 0, kc_ref.shape[1], n_keys))

    @pl.when(pl.program_id(2) == 0)
    def _():
        srcs = [(v_ref, 0)] + ([(vc_ref, n_keys)] if two_sources else [])
        for vr, off in srcs:
            n = vr.shape[1]
            vx_ref[off:off + n, 0:DIFF_DV] = vr[0]
            vx_ref[off:off + n, DIFF_DV:2 * DIFF_DV] = (
                lax.broadcasted_iota(jnp.int32, (n, DIFF_DV), 1) == 0).astype(BF16)

    nblk = tq // sub
    st = [dict() for _ in range(2 * nblk)]

    def qk(u, j):
        x, c = divmod(u, 2)
        d = st[u]
        if j == 0:
            q = q_ref[0, x * sub:(x + 1) * sub, :]
            lane = lax.broadcasted_iota(jnp.int32, q.shape, 1)
            keep = (lane < DIFF_DH) if c == 0 else (lane >= DIFF_DH)
            d["q"] = jnp.where(keep, q, jnp.zeros_like(q))
        kr, c0, n, off = chunks[j]
        s = lax.dot_general(d["q"], kr[0, c0:c0 + n, :], dn, preferred_element_type=F32)
        ss_refs[u % 2][z, :, off:off + n] = s
        mj = jnp.max(s, axis=-1, keepdims=True)
        d["m"] = mj if j == 0 else jnp.maximum(d["m"], mj)

    def ev(u, j):
        x, c = divmod(u, 2)
        d = st[u]
        _, _, n, voff = chunks[j]
        p = jnp.exp2(ss_refs[u % 2][z, :, voff:voff + n] - d["m"]).astype(BF16)
        part = jnp.dot(p, vx_ref[voff:voff + n, :], preferred_element_type=F32)
        d["acc"] = part if j == 0 else d["acc"] + part
        if j == len(chunks) - 1:
            acc = d["acc"]
            on = acc[:, 0:DIFF_DV] * (1.0 / acc[:, DIFF_DV:DIFF_DV + 1])
            o1 = d.get("o1")
            d.clear()
            if c == 0:
                st[u + 1]["o1"] = on
            else:
                o = o1 - lam * on
                r = lax.rsqrt(jnp.mean(o * o, axis=-1, keepdims=True) + EPS)
                o_ref[0, x * sub:(x + 1) * sub, :] = (o * r * g_ref[...] * (1.0 - lam_init)).astype(o_ref.dtype)

    nch = len(chunks)
    for u in range(2 * nblk + 1):
        for j in range(nch):
            if u < 2 * nblk:
                qk(u, j)
            if u >= 1:
                ev(u - 1, j)


def _diff_attn(zq, zkv_extra, lams, subln_g, lam_init, *, tq, sub, kchunk):
    bsz, t_len, _ = zq.shape
    nh = DIFF_HEADS
    two = zkv_extra is not None
    n_all = t_len + (zkv_extra.shape[1] if two else 0)
    in_specs = [pl.BlockSpec((1, tq, LANES), lambda b, h, i: (b, i, h)),
                pl.BlockSpec((1, t_len, LANES), lambda b, h, i: (b, 0, nh + h)),
                pl.BlockSpec((1, t_len, LANES), lambda b, h, i: (b, 0, 2 * nh + h))]
    args = [zq, zq, zq]
    if two:
        c_len = zkv_extra.shape[1]
        in_specs += [pl.BlockSpec((1, c_len, LANES), lambda b, h, i: (b, 0, nh + h)),
                     pl.BlockSpec((1, c_len, LANES), lambda b, h, i: (b, 0, 2 * nh + h))]
        args += [zkv_extra, zkv_extra]
    for v in lams:
        in_specs.append(pl.BlockSpec((1, DIFF_DH), lambda b, h, i: (0, 0)))
        args.append(v.reshape(1, DIFF_DH))
    in_specs.append(pl.BlockSpec((1, DIFF_DV), lambda b, h, i: (0, 0)))
    args.append(subln_g.reshape(1, DIFF_DV))
    return pl.pallas_call(
        functools.partial(_attn_kernel, lam_init=lam_init, two_sources=two, sub=sub, kchunk=kchunk),
        grid=(bsz, nh, t_len // tq),
        in_specs=in_specs,
        out_specs=pl.BlockSpec((1, tq, LANES), lambda b, h, i: (b, i, h)),
        out_shape=jax.ShapeDtypeStruct((bsz, t_len, nh * DIFF_DV), BF16),
        scratch_shapes=[pltpu.VMEM((n_all, 2 * DIFF_DV), BF16), pltpu.VMEM((1, sub, n_all), F32),
                        pltpu.VMEM((1, sub, n_all), F32)],
        compiler_params=_params(("parallel", "parallel", "arbitrary")),
        name="diff_attn" + ("_lat" if two else "_ctx"),
    )(*args)


SUBLANES = 8


def _dwconv_phases(width):
    offs = [HALO - width // 2 + w for w in range(width)]
    return sorted({o % SUBLANES for o in offs} - {0}), max(offs) // SUBLANES * SUBLANES


def _dwconv_scratch(tt, width, nch):
    phases, reach = _dwconv_phases(width)
    return pltpu.VMEM((max(len(phases), 1), tt + reach, nch), F32)


def _dwconv_into(y_ref, ysh_ref, w_ref, b_ref, out_ref, *, tt, width, nch):
    phases, reach = _dwconv_phases(width)
    for idx, p in enumerate(phases):
        ysh_ref[idx] = y_ref[p:p + tt + reach, :]
    rc = 64
    for c0 in range(0, nch, LANES):
        for r0 in range(0, tt, rc):
            acc = jnp.broadcast_to(b_ref[:, c0:c0 + LANES], (rc, LANES))
            for w in range(width):
                off = HALO - width // 2 + w
                p, start = off % SUBLANES, r0 + off // SUBLANES * SUBLANES
                if p == 0:
                    tap = y_ref[start:start + rc, c0:c0 + LANES]
                else:
                    tap = ysh_ref[phases.index(p), start:start + rc, c0:c0 + LANES]
                acc = acc + tap * w_ref[w:w + 1, c0:c0 + LANES]
            out_ref[r0:r0 + rc, c0:c0 + LANES] = acc


def _convmod_kernel(ap_ref, a_ref, an_ref, gp_ref, g_ref, gn_ref, cw_ref, cb_ref, lng_ref, lnb_ref, o_ref,
                    y_ref, c_ref, ysh_ref, *, tt, nch):
    i = pl.program_id(1)
    n = pl.num_programs(1)

    def glu(a, g):
        return a.astype(F32) * _sigmoid(g.astype(F32))

    y_ref[0:HALO] = jnp.where(i > 0, glu(ap_ref[0], gp_ref[0]), 0.0)
    y_ref[HALO:HALO + tt] = glu(a_ref[0], g_ref[0])
    y_ref[HALO + tt:2 * HALO + tt] = jnp.where(i < n - 1, glu(an_ref[0], gn_ref[0]), 0.0)
    _dwconv_into(y_ref, ysh_ref, cw_ref, cb_ref, c_ref, tt=tt, width=CONV_W, nch=nch)
    c = c_ref[...]
    mu = jnp.mean(c, axis=-1, keepdims=True)
    xc = c - mu
    var = jnp.mean(xc * xc, axis=-1, keepdims=True)
    y = xc * lax.rsqrt(var + EPS) * lng_ref[...] + lnb_ref[...]
    o_ref[0] = (y * _sigmoid(y)).astype(o_ref.dtype)


def _halo_specs(tt, t_len, width, colblk):
    per = tt // HALO
    last = t_len // HALO - 1
    return (pl.BlockSpec((1, HALO, width), lambda b, i: (b, jnp.maximum(i * per - 1, 0), colblk)),
            pl.BlockSpec((1, tt, width), lambda b, i: (b, i, colblk)),
            pl.BlockSpec((1, HALO, width), lambda b, i: (b, jnp.minimum((i + 1) * per, last), colblk)))


def _conv_module(z, conv_w, conv_b, ln_g, ln_b, *, tt):
    bsz, t_len, n = z.shape
    nch = conv_w.shape[1]
    a_blk = (n - 2 * nch) // nch
    vec = lambda: pl.BlockSpec((1, nch), lambda b, i: (0, 0))
    return pl.pallas_call(
        functools.partial(_convmod_kernel, tt=tt, nch=nch),
        grid=(bsz, t_len // tt),
        in_specs=[*_halo_specs(tt, t_len, nch, a_blk), *_halo_specs(tt, t_len, nch, a_blk + 1),
                  pl.BlockSpec((CONV_W, nch), lambda b, i: (0, 0)), vec(), vec(), vec()],
        out_specs=pl.BlockSpec((1, tt, nch), lambda b, i: (b, i, 0)),
        out_shape=jax.ShapeDtypeStruct((bsz, t_len, nch), BF16),
        scratch_shapes=[pltpu.VMEM((tt + 2 * HALO, nch), F32), pltpu.VMEM((tt, nch), F32),
                        _dwconv_scratch(tt, CONV_W, nch)],
        compiler_params=_params(("parallel", "arbitrary")),
        name="conformer_conv",
    )(z, z, z, z, z, z, conv_w, conv_b.reshape(1, nch), ln_g.reshape(1, nch), ln_b.reshape(1, nch))


def _conv3_kernel(zp_ref, z_ref, zn_ref, cw_ref, cb_ref, o_ref, y_ref, c_ref, ysh_ref, *, tt, nch, kscale):
    i = pl.program_id(1)
    n = pl.num_programs(1)
    y_ref[0:HALO] = jnp.where(i > 0, zp_ref[0].astype(F32), 0.0)
    y_ref[HALO:HALO + tt] = z_ref[0].astype(F32)
    y_ref[HALO + tt:2 * HALO + tt] = jnp.where(i < n - 1, zn_ref[0].astype(F32), 0.0)
    _dwconv_into(y_ref, ysh_ref, cw_ref, cb_ref, c_ref, tt=tt, width=ML_CONV_W, nch=nch)
    c = c_ref[...]
    s = c * _sigmoid(c)
    half = nch // 2
    o_ref[0, :, 0:half] = s[:, 0:half].astype(o_ref.dtype)
    o_ref[0, :, half:nch] = (s[:, half:nch] * kscale).astype(o_ref.dtype)


def _conv3_silu(z, conv_w, conv_b, *, tt):
    bsz, t_len, _ = z.shape
    nch = conv_w.shape[1]
    return pl.pallas_call(
        functools.partial(_conv3_kernel, tt=tt, nch=nch, kscale=ML_DK ** -0.5),
        grid=(bsz, t_len // tt),
        in_specs=[*_halo_specs(tt, t_len, nch, 0),
                  pl.BlockSpec((ML_CONV_W, nch), lambda b, i: (0, 0)),
                  pl.BlockSpec((1, nch), lambda b, i: (0, 0))],
        out_specs=pl.BlockSpec((1, tt, nch), lambda b, i: (b, i, 0)),
        out_shape=jax.ShapeDtypeStruct((bsz, t_len, nch), BF16),
        scratch_shapes=[pltpu.VMEM((tt + 2 * HALO, nch), F32), pltpu.VMEM((tt, nch), F32),
                        _dwconv_scratch(tt, ML_CONV_W, nch)],
        compiler_params=_params(("parallel", "arbitrary")),
        name="ml_conv3_silu",
    )(z, z, z, conv_w, conv_b.reshape(1, nch))


def _post_residual(y, x, gate, gpost):
    r = lax.rsqrt(jnp.mean(y * y, axis=-1, keepdims=True) + EPS)
    return x + gate * (y * r * gpost)


def _hy_out_kernel(a_ref, c_ref, x_ref, gate_ref, gpost_ref, w_ref, o_ref):
    ka = a_ref.shape[1]
    y = (jnp.dot(a_ref[...], w_ref[0:ka, :], preferred_element_type=F32)
         + jnp.dot(c_ref[...], w_ref[ka:, :], preferred_element_type=F32))
    o_ref[...] = _post_residual(y, x_ref[...], gate_ref[0], gpost_ref[...])


def _hy_out(attn, conv, x, gate, gpost, w, *, tm, rows_per_mod):
    m, d = x.shape
    ka, kc = attn.shape[1], conv.shape[1]
    return pl.pallas_call(
        _hy_out_kernel,
        grid=(m // tm,),
        in_specs=[pl.BlockSpec((tm, ka), lambda i: (i, 0)),
                  pl.BlockSpec((tm, kc), lambda i: (i, 0)),
                  pl.BlockSpec((tm, d), lambda i: (i, 0)),
                  pl.BlockSpec((1, 1, d), lambda i: ((i * tm) // rows_per_mod, 0, 0)),
                  pl.BlockSpec((1, d), lambda i: (0, 0)),
                  pl.BlockSpec((ka + kc, d), lambda i: (0, 0), pipeline_mode=pl.Buffered(1))],
        out_specs=pl.BlockSpec((tm, d), lambda i: (i, 0)),
        out_shape=jax.ShapeDtypeStruct((m, d), F32),
        compiler_params=_params(("parallel",)),
        name="hybrid_out_proj",
    )(attn, conv, x, gate, gpost.reshape(1, d), w)


def _ml_out_kernel(hf_ref, hb_ref, og_ref, x_ref, ng_ref, gate_ref, gpost_ref, w_ref, o_ref):
    hsum = hf_ref[0].astype(F32) + hb_ref[0].astype(F32)
    parts = []
    for h in range(ML_HEADS):
        hh = hsum[:, h * ML_DV:(h + 1) * ML_DV]
        r = lax.rsqrt(jnp.mean(hh * hh, axis=-1, keepdims=True) + EPS)
        parts.append(hh * r * ng_ref[:, h * ML_DV:(h + 1) * ML_DV])
    hn = jnp.concatenate(parts, axis=1)
    a = (hn * _sigmoid(og_ref[...].astype(F32))).astype(BF16)
    y = jnp.dot(a, w_ref[...], preferred_element_type=F32)
    o_ref[...] = _post_residual(y, x_ref[...], gate_ref[0], gpost_ref[...])


def _ml_out(h2, z, o_blk, x, norm_g, gate, gpost, w, *, tm, rows_per_mod):
    m, d = x.shape
    kv = h2.shape[2]
    return pl.pallas_call(
        _ml_out_kernel,
        grid=(m // tm,),
        in_specs=[pl.BlockSpec((1, tm, kv), lambda i: (0, i, 0)),
                  pl.BlockSpec((1, tm, kv), lambda i: (1, i, 0)),
                  pl.BlockSpec((tm, kv), lambda i: (i, o_blk)),
                  pl.BlockSpec((tm, d), lambda i: (i, 0)),
                  pl.BlockSpec((1, kv), lambda i: (0, 0)),
                  pl.BlockSpec((1, 1, d), lambda i: ((i * tm) // rows_per_mod, 0, 0)),
                  pl.BlockSpec((1, d), lambda i: (0, 0)),
                  pl.BlockSpec((kv, d), lambda i: (0, 0), pipeline_mode=pl.Buffered(1))],
        out_specs=pl.BlockSpec((tm, d), lambda i: (i, 0)),
        out_shape=jax.ShapeDtypeStruct((m, d), F32),
        compiler_params=_params(("parallel",)),
        name="mlstm_out_proj",
    )(h2, h2, z, x, norm_g.reshape(1, kv), gate, gpost.reshape(1, d), w)


def _ffn_kernel(x_ref, g_ref, sh_ref, sc_ref, gate_ref, gpost_ref, w1_ref, w2_ref, o_ref, h_ref, acc_ref):
    f = pl.program_id(1)
    nf = pl.num_programs(1)

    @pl.when(f == 0)
    def _():
        h_ref[...] = _prenorm(x_ref[...], g_ref[...], sh_ref[0], sc_ref[0]).astype(BF16)
        acc_ref[...] = jnp.zeros_like(acc_ref)

    t = jnp.maximum(jnp.dot(h_ref[...], w1_ref[...], preferred_element_type=F32), 0.0)
    acc_ref[...] += jnp.dot((t * t).astype(BF16), w2_ref[...], preferred_element_type=F32)

    @pl.when(f == nf - 1)
    def _():
        o_ref[...] = _post_residual(acc_ref[...], x_ref[...], gate_ref[0], gpost_ref[...])


def _ffn(x, g, shift, scale, gate, gpost, w1, w2, *, tm, tf, rows_per_mod):
    m, d = x.shape
    dff = w1.shape[1]
    mod_map = lambda i, f: ((i * tm) // rows_per_mod, 0, 0)
    vec = lambda: pl.BlockSpec((1, d), lambda i, f: (0, 0))
    return pl.pallas_call(
        _ffn_kernel,
        grid=(m // tm, dff // tf),
        in_specs=[pl.BlockSpec((tm, d), lambda i, f: (i, 0)), vec(),
                  pl.BlockSpec((1, 1, d), mod_map), pl.BlockSpec((1, 1, d), mod_map),
                  pl.BlockSpec((1, 1, d), mod_map), vec(),
                  pl.BlockSpec((d, tf), lambda i, f: (0, f)),
                  pl.BlockSpec((tf, d), lambda i, f: (f, 0))],
        out_specs=pl.BlockSpec((tm, d), lambda i, f: (i, 0)),
        out_shape=jax.ShapeDtypeStruct((m, d), F32),
        scratch_shapes=[pltpu.VMEM((tm, d), BF16), pltpu.VMEM((tm, d), F32)],
        compiler_params=_params(("parallel", "arbitrary")),
        name="ffn",
    )(x, g.reshape(1, d), shift, scale, gate, gpost.reshape(1, d), w1, w2)


def _scan_kernel(ql_ref, kl_ref, vl_ref, gil_ref, gfl_ref, qc_ref, kc_ref, vc_ref, gic_ref, gfc_ref,
                 bi_ref, bf_ref, o_ref, c_ref, m_ref):
    L, H, DK, DV = SCAN_CHUNK, ML_HEADS, ML_DK, ML_DV
    fwd = pl.program_id(0) == 0
    s = pl.program_id(2)

    @pl.when(s == 0)
    def _():
        c_ref[...] = jnp.zeros_like(c_ref)
        m_ref[...] = jnp.zeros_like(m_ref)

    is_ctx = s == 0
    q = jnp.where(is_ctx, qc_ref[0], ql_ref[0])
    k = jnp.where(is_ctx, kc_ref[0], kl_ref[0])
    v = jnp.where(is_ctx, vc_ref[0], vl_ref[0])
    ipre = jnp.where(is_ctx, gic_ref[0, 0], gil_ref[0, 0]) + bi_ref[0]
    fpre = jnp.where(is_ctx, gfc_ref[0, 0], gfl_ref[0, 0]) + bf_ref[0]
    lf = jnp.minimum(fpre, 0.0) - jnp.log1p(jnp.exp(-jnp.abs(fpre)))

    row = lax.broadcasted_iota(jnp.int32, (L, L), 0)
    col = lax.broadcasted_iota(jnp.int32, (L, L), 1)
    delta = (row - col) * jnp.where(fwd, 1, -1)
    tri = jnp.where(delta <= 0, 1.0, 0.0).astype(F32)
    b_all = jnp.dot(lf, tri, precision=lax.Precision.HIGHEST, preferred_element_type=F32)
    r_all = ipre - b_all
    b_last_all = jnp.where(fwd, b_all[:, L - 1:L], b_all[:, 0:1])
    mask = delta >= 0
    ones_blk = (lax.broadcasted_iota(jnp.int32, (L, LANES), 1) == 0).astype(BF16)
    dn_t = (((1,), (1,)), ((), ()))

    for h in range(H):
        b_row = b_all[h:h + 1, :]
        r_row = r_all[h:h + 1, :]
        bcol = jnp.transpose(jnp.broadcast_to(b_row, (LANES, L)))
        bcol_l = jnp.concatenate([bcol] * (L // LANES), axis=1)
        dm = jnp.where(mask, bcol_l + r_row, -jnp.inf)
        m_loc = jnp.max(dm, axis=1, keepdims=True)
        m_prev = m_ref[h:h + 1, 0:1]
        g = bcol[:, 0:1] + m_prev
        m_row = jnp.maximum(g, m_loc)
        inter = jnp.exp(g - m_row)
        p = jnp.exp(dm - m_row)
        qh = q[:, h * DK:(h + 1) * DK]
        kh = k[:, h * DK:(h + 1) * DK]
        vext = jnp.concatenate([v[:, h * DV:(h + 1) * DV], ones_blk], axis=1)
        sm = (lax.dot_general(qh, kh, dn_t, preferred_element_type=F32) * p).astype(BF16)
        cst = c_ref[h]
        nd = (inter * jnp.dot(qh, cst.astype(BF16), preferred_element_type=F32)
              + jnp.dot(sm, vext, preferred_element_type=F32))
        den = jnp.maximum(jnp.abs(nd[:, DV:DV + 1]), jnp.exp(-m_row))
        o_ref[0, 0, :, h * DV:(h + 1) * DV] = (nd[:, 0:DV] * (1.0 / den)).astype(o_ref.dtype)

        b_last = b_last_all[h:h + 1, :]
        wlog = b_last + r_row
        m_new = jnp.maximum(b_last + m_prev, jnp.max(wlog, axis=1, keepdims=True))
        decay = jnp.exp(b_last + m_prev - m_new)
        kt = jnp.transpose(kh.astype(F32))
        ktw = (kt * jnp.exp(wlog - m_new)).astype(BF16)
        c_ref[h] = decay * cst + jnp.dot(ktw, vext, preferred_element_type=F32)
        m_ref[h:h + 1, :] = jnp.broadcast_to(m_new, (1, LANES))


def _mlstm_scan(qk_l, z_l, gates_l, qk_c, z_c, gates_c, gate_b):
    L, H = SCAN_CHUNK, ML_HEADS
    bsz, t_len, _ = qk_l.shape
    assert qk_c.shape[1] == L
    n_lat = t_len // L
    hk, hv = H * ML_DK, H * ML_DV

    def lat(d, s):
        return jnp.where(d == 0, jnp.maximum(s - 1, 0), n_lat - jnp.maximum(s, 1))

    in_specs = [
        pl.BlockSpec((1, L, hk), lambda d, b, s: (b, lat(d, s), 0)),
        pl.BlockSpec((1, L, hk), lambda d, b, s: (b, lat(d, s), 1)),
        pl.BlockSpec((1, L, hv), lambda d, b, s: (b, lat(d, s), 1)),
        pl.BlockSpec((1, 1, H, L), lambda d, b, s: (b, 2 * d, 0, lat(d, s))),
        pl.BlockSpec((1, 1, H, L), lambda d, b, s: (b, 2 * d + 1, 0, lat(d, s))),
        pl.BlockSpec((1, L, hk), lambda d, b, s: (b, 0, 0)),
        pl.BlockSpec((1, L, hk), lambda d, b, s: (b, 0, 1)),
        pl.BlockSpec((1, L, hv), lambda d, b, s: (b, 0, 1)),
        pl.BlockSpec((1, 1, H, L), lambda d, b, s: (b, 2 * d, 0, 0)),
        pl.BlockSpec((1, 1, H, L), lambda d, b, s: (b, 2 * d + 1, 0, 0)),
        pl.BlockSpec((1, H, 1), lambda d, b, s: (2 * d, 0, 0)),
        pl.BlockSpec((1, H, 1), lambda d, b, s: (2 * d + 1, 0, 0)),
    ]
    return pl.pallas_call(
        _scan_kernel,
        grid=(2, bsz, n_lat + 1),
        in_specs=in_specs,
        out_specs=pl.BlockSpec((1, 1, L, hv), lambda d, b, s: (d, b, lat(d, s), 0)),
        out_shape=jax.ShapeDtypeStruct((2, bsz, t_len, hv), BF16),
        scratch_shapes=[pltpu.VMEM((H, ML_DK, ML_DV + LANES), F32), pltpu.VMEM((H, LANES), F32)],
        compiler_params=_params(("parallel", "parallel", "arbitrary")),
        name="mlstm_scan",
    )(qk_l, qk_l, z_l, gates_l, gates_l, qk_c, qk_c, z_c, gates_c, gates_c,
      gate_b.reshape(4, H, 1), gate_b.reshape(4, H, 1))


def kernel(x, c, ctx, c_ctx, ada_w, ada_b, g_pre_mix, g_post_mix, g_pre_ffn, g_post_ffn, ffn_w1, ffn_w2, hy_w_in, hy_w_out, diff_lq1, diff_lk1, diff_lq2, diff_lk2, diff_subln_g, conv_w, conv_b, conv_ln_g, conv_ln_b, ml_w_in, ml_conv_w, ml_conv_b, ml_gate_b, ml_norm_g, ml_w_out):
    bsz, t_len, d = x.shape
    c_len = ctx.shape[1]
    depth = ada_w.shape[0]
    assert depth == 2 and bsz <= 7 and c_len == SCAN_CHUNK
    ml = t_len * bsz
    mc = c_len * bsz

    cvec = jnp.concatenate([c, c_ctx[None, :], jnp.zeros((8 - bsz - 1, d), F32)], axis=0)
    mods = _ada(cvec, ada_w, ada_b)

    def mod_lat(l, k):
        return mods[l, :bsz, k * d:(k + 1) * d].reshape(bsz, 1, d)

    def mod_ctx(l, k):
        return mods[l, bsz:bsz + 1, k * d:(k + 1) * d].reshape(1, 1, d)

    xl = x.reshape(ml, d)
    xc = ctx.reshape(mc, d)
    w1 = [ffn_w1[i].astype(BF16) for i in range(depth)]
    w2 = [ffn_w2[i].astype(BF16) for i in range(depth)]

    l = 0
    lam_init = 0.8 - 0.6 * math.exp(-0.3 * l)
    w_in = hy_w_in[0].astype(BF16)
    n_in = w_in.shape[1]
    qk_w = 2 * DIFF_HEADS * 2 * DIFF_DH
    rope = _rope_tables(t_len)
    zl = _prenorm_mm(xl, g_pre_mix[l], mod_lat(l, 0), mod_lat(l, 1), w_in, n_in, tm=1024, tn=1024,
                     rows_per_mod=t_len, out_dtype=BF16, rope=rope, rope_cols=qk_w, qscale_cols=qk_w // 2,
                     qscale=ATTN_QSCALE, name="hy_in_proj_lat")
    zc = _prenorm_mm(xc, g_pre_mix[l], mod_ctx(l, 0), mod_ctx(l, 1), w_in, n_in, tm=mc, tn=1024,
                     rows_per_mod=mc, out_dtype=BF16, qscale_cols=qk_w // 2, qscale=ATTN_QSCALE,
                     name="hy_in_proj_ctx")
    zl3 = zl.reshape(bsz, t_len, n_in)
    zc3 = zc.reshape(bsz, c_len, n_in)
    lams = (diff_lq1[0], diff_lk1[0], diff_lq2[0], diff_lk2[0])
    attn_l = _diff_attn(zl3, zc3, lams, diff_subln_g[0], lam_init, tq=1024, sub=256, kchunk=512)
    attn_c = _diff_attn(zc3, None, lams, diff_subln_g[0], lam_init, tq=c_len, sub=c_len, kchunk=c_len)
    conv_l = _conv_module(zl3, conv_w[0], conv_b[0], conv_ln_g[0], conv_ln_b[0], tt=256)
    conv_c = _conv_module(zc3, conv_w[0], conv_b[0], conv_ln_g[0], conv_ln_b[0], tt=c_len)
    w_out = hy_w_out[0].astype(BF16)
    xl = _hy_out(attn_l.reshape(ml, -1), conv_l.reshape(ml, -1), xl, mod_lat(l, 2), g_post_mix[l], w_out,
                 tm=512, rows_per_mod=t_len)
    xc = _hy_out(attn_c.reshape(mc, -1), conv_c.reshape(mc, -1), xc, mod_ctx(l, 2), g_post_mix[l], w_out,
                 tm=512, rows_per_mod=mc)
    xl = _ffn(xl, g_pre_ffn[l], mod_lat(l, 3), mod_lat(l, 4), mod_lat(l, 5), g_post_ffn[l], w1[l], w2[l],
              tm=512, tf=1024, rows_per_mod=t_len)
    xc = _ffn(xc, g_pre_ffn[l], mod_ctx(l, 3), mod_ctx(l, 4), mod_ctx(l, 5), g_post_ffn[l], w1[l], w2[l],
              tm=512, tf=1024, rows_per_mod=mc)

    l = 1
    qkv_w = 2 * ML_HEADS * ML_DK + ML_HEADS * ML_DV
    n_gates = 4 * ML_HEADS
    wm = ml_w_in[0].astype(BF16)
    w_main = jnp.concatenate([wm[:, :qkv_w], wm[:, qkv_w + n_gates:]], axis=1)
    w_gate = jnp.pad(wm[:, qkv_w:qkv_w + n_gates], ((0, 0), (0, LANES - n_gates)))
    n_main = w_main.shape[1]
    zl, gl = _prenorm_mm(xl, g_pre_mix[l], mod_lat(l, 0), mod_lat(l, 1), w_main, n_main, tm=1024, tn=2048,
                         rows_per_mod=t_len, out_dtype=BF16, w_side=w_gate, name="ml_in_proj_lat")
    zc, gc = _prenorm_mm(xc, g_pre_mix[l], mod_ctx(l, 0), mod_ctx(l, 1), w_main, qkv_w, tm=mc, tn=2048,
                         rows_per_mod=mc, out_dtype=BF16, w_side=w_gate, name="ml_in_proj_ctx")
    zl3 = zl.reshape(bsz, t_len, n_main)
    zc3 = zc.reshape(bsz, c_len, qkv_w)
    qk_l = _conv3_silu(zl3, ml_conv_w[0], ml_conv_b[0], tt=256)
    qk_c = _conv3_silu(zc3, ml_conv_w[0], ml_conv_b[0], tt=c_len)

    def gates_t(gm, n):
        return gm[:, :n_gates].reshape(bsz, n, 4, ML_HEADS).transpose(0, 2, 3, 1)

    h2 = _mlstm_scan(qk_l, zl3, gates_t(gl, t_len), qk_c, zc3, gates_t(gc, c_len), ml_gate_b[0])
    xl = _ml_out(h2.reshape(2, ml, -1), zl, qkv_w // (ML_HEADS * ML_DV), xl, ml_norm_g[0], mod_lat(l, 2),
                 g_post_mix[l], ml_w_out[0].astype(BF16), tm=512, rows_per_mod=t_len)
    xl = _ffn(xl, g_pre_ffn[l], mod_lat(l, 3), mod_lat(l, 4), mod_lat(l, 5), g_post_ffn[l], w1[l], w2[l],
              tm=512, tf=1024, rows_per_mod=t_len)
    return xl.reshape(bsz, t_len, d)
```

```python
import functools
import math

import jax
import jax.numpy as jnp
from jax import lax
from jax.experimental import pallas as pl
from jax.experimental.pallas import tpu as pltpu

F32 = jnp.float32
BF16 = jnp.bfloat16

EPS = 1e-6
ROPE_BASE = 10000.0
GRID_W = 64
LANES = 128
SUBLANES = 8
HALO = 16
DIFF_DH = 64
DIFF_HEADS = 8
DIFF_DV = 128
CONV_W = 31
ML_HEADS = 8
ML_DK = 128
ML_DV = 256
ML_CONV_W = 3
SCAN_CHUNK = 256
ATTN_QSCALE = DIFF_DH ** -0.5 * math.log2(math.e)
VMEM_LIMIT = 56 * 1024 * 1024


def _params(sem, flags=None):
    return pltpu.CompilerParams(dimension_semantics=sem, vmem_limit_bytes=VMEM_LIMIT, flags=flags)


def _sigmoid(x):
    return 1.0 / (1.0 + jnp.exp(-x))


def _runtime_zero():
    return jnp.minimum(pl.program_id(0), 0)


def _ada_kernel(c_ref, w_ref, b_ref, o_ref):
    c = c_ref[...]
    s = (c * _sigmoid(c)).astype(BF16)
    o_ref[0] = jnp.dot(s, w_ref[0].astype(BF16), preferred_element_type=F32) + b_ref[0]


def _ada(cvec, ada_w, ada_b):
    depth, d, n = ada_w.shape
    tn = 1024
    return pl.pallas_call(
        _ada_kernel,
        grid=(depth, n // tn),
        in_specs=[pl.BlockSpec((8, d), lambda l, j: (0, 0)),
                  pl.BlockSpec((1, d, tn), lambda l, j: (l, 0, j)),
                  pl.BlockSpec((1, 1, tn), lambda l, j: (l, 0, j))],
        out_specs=pl.BlockSpec((1, 8, tn), lambda l, j: (l, 0, j)),
        out_shape=jax.ShapeDtypeStruct((depth, 8, n), F32),
        compiler_params=_params(("parallel", "parallel")),
        name="ada_ln",
    )(cvec, ada_w, ada_b.reshape(depth, 1, n))


ROW_CHUNK = 16


def _prenorm_into(h_ref, x_ref, g, shift, scale):
    gs = g * (1.0 + scale)
    for r0 in range(0, x_ref.shape[0], ROW_CHUNK):
        x = x_ref[r0:r0 + ROW_CHUNK, :]
        r = lax.rsqrt(jnp.mean(x * x, axis=-1, keepdims=True) + EPS)
        h_ref[r0:r0 + ROW_CHUNK, :] = (x * r * gs + shift).astype(h_ref.dtype)


def _prenorm_mm_kernel(*refs, rope_blocks, tn, qscale_blocks, qscale, has_side):
    x_ref, g_ref, sh_ref, sc_ref, w_ref = refs[:5]
    rest = list(refs[5:])
    if rope_blocks:
        cos_ref, sina_ref, sinb_ref = rest[:3]
        rest = rest[3:]
    if has_side:
        ws_ref, o_ref, os_ref, h_ref = rest
    else:
        o_ref, h_ref = rest
    j = pl.program_id(1)

    @pl.when(j == 0)
    def _():
        _prenorm_into(h_ref, x_ref, g_ref[...], sh_ref[0], sc_ref[0])
        if has_side:
            os_ref[...] = jnp.dot(h_ref[...], ws_ref[...], preferred_element_type=F32)

    acc = jnp.dot(h_ref[...], w_ref[...], preferred_element_type=F32)
    if qscale_blocks:
        acc = acc * jnp.where(j < qscale_blocks, qscale, 1.0)

    if rope_blocks:
        cos = cos_ref[0]
        sina = sina_ref[0]
        sinb = sinb_ref[0]
        for c in range(tn // LANES):
            xs = acc[:, c * LANES:(c + 1) * LANES]
            rot = pltpu.roll(xs, LANES - 16, 1) * sina + pltpu.roll(xs, 16, 1) * sinb
            o_ref[:, c * LANES:(c + 1) * LANES] = (xs * cos + rot).astype(o_ref.dtype)
    else:
        o_ref[...] = acc.astype(o_ref.dtype)


def _prenorm_mm(x, g, shift, scale, w, n_out, *, tm, tn, rows_per_mod, out_dtype, rope=None, rope_cols=0, qscale_cols=0,
                qscale=1.0, w_side=None, name):
    m, d = x.shape
    mod_map = lambda i, j: ((i * tm) // rows_per_mod, 0, 0)
    in_specs = [pl.BlockSpec((tm, d), lambda i, j: (i, 0)),
                pl.BlockSpec((1, d), lambda i, j: (0, 0)),
                pl.BlockSpec((1, 1, d), mod_map),
                pl.BlockSpec((1, 1, d), mod_map),
                pl.BlockSpec((d, tn), lambda i, j: (0, j))]
    args = [x, g.reshape(1, d), shift, scale, w]
    rope_blocks = 0
    if rope is not None:
        assert rope_cols % tn == 0
        rope_blocks = rope_cols // tn
        t_len = rope[0].shape[0]
        assert t_len % tm == 0
        nblk = t_len // tm
        for tab, ident in zip(rope, (1.0, 0.0, 0.0)):
            in_specs.append(pl.BlockSpec((1, tm, LANES), lambda i, j: (jnp.where(j < rope_blocks, 0, 1), i % nblk, 0)))
            args.append(jnp.stack([tab, jnp.full_like(tab, ident)]))
    out_specs = pl.BlockSpec((tm, tn), lambda i, j: (i, j))
    out_shape = jax.ShapeDtypeStruct((m, n_out), out_dtype)
    if w_side is not None:
        ns = w_side.shape[1]
        in_specs.append(pl.BlockSpec((d, ns), lambda i, j: (0, 0)))
        args.append(w_side)
        out_specs = (out_specs, pl.BlockSpec((tm, ns), lambda i, j: (i, 0)))
        out_shape = (out_shape, jax.ShapeDtypeStruct((m, ns), F32))
    return pl.pallas_call(
        functools.partial(_prenorm_mm_kernel, rope_blocks=rope_blocks, tn=tn, qscale_blocks=qscale_cols // tn,
                          qscale=qscale, has_side=w_side is not None),
        grid=(m // tm, n_out // tn),
        in_specs=in_specs,
        out_specs=out_specs,
        out_shape=out_shape,
        scratch_shapes=[pltpu.VMEM((tm, d), BF16)],
        compiler_params=_params(("parallel", "arbitrary")),
        name=name,
    )(*args)


def _rope_tables(t_len):
    rows = t_len // GRID_W
    r = jnp.repeat(jnp.arange(rows, dtype=F32), GRID_W)
    col = jnp.tile(jnp.arange(GRID_W, dtype=F32), rows)
    n_freq = DIFF_DH // 4
    inv = ROPE_BASE ** (-jnp.arange(n_freq, dtype=F32) / n_freq)
    ar = r[:, None] * inv
    ac = col[:, None] * inv
    ang = jnp.concatenate([ar, ar, ac, ac], axis=-1)
    cos = jnp.tile(jnp.cos(ang), (1, LANES // DIFF_DH))
    sin = jnp.tile(jnp.sin(ang), (1, LANES // DIFF_DH))
    first = (jnp.arange(LANES) % (2 * n_freq)) < n_freq
    return cos, jnp.where(first, -sin, 0.0), jnp.where(first, 0.0, sin)


def _attn_kernel(*refs, lam_init, two_sources, sub, kchunk):
    if two_sources:
        q_ref, k_ref, v_ref, kc_ref, vc_ref, lq1_ref, lk1_ref, lq2_ref, lk2_ref, g_ref, o_ref, vx_ref, *ss_refs = refs
    else:
        q_ref, k_ref, v_ref, lq1_ref, lk1_ref, lq2_ref, lk2_ref, g_ref, o_ref, vx_ref, *ss_refs = refs
    z = _runtime_zero()
    lam = (jnp.exp(jnp.sum(lq1_ref[...] * lk1_ref[...], axis=-1, keepdims=True))
           - jnp.exp(jnp.sum(lq2_ref[...] * lk2_ref[...], axis=-1, keepdims=True)) + lam_init)
    dn = (((1,), (1,)), ((), ()))
    tq = q_ref.shape[1]
    n_keys = k_ref.shape[1]
    chunks = [(k_ref, c0, min(kchunk, n_keys - c0), c0) for c0 in range(0, n_keys, kchunk)]
    if two_sources:
        chunks.append((kc_ref, 0, kc_ref.shape[1], n_keys))

    @pl.when(pl.program_id(2) == 0)
    def _():
        srcs = [(v_ref, 0)] + ([(vc_ref, n_keys)] if two_sources else [])
        for vr, off in srcs:
            n = vr.shape[1]
            vx_ref[off:off + n, 0:DIFF_DV] = vr[0]
            vx_ref[off:off + n, DIFF_DV:2 * DIFF_DV] = (
                lax.broadcasted_iota(jnp.int32, (n, DIFF_DV), 1) == 0).astype(BF16)

    nblk = tq // sub
    st = [dict() for _ in range(2 * nblk)]

    def qk(u, j):
        x, c = divmod(u, 2)
        d = st[u]
        if j == 0:
            q = q_ref[0, x * sub:(x + 1) * sub, :]
            lane = lax.broadcasted_iota(jnp.int32, q.shape, 1)
            keep = (lane < DIFF_DH) if c == 0 else (lane >= DIFF_DH)
            d["q"] = jnp.where(keep, q, jnp.zeros_like(q))
        kr, c0, n, off = chunks[j]
        s = lax.dot_general(d["q"], kr[0, c0:c0 + n, :], dn, preferred_element_type=F32)
        ss_refs[u % 2][z, :, off:off + n] = s
        mj = jnp.max(s, axis=-1, keepdims=True)
        d["m"] = mj if j == 0 else jnp.maximum(d["m"], mj)

    def ev(u, j):
        x, c = divmod(u, 2)
        d = st[u]
        _, _, n, voff = chunks[j]
        p = jnp.exp2(ss_refs[u % 2][z, :, voff:voff + n] - d["m"]).astype(BF16)
        part = jnp.dot(p, vx_ref[voff:voff + n, :], preferred_element_type=F32)
        d["acc"] = part if j == 0 else d["acc"] + part
        if j == len(chunks) - 1:
            acc = d["acc"]
            on = acc[:, 0:DIFF_DV] * (1.0 / acc[:, DIFF_DV:DIFF_DV + 1])
            o1 = d.get("o1")
            d.clear()
            if c == 0:
                st[u + 1]["o1"] = on
            else:
                o = o1 - lam * on
                r = lax.rsqrt(jnp.mean(o * o, axis=-1, keepdims=True) + EPS)
                o_ref[0, x * sub:(x + 1) * sub, :] = (o * r * g_ref[...] * (1.0 - lam_init)).astype(o_ref.dtype)

    nch = len(chunks)
    for u in range(2 * nblk + 1):
        for j in range(nch):
            if u < 2 * nblk:
                qk(u, j)
            if u >= 1:
                ev(u - 1, j)


def _diff_attn(zq, zkv_extra, lams, subln_g, lam_init, *, tq, sub, kchunk):
    bsz, t_len, _ = zq.shape
    nh = DIFF_HEADS
    two = zkv_extra is not None
    n_all = t_len + (zkv_extra.shape[1] if two else 0)
    in_specs = [pl.BlockSpec((1, tq, LANES), lambda b, h, i: (b, i, h)),
                pl.BlockSpec((1, t_len, LANES), lambda b, h, i: (b, 0, nh + h)),
                pl.BlockSpec((1, t_len, LANES), lambda b, h, i: (b, 0, 2 * nh + h))]
    args = [zq, zq, zq]
    if two:
        c_len = zkv_extra.shape[1]
        in_specs += [pl.BlockSpec((1, c_len, LANES), lambda b, h, i: (b, 0, nh + h)),
                     pl.BlockSpec((1, c_len, LANES), lambda b, h, i: (b, 0, 2 * nh + h))]
        args += [zkv_extra, zkv_extra]
    for v in lams:
        in_specs.append(pl.BlockSpec((1, DIFF_DH), lambda b, h, i: (0, 0)))
        args.append(v.reshape(1, DIFF_DH))
    in_specs.append(pl.BlockSpec((1, DIFF_DV), lambda b, h, i: (0, 0)))
    args.append(subln_g.reshape(1, DIFF_DV))
    return pl.pallas_call(
        functools.partial(_attn_kernel, lam_init=lam_init, two_sources=two, sub=sub, kchunk=kchunk),
        grid=(bsz, nh, t_len // tq),
        in_specs=in_specs,
        out_specs=pl.BlockSpec((1, tq, LANES), lambda b, h, i: (b, i, h)),
        out_shape=jax.ShapeDtypeStruct((bsz, t_len, nh * DIFF_DV), BF16),
        scratch_shapes=[pltpu.VMEM((n_all, 2 * DIFF_DV), BF16), pltpu.VMEM((1, sub, n_all), F32),
                        pltpu.VMEM((1, sub, n_all), F32)],
        compiler_params=_params(("parallel", "parallel", "arbitrary")),
        name="diff_attn" + ("_lat" if two else "_ctx"),
    )(*args)


def _dwconv_phases(width):
    offs = [HALO - width // 2 + w for w in range(width)]
    return sorted({o % SUBLANES for o in offs} - {0}), max(offs) // SUBLANES * SUBLANES


def _dwconv_scratch(tt, width, nch):
    phases, reach = _dwconv_phases(width)
    return pltpu.VMEM((max(len(phases), 1), tt + reach, nch), F32)


def _dwconv_into(y_ref, ysh_ref, w_ref, b_ref, out_ref, *, tt, width, nch):
    phases, reach = _dwconv_phases(width)
    for idx, p in enumerate(phases):
        ysh_ref[idx] = y_ref[p:p + tt + reach, :]
    rc = 64
    for c0 in range(0, nch, LANES):
        for r0 in range(0, tt, rc):
            acc = jnp.broadcast_to(b_ref[:, c0:c0 + LANES], (rc, LANES))
            for w in range(width):
                off = HALO - width // 2 + w
                p, start = off % SUBLANES, r0 + off // SUBLANES * SUBLANES
                if p == 0:
                    tap = y_ref[start:start + rc, c0:c0 + LANES]
                else:
                    tap = ysh_ref[phases.index(p), start:start + rc, c0:c0 + LANES]
                acc = acc + tap * w_ref[w:w + 1, c0:c0 + LANES]
            out_ref[r0:r0 + rc, c0:c0 + LANES] = acc


def _convmod_kernel(ap_ref, a_ref, an_ref, gp_ref, g_ref, gn_ref, cw_ref, cb_ref, lng_ref, lnb_ref, o_ref,
                    y_ref, c_ref, ysh_ref, *, tt, nch):
    i = pl.program_id(1)
    n = pl.num_programs(1)

    def glu(a, g):
        return a.astype(F32) * _sigmoid(g.astype(F32))

    y_ref[0:HALO] = jnp.where(i > 0, glu(ap_ref[0], gp_ref[0]), 0.0)
    y_ref[HALO:HALO + tt] = glu(a_ref[0], g_ref[0])
    y_ref[HALO + tt:2 * HALO + tt] = jnp.where(i < n - 1, glu(an_ref[0], gn_ref[0]), 0.0)
    _dwconv_into(y_ref, ysh_ref, cw_ref, cb_ref, c_ref, tt=tt, width=CONV_W, nch=nch)
    c = c_ref[...]
    mu = jnp.mean(c, axis=-1, keepdims=True)
    xc = c - mu
    var = jnp.mean(xc * xc, axis=-1, keepdims=True)
    y = xc * lax.rsqrt(var + EPS) * lng_ref[...] + lnb_ref[...]
    o_ref[0] = (y * _sigmoid(y)).astype(o_ref.dtype)


def _halo_specs(tt, t_len, width, colblk):
    per = tt // HALO
    last = t_len // HALO - 1
    return (pl.BlockSpec((1, HALO, width), lambda b, i: (b, jnp.maximum(i * per - 1, 0), colblk)),
            pl.BlockSpec((1, tt, width), lambda b, i: (b, i, colblk)),
            pl.BlockSpec((1, HALO, width), lambda b, i: (b, jnp.minimum((i + 1) * per, last), colblk)))


def _conv_module(z, conv_w, conv_b, ln_g, ln_b, *, tt):
    bsz, t_len, n = z.shape
    nch = conv_w.shape[1]
    a_blk = (n - 2 * nch) // nch
    vec = lambda: pl.BlockSpec((1, nch), lambda b, i: (0, 0))
    return pl.pallas_call(
        functools.partial(_convmod_kernel, tt=tt, nch=nch),
        grid=(bsz, t_len // tt),
        in_specs=[*_halo_specs(tt, t_len, nch, a_blk), *_halo_specs(tt, t_len, nch, a_blk + 1),
                  pl.BlockSpec((CONV_W, nch), lambda b, i: (0, 0)), vec(), vec(), vec()],
        out_specs=pl.BlockSpec((1, tt, nch), lambda b, i: (b, i, 0)),
        out_shape=jax.ShapeDtypeStruct((bsz, t_len, nch), BF16),
        scratch_shapes=[pltpu.VMEM((tt + 2 * HALO, nch), F32), pltpu.VMEM((tt, nch), F32),
                        _dwconv_scratch(tt, CONV_W, nch)],
        compiler_params=_params(("parallel", "arbitrary")),
        name="conformer_conv",
    )(z, z, z, z, z, z, conv_w, conv_b.reshape(1, nch), ln_g.reshape(1, nch), ln_b.reshape(1, nch))


def _conv3_kernel(zp_ref, z_ref, zn_ref, cw_ref, cb_ref, o_ref, y_ref, c_ref, ysh_ref, *, tt, nch, kscale):
    i = pl.program_id(1)
    n = pl.num_programs(1)
    y_ref[0:HALO] = jnp.where(i > 0, zp_ref[0].astype(F32), 0.0)
    y_ref[HALO:HALO + tt] = z_ref[0].astype(F32)
    y_ref[HALO + tt:2 * HALO + tt] = jnp.where(i < n - 1, zn_ref[0].astype(F32), 0.0)
    _dwconv_into(y_ref, ysh_ref, cw_ref, cb_ref, c_ref, tt=tt, width=ML_CONV_W, nch=nch)
    c = c_ref[...]
    s = c * _sigmoid(c)
    half = nch // 2
    o_ref[0, :, 0:half] = s[:, 0:half].astype(o_ref.dtype)
    o_ref[0, :, half:nch] = (s[:, half:nch] * kscale).astype(o_ref.dtype)


def _conv3_silu(z, conv_w, conv_b, *, tt):
    bsz, t_len, _ = z.shape
    nch = conv_w.shape[1]
    return pl.pallas_call(
        functools.partial(_conv3_kernel, tt=tt, nch=nch, kscale=ML_DK ** -0.5),
        grid=(bsz, t_len // tt),
        in_specs=[*_halo_specs(tt, t_len, nch, 0),
                  pl.BlockSpec((ML_CONV_W, nch), lambda b, i: (0, 0)),
                  pl.BlockSpec((1, nch), lambda b, i: (0, 0))],
        out_specs=pl.BlockSpec((1, tt, nch), lambda b, i: (b, i, 0)),
        out_shape=jax.ShapeDtypeStruct((bsz, t_len, nch), BF16),
        scratch_shapes=[pltpu.VMEM((tt + 2 * HALO, nch), F32), pltpu.VMEM((tt, nch), F32),
                        _dwconv_scratch(tt, ML_CONV_W, nch)],
        compiler_params=_params(("parallel", "arbitrary")),
        name="ml_conv3_silu",
    )(z, z, z, conv_w, conv_b.reshape(1, nch))


def _post_residual(y, x, gate, gpost):
    r = lax.rsqrt(jnp.mean(y * y, axis=-1, keepdims=True) + EPS)
    return x + gate * (y * r * gpost)


OUT_SUB = 256


def _hy_out_kernel(a_ref, c_ref, x_ref, gate_ref, gpost_ref, w_ref, o_ref):
    ka = a_ref.shape[1]
    ys = []
    for r0 in range(0, a_ref.shape[0], OUT_SUB):
        rows = slice(r0, r0 + OUT_SUB)
        ys.append(jnp.dot(a_ref[rows, :], w_ref[0:ka, :], preferred_element_type=F32)
                  + jnp.dot(c_ref[rows, :], w_ref[ka:, :], preferred_element_type=F32))
        if len(ys) > 1:
            prev = slice(r0 - OUT_SUB, r0)
            o_ref[prev, :] = _post_residual(ys[-2], x_ref[prev, :], gate_ref[0], gpost_ref[...])
    last = slice(a_ref.shape[0] - OUT_SUB, a_ref.shape[0])
    o_ref[last, :] = _post_residual(ys[-1], x_ref[last, :], gate_ref[0], gpost_ref[...])


def _hy_out(attn, conv, x, gate, gpost, w, *, tm, rows_per_mod):
    m, d = x.shape
    ka, kc = attn.shape[1], conv.shape[1]
    return pl.pallas_call(
        _hy_out_kernel,
        grid=(m // tm,),
        in_specs=[pl.BlockSpec((tm, ka), lambda i: (i, 0)),
                  pl.BlockSpec((tm, kc), lambda i: (i, 0)),
                  pl.BlockSpec((tm, d), lambda i: (i, 0)),
                  pl.BlockSpec((1, 1, d), lambda i: ((i * tm) // rows_per_mod, 0, 0)),
                  pl.BlockSpec((1, d), lambda i: (0, 0)),
                  pl.BlockSpec((ka + kc, d), lambda i: (0, 0), pipeline_mode=pl.Buffered(1))],
        out_specs=pl.BlockSpec((tm, d), lambda i: (i, 0)),
        out_shape=jax.ShapeDtypeStruct((m, d), F32),
        compiler_params=_params(("parallel",)),
        name="hybrid_out_proj",
    )(attn, conv, x, gate, gpost.reshape(1, d), w)


def _ml_out_kernel(hf_ref, hb_ref, og_ref, x_ref, ng_ref, gate_ref, gpost_ref, w_ref, o_ref):
    def gated(rows):
        hsum = hf_ref[0, rows, :].astype(F32) + hb_ref[0, rows, :].astype(F32)
        parts = []
        for h in range(ML_HEADS):
            hh = hsum[:, h * ML_DV:(h + 1) * ML_DV]
            r = lax.rsqrt(jnp.mean(hh * hh, axis=-1, keepdims=True) + EPS)
            parts.append(hh * r * ng_ref[:, h * ML_DV:(h + 1) * ML_DV])
        hn = jnp.concatenate(parts, axis=1)
        return (hn * _sigmoid(og_ref[rows, :].astype(F32))).astype(BF16)

    subs = [slice(r0, r0 + OUT_SUB) for r0 in range(0, x_ref.shape[0], OUT_SUB)]
    a = gated(subs[0])
    y_prev = None
    for idx, rows in enumerate(subs):
        y = jnp.dot(a, w_ref[...], preferred_element_type=F32)
        if idx + 1 < len(subs):
            a = gated(subs[idx + 1])
        if y_prev is not None:
            prev = subs[idx - 1]
            o_ref[prev, :] = _post_residual(y_prev, x_ref[prev, :], gate_ref[0], gpost_ref[...])
        y_prev = y
    o_ref[subs[-1], :] = _post_residual(y_prev, x_ref[subs[-1], :], gate_ref[0], gpost_ref[...])


def _ml_out(h2, z, o_blk, x, norm_g, gate, gpost, w, *, tm, rows_per_mod):
    m, d = x.shape
    kv = h2.shape[2]
    return pl.pallas_call(
        _ml_out_kernel,
        grid=(m // tm,),
        in_specs=[pl.BlockSpec((1, tm, kv), lambda i: (0, i, 0)),
                  pl.BlockSpec((1, tm, kv), lambda i: (1, i, 0)),
                  pl.BlockSpec((tm, kv), lambda i: (i, o_blk)),
                  pl.BlockSpec((tm, d), lambda i: (i, 0)),
                  pl.BlockSpec((1, kv), lambda i: (0, 0)),
                  pl.BlockSpec((1, 1, d), lambda i: ((i * tm) // rows_per_mod, 0, 0)),
                  pl.BlockSpec((1, d), lambda i: (0, 0)),
                  pl.BlockSpec((kv, d), lambda i: (0, 0), pipeline_mode=pl.Buffered(1))],
        out_specs=pl.BlockSpec((tm, d), lambda i: (i, 0)),
        out_shape=jax.ShapeDtypeStruct((m, d), F32),
        compiler_params=_params(("parallel",)),
        name="mlstm_out_proj",
    )(h2, h2, z, x, norm_g.reshape(1, kv), gate, gpost.reshape(1, d), w)


def _ffn_kernel(x_ref, g_ref, sh_ref, sc_ref, gate_ref, gpost_ref, w1_ref, w2_ref, o_ref, h_ref, acc_ref):
    f = pl.program_id(1)
    nf = pl.num_programs(1)

    @pl.when(f == 0)
    def _():
        _prenorm_into(h_ref, x_ref, g_ref[...], sh_ref[0], sc_ref[0])
        acc_ref[...] = jnp.zeros_like(acc_ref)

    t = jnp.maximum(jnp.dot(h_ref[...], w1_ref[0], preferred_element_type=F32), 0.0)
    acc_ref[...] += jnp.dot((t * t).astype(BF16), w2_ref[0], preferred_element_type=F32)

    @pl.when(f == nf - 1)
    def _():
        for r0 in range(0, x_ref.shape[0], ROW_CHUNK):
            rows = slice(r0, r0 + ROW_CHUNK)
            o_ref[rows, :] = _post_residual(acc_ref[rows, :], x_ref[rows, :], gate_ref[0], gpost_ref[...])


def _ffn(x, g, shift, scale, gate, gpost, w1, w2, layer, *, tm, tf, rows_per_mod):
    m, d = x.shape
    dff = w1.shape[2]
    mod_map = lambda i, f: ((i * tm) // rows_per_mod, 0, 0)
    vec = lambda: pl.BlockSpec((1, d), lambda i, f: (0, 0))
    return pl.pallas_call(
        _ffn_kernel,
        grid=(m // tm, dff // tf),
        in_specs=[pl.BlockSpec((tm, d), lambda i, f: (i, 0)), vec(),
                  pl.BlockSpec((1, 1, d), mod_map), pl.BlockSpec((1, 1, d), mod_map),
                  pl.BlockSpec((1, 1, d), mod_map), vec(),
                  pl.BlockSpec((1, d, tf), lambda i, f: (layer, 0, f)),
                  pl.BlockSpec((1, tf, d), lambda i, f: (layer, f, 0))],
        out_specs=pl.BlockSpec((tm, d), lambda i, f: (i, 0)),
        out_shape=jax.ShapeDtypeStruct((m, d), F32),
        scratch_shapes=[pltpu.VMEM((tm, d), BF16), pltpu.VMEM((tm, d), F32)],
        compiler_params=_params(("parallel", "arbitrary")),
        name="ffn",
    )(x, g.reshape(1, d), shift, scale, gate, gpost.reshape(1, d), w1, w2)


def _scan_kernel(ql_ref, kl_ref, vl_ref, gil_ref, gfl_ref, qc_ref, kc_ref, vc_ref, gic_ref, gfc_ref,
                 bi_ref, bf_ref, o_ref, c_ref, m_ref):
    L, H, DK, DV = SCAN_CHUNK, ML_HEADS, ML_DK, ML_DV
    fwd = pl.program_id(0) == 0
    s = pl.program_id(2)

    @pl.when(s == 0)
    def _():
        c_ref[...] = jnp.zeros_like(c_ref)
        m_ref[...] = jnp.zeros_like(m_ref)

    is_ctx = s == 0
    q = jnp.where(is_ctx, qc_ref[0], ql_ref[0])
    k = jnp.where(is_ctx, kc_ref[0], kl_ref[0])
    v = jnp.where(is_ctx, vc_ref[0], vl_ref[0])
    ipre = jnp.where(is_ctx, gic_ref[0, 0], gil_ref[0, 0]) + bi_ref[0]
    fpre = jnp.where(is_ctx, gfc_ref[0, 0], gfl_ref[0, 0]) + bf_ref[0]
    lf = jnp.minimum(fpre, 0.0) - jnp.log1p(jnp.exp(-jnp.abs(fpre)))

    row = lax.broadcasted_iota(jnp.int32, (L, L), 0)
    col = lax.broadcasted_iota(jnp.int32, (L, L), 1)
    delta = (row - col) * jnp.where(fwd, 1, -1)
    tri = jnp.where(delta <= 0, 1.0, 0.0).astype(F32)
    b_all = jnp.dot(lf, tri, precision=lax.Precision.HIGHEST, preferred_element_type=F32)
    r_all = ipre - b_all
    b_last_all = jnp.where(fwd, b_all[:, L - 1:L], b_all[:, 0:1])
    mask = delta >= 0
    ones_blk = (lax.broadcasted_iota(jnp.int32, (L, LANES), 1) == 0).astype(BF16)
    dn_t = (((1,), (1,)), ((), ()))

    for h in range(H):
        b_row = b_all[h:h + 1, :]
        r_row = r_all[h:h + 1, :]
        bcol = jnp.transpose(jnp.broadcast_to(b_row, (LANES, L)))
        bcol_l = jnp.concatenate([bcol] * (L // LANES), axis=1)
        dm = jnp.where(mask, bcol_l + r_row, -jnp.inf)
        m_loc = jnp.max(dm, axis=1, keepdims=True)
        m_prev = m_ref[h:h + 1, 0:1]
        g = bcol[:, 0:1] + m_prev
        m_row = jnp.maximum(g, m_loc)
        inter = jnp.exp(g - m_row)
        p = jnp.exp(dm - m_row)
        qh = q[:, h * DK:(h + 1) * DK]
        kh = k[:, h * DK:(h + 1) * DK]
        vext = jnp.concatenate([v[:, h * DV:(h + 1) * DV], ones_blk], axis=1)
        sm = (lax.dot_general(qh, kh, dn_t, preferred_element_type=F32) * p).astype(BF16)
        cst = c_ref[h]
        nd = (inter * jnp.dot(qh, cst.astype(BF16), preferred_element_type=F32)
              + jnp.dot(sm, vext, preferred_element_type=F32))
        den = jnp.maximum(jnp.abs(nd[:, DV:DV + 1]), jnp.exp(-m_row))
        o_ref[0, 0, :, h * DV:(h + 1) * DV] = (nd[:, 0:DV] * (1.0 / den)).astype(o_ref.dtype)

        b_last = b_last_all[h:h + 1, :]
        wlog = b_last + r_row
        m_new = jnp.maximum(b_last + m_prev, jnp.max(wlog, axis=1, keepdims=True))
        decay = jnp.exp(b_last + m_prev - m_new)
        kt = jnp.transpose(kh.astype(F32))
        ktw = (kt * jnp.exp(wlog - m_new)).astype(BF16)
        c_ref[h] = decay * cst + jnp.dot(ktw, vext, preferred_element_type=F32)
        m_ref[h:h + 1, :] = jnp.broadcast_to(m_new, (1, LANES))


def _mlstm_scan(qk_l, z_l, gates_l, qk_c, z_c, gates_c, gate_b):
    L, H = SCAN_CHUNK, ML_HEADS
    bsz, t_len, _ = qk_l.shape
    assert qk_c.shape[1] == L
    n_lat = t_len // L
    hk, hv = H * ML_DK, H * ML_DV

    def lat(d, s):
        return jnp.where(d == 0, jnp.maximum(s - 1, 0), n_lat - jnp.maximum(s, 1))

    in_specs = [
        pl.BlockSpec((1, L, hk), lambda d, b, s: (b, lat(d, s), 0)),
        pl.BlockSpec((1, L, hk), lambda d, b, s: (b, lat(d, s), 1)),
        pl.BlockSpec((1, L, hv), lambda d, b, s: (b, lat(d, s), 1)),
        pl.BlockSpec((1, 1, H, L), lambda d, b, s: (b, 2 * d, 0, lat(d, s))),
        pl.BlockSpec((1, 1, H, L), lambda d, b, s: (b, 2 * d + 1, 0, lat(d, s))),
        pl.BlockSpec((1, L, hk), lambda d, b, s: (b, 0, 0)),
        pl.BlockSpec((1, L, hk), lambda d, b, s: (b, 0, 1)),
        pl.BlockSpec((1, L, hv), lambda d, b, s: (b, 0, 1)),
        pl.BlockSpec((1, 1, H, L), lambda d, b, s: (b, 2 * d, 0, 0)),
        pl.BlockSpec((1, 1, H, L), lambda d, b, s: (b, 2 * d + 1, 0, 0)),
        pl.BlockSpec((1, H, 1), lambda d, b, s: (2 * d, 0, 0)),
        pl.BlockSpec((1, H, 1), lambda d, b, s: (2 * d + 1, 0, 0)),
    ]
    return pl.pallas_call(
        _scan_kernel,
        grid=(2, bsz, n_lat + 1),
        in_specs=in_specs,
        out_specs=pl.BlockSpec((1, 1, L, hv), lambda d, b, s: (d, b, lat(d, s), 0)),
        out_shape=jax.ShapeDtypeStruct((2, bsz, t_len, hv), BF16),
        scratch_shapes=[pltpu.VMEM((H, ML_DK, ML_DV + LANES), F32), pltpu.VMEM((H, LANES), F32)],
        compiler_params=_params(("parallel", "parallel", "arbitrary")),
        name="mlstm_scan",
    )(qk_l, qk_l, z_l, gates_l, gates_l, qk_c, qk_c, z_c, gates_c, gates_c,
      gate_b.reshape(4, H, 1), gate_b.reshape(4, H, 1))


def kernel(x, c, ctx, c_ctx, ada_w, ada_b, g_pre_mix, g_post_mix, g_pre_ffn, g_post_ffn, ffn_w1, ffn_w2, hy_w_in, hy_w_out, diff_lq1, diff_lk1, diff_lq2, diff_lk2, diff_subln_g, conv_w, conv_b, conv_ln_g, conv_ln_b, ml_w_in, ml_conv_w, ml_conv_b, ml_gate_b, ml_norm_g, ml_w_out):
    bsz, t_len, d = x.shape
    c_len = ctx.shape[1]
    depth = ada_w.shape[0]
    assert depth == 2 and bsz <= 7 and c_len == SCAN_CHUNK
    ml = t_len * bsz
    mc = c_len * bsz

    cvec = jnp.concatenate([c, c_ctx[None, :], jnp.zeros((8 - bsz - 1, d), F32)], axis=0)
    mods = _ada(cvec, ada_w, ada_b)

    def mod_lat(l, k):
        return mods[l, :bsz, k * d:(k + 1) * d].reshape(bsz, 1, d)

    def mod_ctx(l, k):
        return mods[l, bsz:bsz + 1, k * d:(k + 1) * d].reshape(1, 1, d)

    xl = x.reshape(ml, d)
    xc = ctx.reshape(mc, d)
    w1 = ffn_w1.astype(BF16)
    w2 = ffn_w2.astype(BF16)

    l = 0
    lam_init = 0.8 - 0.6 * math.exp(-0.3 * l)
    w_in = hy_w_in[0].astype(BF16)
    n_in = w_in.shape[1]
    qk_w = 2 * DIFF_HEADS * 2 * DIFF_DH
    rope = _rope_tables(t_len)
    zl = _prenorm_mm(xl, g_pre_mix[l], mod_lat(l, 0), mod_lat(l, 1), w_in, n_in, tm=1024, tn=1024,
                     rows_per_mod=t_len, out_dtype=BF16, rope=rope, rope_cols=qk_w, qscale_cols=qk_w // 2,
                     qscale=ATTN_QSCALE, name="hy_in_proj_lat")
    zc = _prenorm_mm(xc, g_pre_mix[l], mod_ctx(l, 0), mod_ctx(l, 1), w_in, n_in, tm=mc, tn=1024,
                     rows_per_mod=mc, out_dtype=BF16, qscale_cols=qk_w // 2, qscale=ATTN_QSCALE,
                     name="hy_in_proj_ctx")
    zl3 = zl.reshape(bsz, t_len, n_in)
    zc3 = zc.reshape(bsz, c_len, n_in)
    lams = (diff_lq1[0], diff_lk1[0], diff_lq2[0], diff_lk2[0])
    attn_l = _diff_attn(zl3, zc3, lams, diff_subln_g[0], lam_init, tq=1024, sub=256, kchunk=512)
    attn_c = _diff_attn(zc3, None, lams, diff_subln_g[0], lam_init, tq=c_len, sub=c_len, kchunk=c_len)
    conv_l = _conv_module(zl3, conv_w[0], conv_b[0], conv_ln_g[0], conv_ln_b[0], tt=256)
    conv_c = _conv_module(zc3, conv_w[0], conv_b[0], conv_ln_g[0], conv_ln_b[0], tt=c_len)
    w_out = hy_w_out[0].astype(BF16)
    xl = _hy_out(attn_l.reshape(ml, -1), conv_l.reshape(ml, -1), xl, mod_lat(l, 2), g_post_mix[l], w_out,
                 tm=512, rows_per_mod=t_len)
    xc = _hy_out(attn_c.reshape(mc, -1), conv_c.reshape(mc, -1), xc, mod_ctx(l, 2), g_post_mix[l], w_out,
                 tm=512, rows_per_mod=mc)
    xl = _ffn(xl, g_pre_ffn[l], mod_lat(l, 3), mod_lat(l, 4), mod_lat(l, 5), g_post_ffn[l], w1, w2, l,
              tm=512, tf=1024, rows_per_mod=t_len)
    xc = _ffn(xc, g_pre_ffn[l], mod_ctx(l, 3), mod_ctx(l, 4), mod_ctx(l, 5), g_post_ffn[l], w1, w2, l,
              tm=512, tf=1024, rows_per_mod=mc)

    l = 1
    qkv_w = 2 * ML_HEADS * ML_DK + ML_HEADS * ML_DV
    n_gates = 4 * ML_HEADS
    wm = ml_w_in[0].astype(BF16)
    w_main = jnp.concatenate([wm[:, :qkv_w], wm[:, qkv_w + n_gates:]], axis=1)
    w_gate = jnp.pad(wm[:, qkv_w:qkv_w + n_gates], ((0, 0), (0, LANES - n_gates)))
    n_main = w_main.shape[1]
    zl, gl = _prenorm_mm(xl, g_pre_mix[l], mod_lat(l, 0), mod_lat(l, 1), w_main, n_main, tm=1024, tn=2048,
                         rows_per_mod=t_len, out_dtype=BF16, w_side=w_gate, name="ml_in_proj_lat")
    zc, gc = _prenorm_mm(xc, g_pre_mix[l], mod_ctx(l, 0), mod_ctx(l, 1), w_main, qkv_w, tm=mc, tn=2048,
                         rows_per_mod=mc, out_dtype=BF16, w_side=w_gate, name="ml_in_proj_ctx")
    zl3 = zl.reshape(bsz, t_len, n_main)
    zc3 = zc.reshape(bsz, c_len, qkv_w)
    qk_l = _conv3_silu(zl3, ml_conv_w[0], ml_conv_b[0], tt=256)
    qk_c = _conv3_silu(zc3, ml_conv_w[0], ml_conv_b[0], tt=c_len)

    def gates_t(gm, n):
        return gm[:, :n_gates].reshape(bsz, n, 4, ML_HEADS).transpose(0, 2, 3, 1)

    h2 = _mlstm_scan(qk_l, zl3, gates_t(gl, t_len), qk_c, zc3, gates_t(gc, c_len), ml_gate_b[0])
    xl = _ml_out(h2.reshape(2, ml, -1), zl, qkv_w // (ML_HEADS * ML_DV), xl, ml_norm_g[0], mod_lat(l, 2),
                 g_post_mix[l], ml_w_out[0].astype(BF16), tm=512, rows_per_mod=t_len)
    xl = _ffn(xl, g_pre_ffn[l], mod_lat(l, 3), mod_lat(l, 4), mod_lat(l, 5), g_post_ffn[l], w1, w2, l,
              tm=512, tf=1024, rows_per_mod=t_len)
    return xl.reshape(bsz, t_len, d)
```

```python
import functools
import math

import jax
import jax.numpy as jnp
from jax import lax
from jax.experimental import pallas as pl
from jax.experimental.pallas import tpu as pltpu

F32 = jnp.float32
BF16 = jnp.bfloat16

EPS = 1e-6
ROPE_BASE = 10000.0
GRID_W = 64
LANES = 128
SUBLANES = 8
HALO = 16
DIFF_DH = 64
DIFF_HEADS = 8
DIFF_DV = 128
CONV_W = 31
ML_HEADS = 8
ML_DK = 128
ML_DV = 256
ML_CONV_W = 3
SCAN_CHUNK = 256
ATTN_QSCALE = DIFF_DH ** -0.5 * math.log2(math.e)
VMEM_LIMIT = 56 * 1024 * 1024


def _params(sem, flags=None):
    return pltpu.CompilerParams(dimension_semantics=sem, vmem_limit_bytes=VMEM_LIMIT, flags=flags)


def _sigmoid(x):
    return 1.0 / (1.0 + jnp.exp(-x))


def _runtime_zero():
    return jnp.minimum(pl.program_id(0), 0)


def _ada_kernel(c_ref, w_ref, b_ref, o_ref):
    c = c_ref[...]
    s = (c * _sigmoid(c)).astype(BF16)
    o_ref[0] = jnp.dot(s, w_ref[0].astype(BF16), preferred_element_type=F32) + b_ref[0]


def _ada(cvec, ada_w, ada_b):
    depth, d, n = ada_w.shape
    tn = 1024
    return pl.pallas_call(
        _ada_kernel,
        grid=(depth, n // tn),
        in_specs=[pl.BlockSpec((8, d), lambda l, j: (0, 0)),
                  pl.BlockSpec((1, d, tn), lambda l, j: (l, 0, j)),
                  pl.BlockSpec((1, 1, tn), lambda l, j: (l, 0, j))],
        out_specs=pl.BlockSpec((1, 8, tn), lambda l, j: (l, 0, j)),
        out_shape=jax.ShapeDtypeStruct((depth, 8, n), F32),
        compiler_params=_params(("parallel", "parallel")),
        name="ada_ln",
    )(cvec, ada_w, ada_b.reshape(depth, 1, n))


ROW_CHUNK = 16


def _prenorm_into(h_ref, x_ref, g, shift, scale):
    gs = g * (1.0 + scale)
    for r0 in range(0, x_ref.shape[0], ROW_CHUNK):
        x = x_ref[r0:r0 + ROW_CHUNK, :]
        r = lax.rsqrt(jnp.mean(x * x, axis=-1, keepdims=True) + EPS)
        h_ref[r0:r0 + ROW_CHUNK, :] = (x * r * gs + shift).astype(h_ref.dtype)


def _prenorm_mm_kernel(*refs, rope_cols, tn, qscale_cols, qscale, has_side):
    x_ref, g_ref, sh_ref, sc_ref, w_ref = refs[:5]
    rest = list(refs[5:])
    if rope_cols:
        cos_ref, sina_ref, sinb_ref = rest[:3]
        rest = rest[3:]
    if has_side:
        ws_ref, o_ref, os_ref, h_ref = rest
    else:
        o_ref, h_ref = rest
    j = pl.program_id(1)

    @pl.when(j == 0)
    def _():
        _prenorm_into(h_ref, x_ref, g_ref[...], sh_ref[0], sc_ref[0])
        if has_side:
            os_ref[...] = jnp.dot(h_ref[...], ws_ref[...], preferred_element_type=F32)

    acc = jnp.dot(h_ref[...], w_ref[...], preferred_element_type=F32)
    if not (rope_cols or qscale_cols):
        o_ref[...] = acc.astype(o_ref.dtype)
        return
    for c in range(tn // LANES):
        col0 = j * tn + c * LANES
        xs = acc[:, c * LANES:(c + 1) * LANES]
        if qscale_cols:
            xs = xs * jnp.where(col0 < qscale_cols, qscale, 1.0)
        if rope_cols:
            sel = jnp.where(col0 < rope_cols, 0, 1)
            rot = pltpu.roll(xs, LANES - 16, 1) * sina_ref[sel] + pltpu.roll(xs, 16, 1) * sinb_ref[sel]
            xs = xs * cos_ref[sel] + rot
        o_ref[:, c * LANES:(c + 1) * LANES] = xs.astype(o_ref.dtype)


def _prenorm_mm(x, g, shift, scale, w, n_out, *, tm, tn, rows_per_mod, out_dtype, rope=None, rope_cols=0, qscale_cols=0,
                qscale=1.0, w_side=None, name):
    m, d = x.shape
    mod_map = lambda i, j: ((i * tm) // rows_per_mod, 0, 0)
    in_specs = [pl.BlockSpec((tm, d), lambda i, j: (i, 0)),
                pl.BlockSpec((1, d), lambda i, j: (0, 0)),
                pl.BlockSpec((1, 1, d), mod_map),
                pl.BlockSpec((1, 1, d), mod_map),
                pl.BlockSpec((d, tn), lambda i, j: (0, j))]
    args = [x, g.reshape(1, d), shift, scale, w]
    if rope is not None:
        assert rope_cols % LANES == 0 and qscale_cols % LANES == 0
        t_len = rope[0].shape[0]
        assert t_len % tm == 0
        nblk = t_len // tm
        for tab, ident in zip(rope, (1.0, 0.0, 0.0)):
            in_specs.append(pl.BlockSpec((2, tm, LANES), lambda i, j: (0, i % nblk, 0)))
            args.append(jnp.stack([tab, jnp.full_like(tab, ident)]))
    else:
        rope_cols = 0
    out_specs = pl.BlockSpec((tm, tn), lambda i, j: (i, j))
    out_shape = jax.ShapeDtypeStruct((m, n_out), out_dtype)
    if w_side is not None:
        ns = w_side.shape[1]
        in_specs.append(pl.BlockSpec((d, ns), lambda i, j: (0, 0)))
        args.append(w_side)
        out_specs = (out_specs, pl.BlockSpec((tm, ns), lambda i, j: (i, 0)))
        out_shape = (out_shape, jax.ShapeDtypeStruct((m, ns), F32))
    return pl.pallas_call(
        functools.partial(_prenorm_mm_kernel, rope_cols=rope_cols, tn=tn, qscale_cols=qscale_cols, qscale=qscale,
                          has_side=w_side is not None),
        grid=(m // tm, n_out // tn),
        in_specs=in_specs,
        out_specs=out_specs,
        out_shape=out_shape,
        scratch_shapes=[pltpu.VMEM((tm, d), BF16)],
        compiler_params=_params(("parallel", "arbitrary")),
        name=name,
    )(*args)


def _rope_tables(t_len):
    rows = t_len // GRID_W
    r = jnp.repeat(jnp.arange(rows, dtype=F32), GRID_W)
    col = jnp.tile(jnp.arange(GRID_W, dtype=F32), rows)
    n_freq = DIFF_DH // 4
    inv = ROPE_BASE ** (-jnp.arange(n_freq, dtype=F32) / n_freq)
    ar = r[:, None] * inv
    ac = col[:, None] * inv
    ang = jnp.concatenate([ar, ar, ac, ac], axis=-1)
    cos = jnp.tile(jnp.cos(ang), (1, LANES // DIFF_DH))
    sin = jnp.tile(jnp.sin(ang), (1, LANES // DIFF_DH))
    first = (jnp.arange(LANES) % (2 * n_freq)) < n_freq
    return cos, jnp.where(first, -sin, 0.0), jnp.where(first, 0.0, sin)


def _attn_kernel(*refs, lam_init, two_sources, sub, kchunk):
    if two_sources:
        q_ref, k_ref, v_ref, kc_ref, vc_ref, lq1_ref, lk1_ref, lq2_ref, lk2_ref, g_ref, o_ref, vx_ref, *ss_refs = refs
    else:
        q_ref, k_ref, v_ref, lq1_ref, lk1_ref, lq2_ref, lk2_ref, g_ref, o_ref, vx_ref, *ss_refs = refs
    z = _runtime_zero()
    lam = (jnp.exp(jnp.sum(lq1_ref[...] * lk1_ref[...], axis=-1, keepdims=True))
           - jnp.exp(jnp.sum(lq2_ref[...] * lk2_ref[...], axis=-1, keepdims=True)) + lam_init)
    dn = (((1,), (1,)), ((), ()))
    tq = q_ref.shape[1]
    n_keys = k_ref.shape[1]
    chunks = [(k_ref, c0, min(kchunk, n_keys - c0), c0) for c0 in range(0, n_keys, kchunk)]
    if two_sources:
        chunks.append((kc_ref, 0, kc_ref.shape[1], n_keys))

    @pl.when(pl.program_id(2) == 0)
    def _():
        srcs = [(v_ref, 0)] + ([(vc_ref, n_keys)] if two_sources else [])
        for vr, off in srcs:
            n = vr.shape[1]
            vx_ref[off:off + n, 0:DIFF_DV] = vr[0]
            vx_ref[off:off + n, DIFF_DV:2 * DIFF_DV] = (
                lax.broadcasted_iota(jnp.int32, (n, DIFF_DV), 1) == 0).astype(BF16)

    nblk = tq // sub
    st = [dict() for _ in range(2 * nblk)]

    def qk(u, j):
        x, c = divmod(u, 2)
        d = st[u]
        if j == 0:
            q = q_ref[0, x * sub:(x + 1) * sub, :]
            lane = lax.broadcasted_iota(jnp.int32, q.shape, 1)
            keep = (lane < DIFF_DH) if c == 0 else (lane >= DIFF_DH)
            d["q"] = jnp.where(keep, q, jnp.zeros_like(q))
        kr, c0, n, off = chunks[j]
        s = lax.dot_general(d["q"], kr[0, c0:c0 + n, :], dn, preferred_element_type=F32)
        ss_refs[u % 2][z, :, off:off + n] = s
        mj = jnp.max(s, axis=-1, keepdims=True)
        d["m"] = mj if j == 0 else jnp.maximum(d["m"], mj)

    def ev(u, j):
        x, c = divmod(u, 2)
        d = st[u]
        _, _, n, voff = chunks[j]
        p = jnp.exp2(ss_refs[u % 2][z, :, voff:voff + n] - d["m"]).astype(BF16)
        part = jnp.dot(p, vx_ref[voff:voff + n, :], preferred_element_type=F32)
        d["acc"] = part if j == 0 else d["acc"] + part
        if j == len(chunks) - 1:
            acc = d["acc"]
            on = acc[:, 0:DIFF_DV] * (1.0 / acc[:, DIFF_DV:DIFF_DV + 1])
            o1 = d.get("o1")
            d.clear()
            if c == 0:
                st[u + 1]["o1"] = on
            else:
                o = o1 - lam * on
                r = lax.rsqrt(jnp.mean(o * o, axis=-1, keepdims=True) + EPS)
                o_ref[0, x * sub:(x + 1) * sub, :] = (o * r * g_ref[...] * (1.0 - lam_init)).astype(o_ref.dtype)

    nch = len(chunks)
    for u in range(2 * nblk + 1):
        for j in range(nch):
            if u < 2 * nblk:
                qk(u, j)
            if u >= 1:
                ev(u - 1, j)


def _diff_attn(zq, zkv_extra, lams, subln_g, lam_init, *, tq, sub, kchunk):
    bsz, t_len, _ = zq.shape
    nh = DIFF_HEADS
    two = zkv_extra is not None
    n_all = t_len + (zkv_extra.shape[1] if two else 0)
    in_specs = [pl.BlockSpec((1, tq, LANES), lambda b, h, i: (b, i, h)),
                pl.BlockSpec((1, t_len, LANES), lambda b, h, i: (b, 0, nh + h)),
                pl.BlockSpec((1, t_len, LANES), lambda b, h, i: (b, 0, 2 * nh + h))]
    args = [zq, zq, zq]
    if two:
        c_len = zkv_extra.shape[1]
        in_specs += [pl.BlockSpec((1, c_len, LANES), lambda b, h, i: (b, 0, nh + h)),
                     pl.BlockSpec((1, c_len, LANES), lambda b, h, i: (b, 0, 2 * nh + h))]
        args += [zkv_extra, zkv_extra]
    for v in lams:
        in_specs.append(pl.BlockSpec((1, DIFF_DH), lambda b, h, i: (0, 0)))
        args.append(v.reshape(1, DIFF_DH))
    in_specs.append(pl.BlockSpec((1, DIFF_DV), lambda b, h, i: (0, 0)))
    args.append(subln_g.reshape(1, DIFF_DV))
    return pl.pallas_call(
        functools.partial(_attn_kernel, lam_init=lam_init, two_sources=two, sub=sub, kchunk=kchunk),
        grid=(bsz, nh, t_len // tq),
        in_specs=in_specs,
        out_specs=pl.BlockSpec((1, tq, LANES), lambda b, h, i: (b, i, h)),
        out_shape=jax.ShapeDtypeStruct((bsz, t_len, nh * DIFF_DV), BF16),
        scratch_shapes=[pltpu.VMEM((n_all, 2 * DIFF_DV), BF16), pltpu.VMEM((1, sub, n_all), F32),
                        pltpu.VMEM((1, sub, n_all), F32)],
        compiler_params=_params(("parallel", "parallel", "arbitrary")),
        name="diff_attn" + ("_lat" if two else "_ctx"),
    )(*args)


def _dwconv_phases(width):
    offs = [HALO - width // 2 + w for w in range(width)]
    return sorted({o % SUBLANES for o in offs} - {0}), max(offs) // SUBLANES * SUBLANES


def _dwconv_scratch(tt, width, nch):
    phases, reach = _dwconv_phases(width)
    return pltpu.VMEM((max(len(phases), 1), tt + reach, nch), F32)


def _dwconv_into(y_ref, ysh_ref, w_ref, b_ref, out_ref, *, tt, width, nch):
    phases, reach = _dwconv_phases(width)
    for idx, p in enumerate(phases):
        ysh_ref[idx] = y_ref[p:p + tt + reach, :]
    rc = 64
    for c0 in range(0, nch, LANES):
        for r0 in range(0, tt, rc):
            acc = jnp.broadcast_to(b_ref[:, c0:c0 + LANES], (rc, LANES))
            for w in range(width):
                off = HALO - width // 2 + w
                p, start = off % SUBLANES, r0 + off // SUBLANES * SUBLANES
                if p == 0:
                    tap = y_ref[start:start + rc, c0:c0 + LANES]
                else:
                    tap = ysh_ref[phases.index(p), start:start + rc, c0:c0 + LANES]
                acc = acc + tap * w_ref[w:w + 1, c0:c0 + LANES]
            out_ref[r0:r0 + rc, c0:c0 + LANES] = acc


def _convmod_kernel(ap_ref, a_ref, an_ref, gp_ref, g_ref, gn_ref, cw_ref, cb_ref, lng_ref, lnb_ref, o_ref,
                    y_ref, c_ref, ysh_ref, *, tt, nch):
    i = pl.program_id(1)
    n = pl.num_programs(1)

    def glu(a, g):
        return a.astype(F32) * _sigmoid(g.astype(F32))

    y_ref[0:HALO] = jnp.where(i > 0, glu(ap_ref[0], gp_ref[0]), 0.0)
    y_ref[HALO:HALO + tt] = glu(a_ref[0], g_ref[0])
    y_ref[HALO + tt:2 * HALO + tt] = jnp.where(i < n - 1, glu(an_ref[0], gn_ref[0]), 0.0)
    _dwconv_into(y_ref, ysh_ref, cw_ref, cb_ref, c_ref, tt=tt, width=CONV_W, nch=nch)
    c = c_ref[...]
    mu = jnp.mean(c, axis=-1, keepdims=True)
    xc = c - mu
    var = jnp.mean(xc * xc, axis=-1, keepdims=True)
    y = xc * lax.rsqrt(var + EPS) * lng_ref[...] + lnb_ref[...]
    o_ref[0] = (y * _sigmoid(y)).astype(o_ref.dtype)


def _halo_specs(tt, t_len, width, colblk):
    per = tt // HALO
    last = t_len // HALO - 1
    return (pl.BlockSpec((1, HALO, width), lambda b, i: (b, jnp.maximum(i * per - 1, 0), colblk)),
            pl.BlockSpec((1, tt, width), lambda b, i: (b, i, colblk)),
            pl.BlockSpec((1, HALO, width), lambda b, i: (b, jnp.minimum((i + 1) * per, last), colblk)))


def _conv_module(z, conv_w, conv_b, ln_g, ln_b, *, tt):
    bsz, t_len, n = z.shape
    nch = conv_w.shape[1]
    a_blk = (n - 2 * nch) // nch
    vec = lambda: pl.BlockSpec((1, nch), lambda b, i: (0, 0))
    return pl.pallas_call(
        functools.partial(_convmod_kernel, tt=tt, nch=nch),
        grid=(bsz, t_len // tt),
        in_specs=[*_halo_specs(tt, t_len, nch, a_blk), *_halo_specs(tt, t_len, nch, a_blk + 1),
                  pl.BlockSpec((CONV_W, nch), lambda b, i: (0, 0)), vec(), vec(), vec()],
        out_specs=pl.BlockSpec((1, tt, nch), lambda b, i: (b, i, 0)),
        out_shape=jax.ShapeDtypeStruct((bsz, t_len, nch), BF16),
        scratch_shapes=[pltpu.VMEM((tt + 2 * HALO, nch), F32), pltpu.VMEM((tt, nch), F32),
                        _dwconv_scratch(tt, CONV_W, nch)],
        compiler_params=_params(("parallel", "arbitrary")),
        name="conformer_conv",
    )(z, z, z, z, z, z, conv_w, conv_b.reshape(1, nch), ln_g.reshape(1, nch), ln_b.reshape(1, nch))


def _conv3_kernel(zp_ref, z_ref, zn_ref, cw_ref, cb_ref, o_ref, y_ref, c_ref, ysh_ref, *, tt, nch, kscale):
    i = pl.program_id(1)
    n = pl.num_programs(1)
    y_ref[0:HALO] = jnp.where(i > 0, zp_ref[0].astype(F32), 0.0)
    y_ref[HALO:HALO + tt] = z_ref[0].astype(F32)
    y_ref[HALO + tt:2 * HALO + tt] = jnp.where(i < n - 1, zn_ref[0].astype(F32), 0.0)
    _dwconv_into(y_ref, ysh_ref, cw_ref, cb_ref, c_ref, tt=tt, width=ML_CONV_W, nch=nch)
    c = c_ref[...]
    s = c * _sigmoid(c)
    half = nch // 2
    o_ref[0, :, 0:half] = s[:, 0:half].astype(o_ref.dtype)
    o_ref[0, :, half:nch] = (s[:, half:nch] * kscale).astype(o_ref.dtype)


def _conv3_silu(z, conv_w, conv_b, *, tt):
    bsz, t_len, _ = z.shape
    nch = conv_w.shape[1]
    return pl.pallas_call(
        functools.partial(_conv3_kernel, tt=tt, nch=nch, kscale=ML_DK ** -0.5),
        grid=(bsz, t_len // tt),
        in_specs=[*_halo_specs(tt, t_len, nch, 0),
                  pl.BlockSpec((ML_CONV_W, nch), lambda b, i: (0, 0)),
                  pl.BlockSpec((1, nch), lambda b, i: (0, 0))],
        out_specs=pl.BlockSpec((1, tt, nch), lambda b, i: (b, i, 0)),
        out_shape=jax.ShapeDtypeStruct((bsz, t_len, nch), BF16),
        scratch_shapes=[pltpu.VMEM((tt + 2 * HALO, nch), F32), pltpu.VMEM((tt, nch), F32),
                        _dwconv_scratch(tt, ML_CONV_W, nch)],
        compiler_params=_params(("parallel", "arbitrary")),
        name="ml_conv3_silu",
    )(z, z, z, conv_w, conv_b.reshape(1, nch))


def _post_residual(y, x, gate, gpost):
    r = lax.rsqrt(jnp.mean(y * y, axis=-1, keepdims=True) + EPS)
    return x + gate * (y * r * gpost)


OUT_SUB = 256


def _hy_out_kernel(a_ref, c_ref, x_ref, gate_ref, gpost_ref, w_ref, o_ref):
    ka = a_ref.shape[1]
    ys = []
    for r0 in range(0, a_ref.shape[0], OUT_SUB):
        rows = slice(r0, r0 + OUT_SUB)
        ys.append(jnp.dot(a_ref[rows, :], w_ref[0:ka, :], preferred_element_type=F32)
                  + jnp.dot(c_ref[rows, :], w_ref[ka:, :], preferred_element_type=F32))
        if len(ys) > 1:
            prev = slice(r0 - OUT_SUB, r0)
            o_ref[prev, :] = _post_residual(ys[-2], x_ref[prev, :], gate_ref[0], gpost_ref[...])
    last = slice(a_ref.shape[0] - OUT_SUB, a_ref.shape[0])
    o_ref[last, :] = _post_residual(ys[-1], x_ref[last, :], gate_ref[0], gpost_ref[...])


def _hy_out(attn, conv, x, gate, gpost, w, *, tm, rows_per_mod):
    m, d = x.shape
    ka, kc = attn.shape[1], conv.shape[1]
    return pl.pallas_call(
        _hy_out_kernel,
        grid=(m // tm,),
        in_specs=[pl.BlockSpec((tm, ka), lambda i: (i, 0)),
                  pl.BlockSpec((tm, kc), lambda i: (i, 0)),
                  pl.BlockSpec((tm, d), lambda i: (i, 0)),
                  pl.BlockSpec((1, 1, d), lambda i: ((i * tm) // rows_per_mod, 0, 0)),
                  pl.BlockSpec((1, d), lambda i: (0, 0)),
                  pl.BlockSpec((ka + kc, d), lambda i: (0, 0), pipeline_mode=pl.Buffered(1))],
        out_specs=pl.BlockSpec((tm, d), lambda i: (i, 0)),
        out_shape=jax.ShapeDtypeStruct((m, d), F32),
        compiler_params=_params(("parallel",)),
        name="hybrid_out_proj",
    )(attn, conv, x, gate, gpost.reshape(1, d), w)


def _ml_out_kernel(hf_ref, hb_ref, og_ref, x_ref, ng_ref, gate_ref, gpost_ref, w_ref, o_ref):
    def gated(rows):
        hsum = hf_ref[0, rows, :].astype(F32) + hb_ref[0, rows, :].astype(F32)
        parts = []
        for h in range(ML_HEADS):
            hh = hsum[:, h * ML_DV:(h + 1) * ML_DV]
            r = lax.rsqrt(jnp.mean(hh * hh, axis=-1, keepdims=True) + EPS)
            parts.append(hh * r * ng_ref[:, h * ML_DV:(h + 1) * ML_DV])
        hn = jnp.concatenate(parts, axis=1)
        return (hn * _sigmoid(og_ref[rows, :].astype(F32))).astype(BF16)

    subs = [slice(r0, r0 + OUT_SUB) for r0 in range(0, x_ref.shape[0], OUT_SUB)]
    a = gated(subs[0])
    y_prev = None
    for idx, rows in enumerate(subs):
        y = jnp.dot(a, w_ref[...], preferred_element_type=F32)
        if idx + 1 < len(subs):
            a = gated(subs[idx + 1])
        if y_prev is not None:
            prev = subs[idx - 1]
            o_ref[prev, :] = _post_residual(y_prev, x_ref[prev, :], gate_ref[0], gpost_ref[...])
        y_prev = y
    o_ref[subs[-1], :] = _post_residual(y_prev, x_ref[subs[-1], :], gate_ref[0], gpost_ref[...])


def _ml_out(h2, z, o_blk, x, norm_g, gate, gpost, w, *, tm, rows_per_mod):
    m, d = x.shape
    kv = h2.shape[2]
    return pl.pallas_call(
        _ml_out_kernel,
        grid=(m // tm,),
        in_specs=[pl.BlockSpec((1, tm, kv), lambda i: (0, i, 0)),
                  pl.BlockSpec((1, tm, kv), lambda i: (1, i, 0)),
                  pl.BlockSpec((tm, kv), lambda i: (i, o_blk)),
                  pl.BlockSpec((tm, d), lambda i: (i, 0)),
                  pl.BlockSpec((1, kv), lambda i: (0, 0)),
                  pl.BlockSpec((1, 1, d), lambda i: ((i * tm) // rows_per_mod, 0, 0)),
                  pl.BlockSpec((1, d), lambda i: (0, 0)),
                  pl.BlockSpec((kv, d), lambda i: (0, 0), pipeline_mode=pl.Buffered(1))],
        out_specs=pl.BlockSpec((tm, d), lambda i: (i, 0)),
        out_shape=jax.ShapeDtypeStruct((m, d), F32),
        compiler_params=_params(("parallel",)),
        name="mlstm_out_proj",
    )(h2, h2, z, x, norm_g.reshape(1, kv), gate, gpost.reshape(1, d), w)


def _ffn_kernel(x_ref, g_ref, sh_ref, sc_ref, gate_ref, gpost_ref, w1_ref, w2_ref, o_ref, h_ref, acc_ref):
    f = pl.program_id(1)
    nf = pl.num_programs(1)

    @pl.when(f == 0)
    def _():
        _prenorm_into(h_ref, x_ref, g_ref[...], sh_ref[0], sc_ref[0])
        acc_ref[...] = jnp.zeros_like(acc_ref)

    t = jnp.maximum(jnp.dot(h_ref[...], w1_ref[0], preferred_element_type=F32), 0.0)
    acc_ref[...] += jnp.dot((t * t).astype(BF16), w2_ref[0], preferred_element_type=F32)

    @pl.when(f == nf - 1)
    def _():
        for r0 in range(0, x_ref.shape[0], ROW_CHUNK):
            rows = slice(r0, r0 + ROW_CHUNK)
            o_ref[rows, :] = _post_residual(acc_ref[rows, :], x_ref[rows, :], gate_ref[0], gpost_ref[...])


def _ffn(x, g, shift, scale, gate, gpost, w1, w2, layer, *, tm, tf, rows_per_mod):
    m, d = x.shape
    dff = w1.shape[2]
    mod_map = lambda i, f: ((i * tm) // rows_per_mod, 0, 0)
    vec = lambda: pl.BlockSpec((1, d), lambda i, f: (0, 0))
    return pl.pallas_call(
        _ffn_kernel,
        grid=(m // tm, dff // tf),
        in_specs=[pl.BlockSpec((tm, d), lambda i, f: (i, 0)), vec(),
                  pl.BlockSpec((1, 1, d), mod_map), pl.BlockSpec((1, 1, d), mod_map),
                  pl.BlockSpec((1, 1, d), mod_map), vec(),
                  pl.BlockSpec((1, d, tf), lambda i, f: (layer, 0, f)),
                  pl.BlockSpec((1, tf, d), lambda i, f: (layer, f, 0))],
        out_specs=pl.BlockSpec((tm, d), lambda i, f: (i, 0)),
        out_shape=jax.ShapeDtypeStruct((m, d), F32),
        scratch_shapes=[pltpu.VMEM((tm, d), BF16), pltpu.VMEM((tm, d), F32)],
        compiler_params=_params(("parallel", "arbitrary")),
        name="ffn",
    )(x, g.reshape(1, d), shift, scale, gate, gpost.reshape(1, d), w1, w2)


def _scan_kernel(ql_ref, kl_ref, vl_ref, gil_ref, gfl_ref, qc_ref, kc_ref, vc_ref, gic_ref, gfc_ref,
                 bi_ref, bf_ref, o_ref, c_ref, m_ref):
    L, H, DK, DV = SCAN_CHUNK, ML_HEADS, ML_DK, ML_DV
    fwd = pl.program_id(0) == 0
    s = pl.program_id(2)

    @pl.when(s == 0)
    def _():
        c_ref[...] = jnp.zeros_like(c_ref)
        m_ref[...] = jnp.zeros_like(m_ref)

    is_ctx = s == 0
    q = jnp.where(is_ctx, qc_ref[0], ql_ref[0])
    k = jnp.where(is_ctx, kc_ref[0], kl_ref[0])
    v = jnp.where(is_ctx, vc_ref[0], vl_ref[0])
    ipre = jnp.where(is_ctx, gic_ref[0, 0], gil_ref[0, 0]) + bi_ref[0]
    fpre = jnp.where(is_ctx, gfc_ref[0, 0], gfl_ref[0, 0]) + bf_ref[0]
    lf = jnp.minimum(fpre, 0.0) - jnp.log1p(jnp.exp(-jnp.abs(fpre)))

    row = lax.broadcasted_iota(jnp.int32, (L, L), 0)
    col = lax.broadcasted_iota(jnp.int32, (L, L), 1)
    delta = (row - col) * jnp.where(fwd, 1, -1)
    tri = jnp.where(delta <= 0, 1.0, 0.0).astype(F32)
    b_all = jnp.dot(lf, tri, precision=lax.Precision.HIGHEST, preferred_element_type=F32)
    r_all = ipre - b_all
    b_last_all = jnp.where(fwd, b_all[:, L - 1:L], b_all[:, 0:1])
    mask = delta >= 0
    ones_blk = (lax.broadcasted_iota(jnp.int32, (L, LANES), 1) == 0).astype(BF16)
    dn_t = (((1,), (1,)), ((), ()))

    for h in range(H):
        b_row = b_all[h:h + 1, :]
        r_row = r_all[h:h + 1, :]
        bcol = jnp.transpose(jnp.broadcast_to(b_row, (LANES, L)))
        bcol_l = jnp.concatenate([bcol] * (L // LANES), axis=1)
        dm = jnp.where(mask, bcol_l + r_row, -jnp.inf)
        m_loc = jnp.max(dm, axis=1, keepdims=True)
        m_prev = m_ref[h:h + 1, 0:1]
        g = bcol[:, 0:1] + m_prev
        m_row = jnp.maximum(g, m_loc)
        inter = jnp.exp(g - m_row)
        p = jnp.exp(dm - m_row)
        qh = q[:, h * DK:(h + 1) * DK]
        kh = k[:, h * DK:(h + 1) * DK]
        vext = jnp.concatenate([v[:, h * DV:(h + 1) * DV], ones_blk], axis=1)
        sm = (lax.dot_general(qh, kh, dn_t, preferred_element_type=F32) * p).astype(BF16)
        cst = c_ref[h]
        nd = (inter * jnp.dot(qh, cst.astype(BF16), preferred_element_type=F32)
              + jnp.dot(sm, vext, preferred_element_type=F32))
        den = jnp.maximum(jnp.abs(nd[:, DV:DV + 1]), jnp.exp(-m_row))
        o_ref[0, 0, :, h * DV:(h + 1) * DV] = (nd[:, 0:DV] * (1.0 / den)).astype(o_ref.dtype)

        b_last = b_last_all[h:h + 1, :]
        wlog = b_last + r_row
        m_new = jnp.maximum(b_last + m_prev, jnp.max(wlog, axis=1, keepdims=True))
        decay = jnp.exp(b_last + m_prev - m_new)
        kt = jnp.transpose(kh.astype(F32))
        ktw = (kt * jnp.exp(wlog - m_new)).astype(BF16)
        c_ref[h] = decay * cst + jnp.dot(ktw, vext, preferred_element_type=F32)
        m_ref[h:h + 1, :] = jnp.broadcast_to(m_new, (1, LANES))


def _mlstm_scan(qk_l, z_l, gates_l, qk_c, z_c, gates_c, gate_b):
    L, H = SCAN_CHUNK, ML_HEADS
    bsz, t_len, _ = qk_l.shape
    assert qk_c.shape[1] == L
    n_lat = t_len // L
    hk, hv = H * ML_DK, H * ML_DV

    def lat(d, s):
        return jnp.where(d == 0, jnp.maximum(s - 1, 0), n_lat - jnp.maximum(s, 1))

    in_specs = [
        pl.BlockSpec((1, L, hk), lambda d, b, s: (b, lat(d, s), 0)),
        pl.BlockSpec((1, L, hk), lambda d, b, s: (b, lat(d, s), 1)),
        pl.BlockSpec((1, L, hv), lambda d, b, s: (b, lat(d, s), 1)),
        pl.BlockSpec((1, 1, H, L), lambda d, b, s: (b, 2 * d, 0, lat(d, s))),
        pl.BlockSpec((1, 1, H, L), lambda d, b, s: (b, 2 * d + 1, 0, lat(d, s))),
        pl.BlockSpec((1, L, hk), lambda d, b, s: (b, 0, 0)),
        pl.BlockSpec((1, L, hk), lambda d, b, s: (b, 0, 1)),
        pl.BlockSpec((1, L, hv), lambda d, b, s: (b, 0, 1)),
        pl.BlockSpec((1, 1, H, L), lambda d, b, s: (b, 2 * d, 0, 0)),
        pl.BlockSpec((1, 1, H, L), lambda d, b, s: (b, 2 * d + 1, 0, 0)),
        pl.BlockSpec((1, H, 1), lambda d, b, s: (2 * d, 0, 0)),
        pl.BlockSpec((1, H, 1), lambda d, b, s: (2 * d + 1, 0, 0)),
    ]
    return pl.pallas_call(
        _scan_kernel,
        grid=(2, bsz, n_lat + 1),
        in_specs=in_specs,
        out_specs=pl.BlockSpec((1, 1, L, hv), lambda d, b, s: (d, b, lat(d, s), 0)),
        out_shape=jax.ShapeDtypeStruct((2, bsz, t_len, hv), BF16),
        scratch_shapes=[pltpu.VMEM((H, ML_DK, ML_DV + LANES), F32), pltpu.VMEM((H, LANES), F32)],
        compiler_params=_params(("parallel", "parallel", "arbitrary")),
        name="mlstm_scan",
    )(qk_l, qk_l, z_l, gates_l, gates_l, qk_c, qk_c, z_c, gates_c, gates_c,
      gate_b.reshape(4, H, 1), gate_b.reshape(4, H, 1))


def kernel(x, c, ctx, c_ctx, ada_w, ada_b, g_pre_mix, g_post_mix, g_pre_ffn, g_post_ffn, ffn_w1, ffn_w2, hy_w_in, hy_w_out, diff_lq1, diff_lk1, diff_lq2, diff_lk2, diff_subln_g, conv_w, conv_b, conv_ln_g, conv_ln_b, ml_w_in, ml_conv_w, ml_conv_b, ml_gate_b, ml_norm_g, ml_w_out):
    bsz, t_len, d = x.shape
    c_len = ctx.shape[1]
    depth = ada_w.shape[0]
    assert depth == 2 and bsz <= 7 and c_len == SCAN_CHUNK
    ml = t_len * bsz
    mc = c_len * bsz

    cvec = jnp.concatenate([c, c_ctx[None, :], jnp.zeros((8 - bsz - 1, d), F32)], axis=0)
    mods = _ada(cvec, ada_w, ada_b)

    def mod_lat(l, k):
        return mods[l, :bsz, k * d:(k + 1) * d].reshape(bsz, 1, d)

    def mod_ctx(l, k):
        return mods[l, bsz:bsz + 1, k * d:(k + 1) * d].reshape(1, 1, d)

    xl = x.reshape(ml, d)
    xc = ctx.reshape(mc, d)
    w1 = ffn_w1.astype(BF16)
    w2 = ffn_w2.astype(BF16)

    l = 0
    lam_init = 0.8 - 0.6 * math.exp(-0.3 * l)
    w_in = hy_w_in[0].astype(BF16)
    n_in = w_in.shape[1]
    qk_w = 2 * DIFF_HEADS * 2 * DIFF_DH
    rope = _rope_tables(t_len)
    zl = _prenorm_mm(xl, g_pre_mix[l], mod_lat(l, 0), mod_lat(l, 1), w_in, n_in, tm=512, tn=2560,
                     rows_per_mod=t_len, out_dtype=BF16, rope=rope, rope_cols=qk_w, qscale_cols=qk_w // 2,
                     qscale=ATTN_QSCALE, name="hy_in_proj_lat")
    zc = _prenorm_mm(xc, g_pre_mix[l], mod_ctx(l, 0), mod_ctx(l, 1), w_in, n_in, tm=mc, tn=1024,
                     rows_per_mod=mc, out_dtype=BF16, qscale_cols=qk_w // 2, qscale=ATTN_QSCALE,
                     name="hy_in_proj_ctx")
    zl3 = zl.reshape(bsz, t_len, n_in)
    zc3 = zc.reshape(bsz, c_len, n_in)
    lams = (diff_lq1[0], diff_lk1[0], diff_lq2[0], diff_lk2[0])
    attn_l = _diff_attn(zl3, zc3, lams, diff_subln_g[0], lam_init, tq=2048, sub=256, kchunk=512)
    attn_c = _diff_attn(zc3, None, lams, diff_subln_g[0], lam_init, tq=c_len, sub=c_len, kchunk=c_len)
    conv_l = _conv_module(zl3, conv_w[0], conv_b[0], conv_ln_g[0], conv_ln_b[0], tt=256)
    conv_c = _conv_module(zc3, conv_w[0], conv_b[0], conv_ln_g[0], conv_ln_b[0], tt=c_len)
    w_out = hy_w_out[0].astype(BF16)
    xl = _hy_out(attn_l.reshape(ml, -1), conv_l.reshape(ml, -1), xl, mod_lat(l, 2), g_post_mix[l], w_out,
                 tm=512, rows_per_mod=t_len)
    xc = _hy_out(attn_c.reshape(mc, -1), conv_c.reshape(mc, -1), xc, mod_ctx(l, 2), g_post_mix[l], w_out,
                 tm=512, rows_per_mod=mc)
    xl = _ffn(xl, g_pre_ffn[l], mod_lat(l, 3), mod_lat(l, 4), mod_lat(l, 5), g_post_ffn[l], w1, w2, l,
              tm=512, tf=1024, rows_per_mod=t_len)
    xc = _ffn(xc, g_pre_ffn[l], mod_ctx(l, 3), mod_ctx(l, 4), mod_ctx(l, 5), g_post_ffn[l], w1, w2, l,
              tm=512, tf=1024, rows_per_mod=mc)

    l = 1
    qkv_w = 2 * ML_HEADS * ML_DK + ML_HEADS * ML_DV
    n_gates = 4 * ML_HEADS
    wm = ml_w_in[0].astype(BF16)
    w_main = jnp.concatenate([wm[:, :qkv_w], wm[:, qkv_w + n_gates:]], axis=1)
    w_gate = jnp.pad(wm[:, qkv_w:qkv_w + n_gates], ((0, 0), (0, LANES - n_gates)))
    n_main = w_main.shape[1]
    zl, gl = _prenorm_mm(xl, g_pre_mix[l], mod_lat(l, 0), mod_lat(l, 1), w_main, n_main, tm=1024, tn=2048,
                         rows_per_mod=t_len, out_dtype=BF16, w_side=w_gate, name="ml_in_proj_lat")
    zc, gc = _prenorm_mm(xc, g_pre_mix[l], mod_ctx(l, 0), mod_ctx(l, 1), w_main, qkv_w, tm=mc, tn=2048,
                         rows_per_mod=mc, out_dtype=BF16, w_side=w_gate, name="ml_in_proj_ctx")
    zl3 = zl.reshape(bsz, t_len, n_main)
    zc3 = zc.reshape(bsz, c_len, qkv_w)
    qk_l = _conv3_silu(zl3, ml_conv_w[0], ml_conv_b[0], tt=256)
    qk_c = _conv3_silu(zc3, ml_conv_w[0], ml_conv_b[0], tt=c_len)

    def gates_t(gm, n):
        return gm[:, :n_gates].reshape(bsz, n, 4, ML_HEADS).transpose(0, 2, 3, 1)

    h2 = _mlstm_scan(qk_l, zl3, gates_t(gl, t_len), qk_c, zc3, gates_t(gc, c_len), ml_gate_b[0])
    xl = _ml_out(h2.reshape(2, ml, -1), zl, qkv_w // (ML_HEADS * ML_DV), xl, ml_norm_g[0], mod_lat(l, 2),
                 g_post_mix[l], ml_w_out[0].astype(BF16), tm=512, rows_per_mod=t_len)
    xl = _ffn(xl, g_pre_ffn[l], mod_lat(l, 3), mod_lat(l, 4), mod_lat(l, 5), g_post_ffn[l], w1, w2, l,
              tm=512, tf=1024, rows_per_mod=t_len)
    return xl.reshape(bsz, t_len, d)
```

```python
import functools
import math

import jax
import jax.numpy as jnp
from jax import lax
from jax.experimental import pallas as pl
from jax.experimental.pallas import tpu as pltpu

F32 = jnp.float32
BF16 = jnp.bfloat16

EPS = 1e-6
ROPE_BASE = 10000.0
GRID_W = 64
LANES = 128
SUBLANES = 8
HALO = 16
DIFF_DH = 64
DIFF_HEADS = 8
DIFF_DV = 128
CONV_W = 31
ML_HEADS = 8
ML_DK = 128
ML_DV = 256
ML_CONV_W = 3
SCAN_CHUNK = 256
ATTN_QSCALE = DIFF_DH ** -0.5 * math.log2(math.e)
VMEM_LIMIT = 56 * 1024 * 1024


def _params(sem, flags=None):
    return pltpu.CompilerParams(dimension_semantics=sem, vmem_limit_bytes=VMEM_LIMIT, flags=flags)


def _sigmoid(x):
    return 1.0 / (1.0 + jnp.exp(-x))


def _runtime_zero():
    return jnp.minimum(pl.program_id(0), 0)


def _ada_kernel(c_ref, w_ref, b_ref, o_ref):
    c = c_ref[...]
    s = (c * _sigmoid(c)).astype(BF16)
    o_ref[0] = jnp.dot(s, w_ref[0].astype(BF16), preferred_element_type=F32) + b_ref[0]


def _ada(cvec, ada_w, ada_b):
    depth, d, n = ada_w.shape
    tn = 1024
    return pl.pallas_call(
        _ada_kernel,
        grid=(depth, n // tn),
        in_specs=[pl.BlockSpec((8, d), lambda l, j: (0, 0)),
                  pl.BlockSpec((1, d, tn), lambda l, j: (l, 0, j)),
                  pl.BlockSpec((1, 1, tn), lambda l, j: (l, 0, j))],
        out_specs=pl.BlockSpec((1, 8, tn), lambda l, j: (l, 0, j)),
        out_shape=jax.ShapeDtypeStruct((depth, 8, n), F32),
        compiler_params=_params(("parallel", "parallel")),
        name="ada_ln",
    )(cvec, ada_w, ada_b.reshape(depth, 1, n))


ROW_CHUNK = 16


def _prenorm_into(h_ref, x_ref, g, shift, scale):
    gs = g * (1.0 + scale)
    for r0 in range(0, x_ref.shape[0], ROW_CHUNK):
        x = x_ref[r0:r0 + ROW_CHUNK, :]
        r = lax.rsqrt(jnp.mean(x * x, axis=-1, keepdims=True) + EPS)
        h_ref[r0:r0 + ROW_CHUNK, :] = (x * r * gs + shift).astype(h_ref.dtype)


def _prenorm_mm_kernel(*refs, rope_cols, tn, qscale_cols, qscale, has_side, lookahead, nj):
    x_ref, g_ref, sh_ref, sc_ref, w_ref = refs[:5]
    rest = list(refs[5:])
    if lookahead:
        xn_ref, shn_ref, scn_ref = rest[:3]
        rest = rest[3:]
    if rope_cols:
        cos_ref, sina_ref, sinb_ref = rest[:3]
        rest = rest[3:]
    if has_side:
        ws_ref, o_ref, os_ref, *h_refs = rest
    else:
        o_ref, *h_refs = rest
    i = pl.program_id(0)
    j = pl.program_id(1)
    tm = x_ref.shape[0]

    def matmul(h_ref):
        acc = jnp.dot(h_ref[...], w_ref[...], preferred_element_type=F32)
        if not (rope_cols or qscale_cols):
            o_ref[...] = acc.astype(o_ref.dtype)
            return
        for c in range(tn // LANES):
            col0 = j * tn + c * LANES
            xs = acc[:, c * LANES:(c + 1) * LANES]
            if qscale_cols:
                xs = xs * jnp.where(col0 < qscale_cols, qscale, 1.0)
            if rope_cols:
                sel = jnp.where(col0 < rope_cols, 0, 1)
                rot = pltpu.roll(xs, LANES - 16, 1) * sina_ref[sel] + pltpu.roll(xs, 16, 1) * sinb_ref[sel]
                xs = xs * cos_ref[sel] + rot
            o_ref[:, c * LANES:(c + 1) * LANES] = xs.astype(o_ref.dtype)

    if not lookahead:
        h_ref, = h_refs

        @pl.when(j == 0)
        def _():
            _prenorm_into(h_ref, x_ref, g_ref[...], sh_ref[0], sc_ref[0])
            if has_side:
                os_ref[...] = jnp.dot(h_ref[...], ws_ref[...], preferred_element_type=F32)

        matmul(h_ref)
        return

    assert not has_side
    h0_ref, h1_ref = h_refs

    @pl.when((i == 0) & (j == 0))
    def _():
        _prenorm_rows(h0_ref, x_ref, 0, tm, g_ref[...], sh_ref[0], sc_ref[0])

    def step(h_cur, h_next):
        matmul(h_cur)
        _prenorm_rows(h_next, xn_ref, j * (tm // nj), tm // nj, g_ref[...], shn_ref[0], scn_ref[0])

    @pl.when(i % 2 == 0)
    def _():
        step(h0_ref, h1_ref)

    @pl.when(i % 2 == 1)
    def _():
        step(h1_ref, h0_ref)


def _prenorm_mm(x, g, shift, scale, w, n_out, *, tm, tn, rows_per_mod, out_dtype, rope=None, rope_cols=0, qscale_cols=0,
                qscale=1.0, w_side=None, lookahead=False, name):
    m, d = x.shape
    n_i, nj = m // tm, n_out // tn
    mod_map = lambda i, j: ((i * tm) // rows_per_mod, 0, 0)
    in_specs = [pl.BlockSpec((tm, d), lambda i, j: (i, 0)),
                pl.BlockSpec((1, d), lambda i, j: (0, 0)),
                pl.BlockSpec((1, 1, d), mod_map),
                pl.BlockSpec((1, 1, d), mod_map),
                pl.BlockSpec((d, tn), lambda i, j: (0, j))]
    args = [x, g.reshape(1, d), shift, scale, w]
    if lookahead:
        assert tm % (nj * ROW_CHUNK) == 0
        nxt = lambda i: jnp.minimum(i + 1, n_i - 1)
        mod_next = lambda i, j: ((nxt(i) * tm) // rows_per_mod, 0, 0)
        in_specs += [pl.BlockSpec((tm, d), lambda i, j: (nxt(i), 0)),
                     pl.BlockSpec((1, 1, d), mod_next), pl.BlockSpec((1, 1, d), mod_next)]
        args += [x, shift, scale]
    if rope is not None:
        assert rope_cols % LANES == 0 and qscale_cols % LANES == 0
        t_len = rope[0].shape[0]
        assert t_len % tm == 0
        nblk = t_len // tm
        for tab, ident in zip(rope, (1.0, 0.0, 0.0)):
            in_specs.append(pl.BlockSpec((2, tm, LANES), lambda i, j: (0, i % nblk, 0)))
            args.append(jnp.stack([tab, jnp.full_like(tab, ident)]))
    else:
        rope_cols = 0
    out_specs = pl.BlockSpec((tm, tn), lambda i, j: (i, j))
    out_shape = jax.ShapeDtypeStruct((m, n_out), out_dtype)
    if w_side is not None:
        ns = w_side.shape[1]
        in_specs.append(pl.BlockSpec((d, ns), lambda i, j: (0, 0)))
        args.append(w_side)
        out_specs = (out_specs, pl.BlockSpec((tm, ns), lambda i, j: (i, 0)))
        out_shape = (out_shape, jax.ShapeDtypeStruct((m, ns), F32))
    return pl.pallas_call(
        functools.partial(_prenorm_mm_kernel, rope_cols=rope_cols, tn=tn, qscale_cols=qscale_cols, qscale=qscale,
                          has_side=w_side is not None, lookahead=lookahead, nj=nj),
        grid=(n_i, nj),
        in_specs=in_specs,
        out_specs=out_specs,
        out_shape=out_shape,
        scratch_shapes=[pltpu.VMEM((tm, d), BF16)] * (2 if lookahead else 1),
        compiler_params=_params(("arbitrary", "arbitrary")),
        name=name,
    )(*args)


def _rope_tables(t_len):
    rows = t_len // GRID_W
    r = jnp.repeat(jnp.arange(rows, dtype=F32), GRID_W)
    col = jnp.tile(jnp.arange(GRID_W, dtype=F32), rows)
    n_freq = DIFF_DH // 4
    inv = ROPE_BASE ** (-jnp.arange(n_freq, dtype=F32) / n_freq)
    ar = r[:, None] * inv
    ac = col[:, None] * inv
    ang = jnp.concatenate([ar, ar, ac, ac], axis=-1)
    cos = jnp.tile(jnp.cos(ang), (1, LANES // DIFF_DH))
    sin = jnp.tile(jnp.sin(ang), (1, LANES // DIFF_DH))
    first = (jnp.arange(LANES) % (2 * n_freq)) < n_freq
    return cos, jnp.where(first, -sin, 0.0), jnp.where(first, 0.0, sin)


def _attn_kernel(*refs, lam_init, two_sources, sub, kchunk):
    if two_sources:
        q_ref, k_ref, v_ref, kc_ref, vc_ref, lq1_ref, lk1_ref, lq2_ref, lk2_ref, g_ref, o_ref, vx_ref, *ss_refs = refs
    else:
        q_ref, k_ref, v_ref, lq1_ref, lk1_ref, lq2_ref, lk2_ref, g_ref, o_ref, vx_ref, *ss_refs = refs
    z = _runtime_zero()
    lam = (jnp.exp(jnp.sum(lq1_ref[...] * lk1_ref[...], axis=-1, keepdims=True))
           - jnp.exp(jnp.sum(lq2_ref[...] * lk2_ref[...], axis=-1, keepdims=True)) + lam_init)
    dn = (((1,), (1,)), ((), ()))
    tq = q_ref.shape[1]
    n_keys = k_ref.shape[1]
    chunks = [(k_ref, c0, min(kchunk, n_keys - c0), c0) for c0 in range(0, n_keys, kchunk)]
    if two_sources:
        chunks.append((kc_ref, 0, kc_ref.shape[1], n_keys))

    @pl.when(pl.program_id(2) == 0)
    def _():
        srcs = [(v_ref, 0)] + ([(vc_ref, n_keys)] if two_sources else [])
        for vr, off in srcs:
            n = vr.shape[1]
            vx_ref[off:off + n, 0:DIFF_DV] = vr[0]
            vx_ref[off:off + n, DIFF_DV:2 * DIFF_DV] = (
                lax.broadcasted_iota(jnp.int32, (n, DIFF_DV), 1) == 0).astype(BF16)

    nblk = tq // sub
    st = [dict() for _ in range(2 * nblk)]

    def qk(u, j):
        x, c = divmod(u, 2)
        d = st[u]
        if j == 0:
            q = q_ref[0, x * sub:(x + 1) * sub, :]
            lane = lax.broadcasted_iota(jnp.int32, q.shape, 1)
            keep = (lane < DIFF_DH) if c == 0 else (lane >= DIFF_DH)
            d["q"] = jnp.where(keep, q, jnp.zeros_like(q))
        kr, c0, n, off = chunks[j]
        s = lax.dot_general(d["q"], kr[0, c0:c0 + n, :], dn, preferred_element_type=F32)
        ss_refs[u % 2][z, :, off:off + n] = s
        mj = jnp.max(s, axis=-1, keepdims=True)
        d["m"] = mj if j == 0 else jnp.maximum(d["m"], mj)

    def ev(u, j):
        x, c = divmod(u, 2)
        d = st[u]
        _, _, n, voff = chunks[j]
        p = jnp.exp2(ss_refs[u % 2][z, :, voff:voff + n] - d["m"]).astype(BF16)
        part = jnp.dot(p, vx_ref[voff:voff + n, :], preferred_element_type=F32)
        d["acc"] = part if j == 0 else d["acc"] + part
        if j == len(chunks) - 1:
            acc = d["acc"]
            on = acc[:, 0:DIFF_DV] * (1.0 / acc[:, DIFF_DV:DIFF_DV + 1])
            o1 = d.get("o1")
            d.clear()
            if c == 0:
                st[u + 1]["o1"] = on
            else:
                o = o1 - lam * on
                r = lax.rsqrt(jnp.mean(o * o, axis=-1, keepdims=True) + EPS)
                o_ref[0, x * sub:(x + 1) * sub, :] = (o * r * g_ref[...] * (1.0 - lam_init)).astype(o_ref.dtype)

    nch = len(chunks)
    for u in range(2 * nblk + 1):
        for j in range(nch):
            if u < 2 * nblk:
                qk(u, j)
            if u >= 1:
                ev(u - 1, j)


def _diff_attn(zq, zkv_extra, lams, subln_g, lam_init, *, tq, sub, kchunk):
    bsz, t_len, _ = zq.shape
    nh = DIFF_HEADS
    two = zkv_extra is not None
    n_all = t_len + (zkv_extra.shape[1] if two else 0)
    in_specs = [pl.BlockSpec((1, tq, LANES), lambda b, h, i: (b, i, h)),
                pl.BlockSpec((1, t_len, LANES), lambda b, h, i: (b, 0, nh + h)),
                pl.BlockSpec((1, t_len, LANES), lambda b, h, i: (b, 0, 2 * nh + h))]
    args = [zq, zq, zq]
    if two:
        c_len = zkv_extra.shape[1]
        in_specs += [pl.BlockSpec((1, c_len, LANES), lambda b, h, i: (b, 0, nh + h)),
                     pl.BlockSpec((1, c_len, LANES), lambda b, h, i: (b, 0, 2 * nh + h))]
        args += [zkv_extra, zkv_extra]
    for v in lams:
        in_specs.append(pl.BlockSpec((1, DIFF_DH), lambda b, h, i: (0, 0)))
        args.append(v.reshape(1, DIFF_DH))
    in_specs.append(pl.BlockSpec((1, DIFF_DV), lambda b, h, i: (0, 0)))
    args.append(subln_g.reshape(1, DIFF_DV))
    return pl.pallas_call(
        functools.partial(_attn_kernel, lam_init=lam_init, two_sources=two, sub=sub, kchunk=kchunk),
        grid=(bsz, nh, t_len // tq),
        in_specs=in_specs,
        out_specs=pl.BlockSpec((1, tq, LANES), lambda b, h, i: (b, i, h)),
        out_shape=jax.ShapeDtypeStruct((bsz, t_len, nh * DIFF_DV), BF16),
        scratch_shapes=[pltpu.VMEM((n_all, 2 * DIFF_DV), BF16), pltpu.VMEM((1, sub, n_all), F32),
                        pltpu.VMEM((1, sub, n_all), F32)],
        compiler_params=_params(("parallel", "parallel", "arbitrary")),
        name="diff_attn" + ("_lat" if two else "_ctx"),
    )(*args)


def _dwconv_phases(width):
    offs = [HALO - width // 2 + w for w in range(width)]
    return sorted({o % SUBLANES for o in offs} - {0}), max(offs) // SUBLANES * SUBLANES


def _dwconv_scratch(tt, width, nch):
    phases, reach = _dwconv_phases(width)
    return pltpu.VMEM((max(len(phases), 1), tt + reach, nch), F32)


def _dwconv_into(y_ref, ysh_ref, w_ref, b_ref, out_ref, *, tt, width, nch):
    phases, reach = _dwconv_phases(width)
    for idx, p in enumerate(phases):
        ysh_ref[idx] = y_ref[p:p + tt + reach, :]
    rc = 64
    for c0 in range(0, nch, LANES):
        for r0 in range(0, tt, rc):
            acc = jnp.broadcast_to(b_ref[:, c0:c0 + LANES], (rc, LANES))
            for w in range(width):
                off = HALO - width // 2 + w
                p, start = off % SUBLANES, r0 + off // SUBLANES * SUBLANES
                if p == 0:
                    tap = y_ref[start:start + rc, c0:c0 + LANES]
                else:
                    tap = ysh_ref[phases.index(p), start:start + rc, c0:c0 + LANES]
                acc = acc + tap * w_ref[w:w + 1, c0:c0 + LANES]
            out_ref[r0:r0 + rc, c0:c0 + LANES] = acc


def _convmod_kernel(ap_ref, a_ref, an_ref, gp_ref, g_ref, gn_ref, cw_ref, cb_ref, lng_ref, lnb_ref, o_ref,
                    y_ref, c_ref, ysh_ref, *, tt, nch):
    i = pl.program_id(1)
    n = pl.num_programs(1)

    def glu(a, g):
        return a.astype(F32) * _sigmoid(g.astype(F32))

    y_ref[0:HALO] = jnp.where(i > 0, glu(ap_ref[0], gp_ref[0]), 0.0)
    y_ref[HALO:HALO + tt] = glu(a_ref[0], g_ref[0])
    y_ref[HALO + tt:2 * HALO + tt] = jnp.where(i < n - 1, glu(an_ref[0], gn_ref[0]), 0.0)
    _dwconv_into(y_ref, ysh_ref, cw_ref, cb_ref, c_ref, tt=tt, width=CONV_W, nch=nch)
    c = c_ref[...]
    mu = jnp.mean(c, axis=-1, keepdims=True)
    xc = c - mu
    var = jnp.mean(xc * xc, axis=-1, keepdims=True)
    y = xc * lax.rsqrt(var + EPS) * lng_ref[...] + lnb_ref[...]
    o_ref[0] = (y * _sigmoid(y)).astype(o_ref.dtype)


def _halo_specs(tt, t_len, width, colblk):
    per = tt // HALO
    last = t_len // HALO - 1
    return (pl.BlockSpec((1, HALO, width), lambda b, i: (b, jnp.maximum(i * per - 1, 0), colblk)),
            pl.BlockSpec((1, tt, width), lambda b, i: (b, i, colblk)),
            pl.BlockSpec((1, HALO, width), lambda b, i: (b, jnp.minimum((i + 1) * per, last), colblk)))


def _conv_module(z, conv_w, conv_b, ln_g, ln_b, *, tt):
    bsz, t_len, n = z.shape
    nch = conv_w.shape[1]
    a_blk = (n - 2 * nch) // nch
    vec = lambda: pl.BlockSpec((1, nch), lambda b, i: (0, 0))
    return pl.pallas_call(
        functools.partial(_convmod_kernel, tt=tt, nch=nch),
        grid=(bsz, t_len // tt),
        in_specs=[*_halo_specs(tt, t_len, nch, a_blk), *_halo_specs(tt, t_len, nch, a_blk + 1),
                  pl.BlockSpec((CONV_W, nch), lambda b, i: (0, 0)), vec(), vec(), vec()],
        out_specs=pl.BlockSpec((1, tt, nch), lambda b, i: (b, i, 0)),
        out_shape=jax.ShapeDtypeStruct((bsz, t_len, nch), BF16),
        scratch_shapes=[pltpu.VMEM((tt + 2 * HALO, nch), F32), pltpu.VMEM((tt, nch), F32),
                        _dwconv_scratch(tt, CONV_W, nch)],
        compiler_params=_params(("parallel", "arbitrary")),
        name="conformer_conv",
    )(z, z, z, z, z, z, conv_w, conv_b.reshape(1, nch), ln_g.reshape(1, nch), ln_b.reshape(1, nch))


def _conv3_kernel(zp_ref, z_ref, zn_ref, cw_ref, cb_ref, o_ref, y_ref, c_ref, ysh_ref, *, tt, nch, kscale):
    i = pl.program_id(1)
    n = pl.num_programs(1)
    y_ref[0:HALO] = jnp.where(i > 0, zp_ref[0].astype(F32), 0.0)
    y_ref[HALO:HALO + tt] = z_ref[0].astype(F32)
    y_ref[HALO + tt:2 * HALO + tt] = jnp.where(i < n - 1, zn_ref[0].astype(F32), 0.0)
    _dwconv_into(y_ref, ysh_ref, cw_ref, cb_ref, c_ref, tt=tt, width=ML_CONV_W, nch=nch)
    c = c_ref[...]
    s = c * _sigmoid(c)
    half = nch // 2
    o_ref[0, :, 0:half] = s[:, 0:half].astype(o_ref.dtype)
    o_ref[0, :, half:nch] = (s[:, half:nch] * kscale).astype(o_ref.dtype)


def _conv3_silu(z, conv_w, conv_b, *, tt):
    bsz, t_len, _ = z.shape
    nch = conv_w.shape[1]
    return pl.pallas_call(
        functools.partial(_conv3_kernel, tt=tt, nch=nch, kscale=ML_DK ** -0.5),
        grid=(bsz, t_len // tt),
        in_specs=[*_halo_specs(tt, t_len, nch, 0),
                  pl.BlockSpec((ML_CONV_W, nch), lambda b, i: (0, 0)),
                  pl.BlockSpec((1, nch), lambda b, i: (0, 0))],
        out_specs=pl.BlockSpec((1, tt, nch), lambda b, i: (b, i, 0)),
        out_shape=jax.ShapeDtypeStruct((bsz, t_len, nch), BF16),
        scratch_shapes=[pltpu.VMEM((tt + 2 * HALO, nch), F32), pltpu.VMEM((tt, nch), F32),
                        _dwconv_scratch(tt, ML_CONV_W, nch)],
        compiler_params=_params(("parallel", "arbitrary")),
        name="ml_conv3_silu",
    )(z, z, z, conv_w, conv_b.reshape(1, nch))


def _post_residual(y, x, gate, gpost):
    r = lax.rsqrt(jnp.mean(y * y, axis=-1, keepdims=True) + EPS)
    return x + gate * (y * r * gpost)


OUT_SUB = 256


def _hy_out_kernel(a_ref, c_ref, x_ref, gate_ref, gpost_ref, w_ref, o_ref):
    ka = a_ref.shape[1]
    ys = []
    for r0 in range(0, a_ref.shape[0], OUT_SUB):
        rows = slice(r0, r0 + OUT_SUB)
        ys.append(jnp.dot(a_ref[rows, :], w_ref[0:ka, :], preferred_element_type=F32)
                  + jnp.dot(c_ref[rows, :], w_ref[ka:, :], preferred_element_type=F32))
        if len(ys) > 1:
            prev = slice(r0 - OUT_SUB, r0)
            o_ref[prev, :] = _post_residual(ys[-2], x_ref[prev, :], gate_ref[0], gpost_ref[...])
    last = slice(a_ref.shape[0] - OUT_SUB, a_ref.shape[0])
    o_ref[last, :] = _post_residual(ys[-1], x_ref[last, :], gate_ref[0], gpost_ref[...])


def _hy_out(attn, conv, x, gate, gpost, w, *, tm, rows_per_mod):
    m, d = x.shape
    ka, kc = attn.shape[1], conv.shape[1]
    return pl.pallas_call(
        _hy_out_kernel,
        grid=(m // tm,),
        in_specs=[pl.BlockSpec((tm, ka), lambda i: (i, 0)),
                  pl.BlockSpec((tm, kc), lambda i: (i, 0)),
                  pl.BlockSpec((tm, d), lambda i: (i, 0)),
                  pl.BlockSpec((1, 1, d), lambda i: ((i * tm) // rows_per_mod, 0, 0)),
                  pl.BlockSpec((1, d), lambda i: (0, 0)),
                  pl.BlockSpec((ka + kc, d), lambda i: (0, 0), pipeline_mode=pl.Buffered(1))],
        out_specs=pl.BlockSpec((tm, d), lambda i: (i, 0)),
        out_shape=jax.ShapeDtypeStruct((m, d), F32),
        compiler_params=_params(("parallel",)),
        name="hybrid_out_proj",
    )(attn, conv, x, gate, gpost.reshape(1, d), w)


def _ml_out_kernel(hf_ref, hb_ref, og_ref, x_ref, ng_ref, gate_ref, gpost_ref, w_ref, o_ref):
    def gated(rows):
        hsum = hf_ref[0, rows, :].astype(F32) + hb_ref[0, rows, :].astype(F32)
        parts = []
        for h in range(ML_HEADS):
            hh = hsum[:, h * ML_DV:(h + 1) * ML_DV]
            r = lax.rsqrt(jnp.mean(hh * hh, axis=-1, keepdims=True) + EPS)
            parts.append(hh * r * ng_ref[:, h * ML_DV:(h + 1) * ML_DV])
        hn = jnp.concatenate(parts, axis=1)
        return (hn * _sigmoid(og_ref[rows, :].astype(F32))).astype(BF16)

    subs = [slice(r0, r0 + OUT_SUB) for r0 in range(0, x_ref.shape[0], OUT_SUB)]
    a = gated(subs[0])
    y_prev = None
    for idx, rows in enumerate(subs):
        y = jnp.dot(a, w_ref[...], preferred_element_type=F32)
        if idx + 1 < len(subs):
            a = gated(subs[idx + 1])
        if y_prev is not None:
            prev = subs[idx - 1]
            o_ref[prev, :] = _post_residual(y_prev, x_ref[prev, :], gate_ref[0], gpost_ref[...])
        y_prev = y
    o_ref[subs[-1], :] = _post_residual(y_prev, x_ref[subs[-1], :], gate_ref[0], gpost_ref[...])


def _ml_out(h2, z, o_blk, x, norm_g, gate, gpost, w, *, tm, rows_per_mod):
    m, d = x.shape
    kv = h2.shape[2]
    return pl.pallas_call(
        _ml_out_kernel,
        grid=(m // tm,),
        in_specs=[pl.BlockSpec((1, tm, kv), lambda i: (0, i, 0)),
                  pl.BlockSpec((1, tm, kv), lambda i: (1, i, 0)),
                  pl.BlockSpec((tm, kv), lambda i: (i, o_blk)),
                  pl.BlockSpec((tm, d), lambda i: (i, 0)),
                  pl.BlockSpec((1, kv), lambda i: (0, 0)),
                  pl.BlockSpec((1, 1, d), lambda i: ((i * tm) // rows_per_mod, 0, 0)),
                  pl.BlockSpec((1, d), lambda i: (0, 0)),
                  pl.BlockSpec((kv, d), lambda i: (0, 0), pipeline_mode=pl.Buffered(1))],
        out_specs=pl.BlockSpec((tm, d), lambda i: (i, 0)),
        out_shape=jax.ShapeDtypeStruct((m, d), F32),
        compiler_params=_params(("parallel",)),
        name="mlstm_out_proj",
    )(h2, h2, z, x, norm_g.reshape(1, kv), gate, gpost.reshape(1, d), w)


def _prenorm_rows(h_ref, x_ref, r0, nrows, g, shift, scale):
    gs = g * (1.0 + scale)
    for c in range(0, nrows, ROW_CHUNK):
        start = r0 + c
        rows = (slice(start, start + ROW_CHUNK) if isinstance(start, int)
                else pl.ds(pl.multiple_of(start, ROW_CHUNK), ROW_CHUNK))
        x = x_ref[rows, :]
        r = lax.rsqrt(jnp.mean(x * x, axis=-1, keepdims=True) + EPS)
        h_ref[rows, :] = (x * r * gs + shift).astype(h_ref.dtype)


def _ffn_kernel(x_ref, xn_ref, g_ref, sh_ref, sc_ref, shn_ref, scn_ref, gate_ref, gpost_ref, w1_ref, w2_ref, o_ref,
                h0_ref, h1_ref, acc_ref, *, nf):
    i = pl.program_id(0)
    f = pl.program_id(1)
    tm = x_ref.shape[0]

    @pl.when((i == 0) & (f == 0))
    def _():
        _prenorm_rows(h0_ref, x_ref, 0, tm, g_ref[...], sh_ref[0], sc_ref[0])
        acc_ref[...] = jnp.zeros_like(acc_ref)

    def step(h_cur, h_next):
        t = jnp.maximum(jnp.dot(h_cur[...], w1_ref[0], preferred_element_type=F32), 0.0)
        p = jnp.dot((t * t).astype(BF16), w2_ref[0], preferred_element_type=F32)
        acc_ref[...] = jnp.where(f == 0, p, acc_ref[...] + p)
        _prenorm_rows(h_next, xn_ref, f * (tm // nf), tm // nf, g_ref[...], shn_ref[0], scn_ref[0])

    @pl.when(i % 2 == 0)
    def _():
        step(h0_ref, h1_ref)

    @pl.when(i % 2 == 1)
    def _():
        step(h1_ref, h0_ref)

    @pl.when(f == nf - 1)
    def _():
        for r0 in range(0, tm, ROW_CHUNK):
            rows = slice(r0, r0 + ROW_CHUNK)
            o_ref[rows, :] = _post_residual(acc_ref[rows, :], x_ref[rows, :], gate_ref[0], gpost_ref[...])


def _ffn(x, g, shift, scale, gate, gpost, w1, w2, layer, *, tm, tf, rows_per_mod):
    m, d = x.shape
    dff = w1.shape[2]
    nblk, nf = m // tm, dff // tf
    assert tm % (nf * ROW_CHUNK) == 0
    nxt = lambda i: jnp.minimum(i + 1, nblk - 1)
    mod_map = lambda i, f: ((i * tm) // rows_per_mod, 0, 0)
    mod_next = lambda i, f: ((nxt(i) * tm) // rows_per_mod, 0, 0)
    vec = lambda: pl.BlockSpec((1, d), lambda i, f: (0, 0))
    return pl.pallas_call(
        functools.partial(_ffn_kernel, nf=nf),
        grid=(nblk, nf),
        in_specs=[pl.BlockSpec((tm, d), lambda i, f: (i, 0)), pl.BlockSpec((tm, d), lambda i, f: (nxt(i), 0)), vec(),
                  pl.BlockSpec((1, 1, d), mod_map), pl.BlockSpec((1, 1, d), mod_map),
                  pl.BlockSpec((1, 1, d), mod_next), pl.BlockSpec((1, 1, d), mod_next),
                  pl.BlockSpec((1, 1, d), mod_map), vec(),
                  pl.BlockSpec((1, d, tf), lambda i, f: (layer, 0, f)),
                  pl.BlockSpec((1, tf, d), lambda i, f: (layer, f, 0))],
        out_specs=pl.BlockSpec((tm, d), lambda i, f: (i, 0)),
        out_shape=jax.ShapeDtypeStruct((m, d), F32),
        scratch_shapes=[pltpu.VMEM((tm, d), BF16), pltpu.VMEM((tm, d), BF16), pltpu.VMEM((tm, d), F32)],
        compiler_params=_params(("arbitrary", "arbitrary")),
        name="ffn",
    )(x, x, g.reshape(1, d), shift, scale, shift, scale, gate, gpost.reshape(1, d), w1, w2)


def _scan_kernel(ql_ref, kl_ref, vl_ref, gil_ref, gfl_ref, qc_ref, kc_ref, vc_ref, gic_ref, gfc_ref,
                 bi_ref, bf_ref, o_ref, c_ref, m_ref):
    L, H, DK, DV = SCAN_CHUNK, ML_HEADS, ML_DK, ML_DV
    fwd = pl.program_id(0) == 0
    s = pl.program_id(2)

    @pl.when(s == 0)
    def _():
        c_ref[...] = jnp.zeros_like(c_ref)
        m_ref[...] = jnp.zeros_like(m_ref)

    is_ctx = s == 0
    q = jnp.where(is_ctx, qc_ref[0], ql_ref[0])
    k = jnp.where(is_ctx, kc_ref[0], kl_ref[0])
    v = jnp.where(is_ctx, vc_ref[0], vl_ref[0])
    ipre = jnp.where(is_ctx, gic_ref[0, 0], gil_ref[0, 0]) + bi_ref[0]
    fpre = jnp.where(is_ctx, gfc_ref[0, 0], gfl_ref[0, 0]) + bf_ref[0]
    lf = jnp.minimum(fpre, 0.0) - jnp.log1p(jnp.exp(-jnp.abs(fpre)))

    row = lax.broadcasted_iota(jnp.int32, (L, L), 0)
    col = lax.broadcasted_iota(jnp.int32, (L, L), 1)
    delta = (row - col) * jnp.where(fwd, 1, -1)
    tri = jnp.where(delta <= 0, 1.0, 0.0).astype(F32)
    b_all = jnp.dot(lf, tri, precision=lax.Precision.HIGHEST, preferred_element_type=F32)
    r_all = ipre - b_all
    b_last_all = jnp.where(fwd, b_all[:, L - 1:L], b_all[:, 0:1])
    mask = delta >= 0
    ones_blk = (lax.broadcasted_iota(jnp.int32, (L, LANES), 1) == 0).astype(BF16)
    dn_t = (((1,), (1,)), ((), ()))

    for h in range(H):
        b_row = b_all[h:h + 1, :]
        r_row = r_all[h:h + 1, :]
        bcol = jnp.transpose(jnp.broadcast_to(b_row, (LANES, L)))
        bcol_l = jnp.concatenate([bcol] * (L // LANES), axis=1)
        dm = jnp.where(mask, bcol_l + r_row, -jnp.inf)
        m_loc = jnp.max(dm, axis=1, keepdims=True)
        m_prev = m_ref[h:h + 1, 0:1]
        g = bcol[:, 0:1] + m_prev
        m_row = jnp.maximum(g, m_loc)
        inter = jnp.exp(g - m_row)
        p = jnp.exp(dm - m_row)
        qh = q[:, h * DK:(h + 1) * DK]
        kh = k[:, h * DK:(h + 1) * DK]
        vext = jnp.concatenate([v[:, h * DV:(h + 1) * DV], ones_blk], axis=1)
        sm = (lax.dot_general(qh, kh, dn_t, preferred_element_type=F32) * p).astype(BF16)
        cst = c_ref[h]
        nd = (inter * jnp.dot(qh, cst.astype(BF16), preferred_element_type=F32)
              + jnp.dot(sm, vext, preferred_element_type=F32))
        den = jnp.maximum(jnp.abs(nd[:, DV:DV + 1]), jnp.exp(-m_row))
        o_ref[0, 0, :, h * DV:(h + 1) * DV] = (nd[:, 0:DV] * (1.0 / den)).astype(o_ref.dtype)

        b_last = b_last_all[h:h + 1, :]
        wlog = b_last + r_row
        m_new = jnp.maximum(b_last + m_prev, jnp.max(wlog, axis=1, keepdims=True))
        decay = jnp.exp(b_last + m_prev - m_new)
        kt = jnp.transpose(kh.astype(F32))
        ktw = (kt * jnp.exp(wlog - m_new)).astype(BF16)
        c_ref[h] = decay * cst + jnp.dot(ktw, vext, preferred_element_type=F32)
        m_ref[h:h + 1, :] = jnp.broadcast_to(m_new, (1, LANES))


def _mlstm_scan(qk_l, z_l, gates_l, qk_c, z_c, gates_c, gate_b):
    L, H = SCAN_CHUNK, ML_HEADS
    bsz, t_len, _ = qk_l.shape
    assert qk_c.shape[1] == L
    n_lat = t_len // L
    hk, hv = H * ML_DK, H * ML_DV

    def lat(d, s):
        return jnp.where(d == 0, jnp.maximum(s - 1, 0), n_lat - jnp.maximum(s, 1))

    in_specs = [
        pl.BlockSpec((1, L, hk), lambda d, b, s: (b, lat(d, s), 0)),
        pl.BlockSpec((1, L, hk), lambda d, b, s: (b, lat(d, s), 1)),
        pl.BlockSpec((1, L, hv), lambda d, b, s: (b, lat(d, s), 1)),
        pl.BlockSpec((1, 1, H, L), lambda d, b, s: (b, 2 * d, 0, lat(d, s))),
        pl.BlockSpec((1, 1, H, L), lambda d, b, s: (b, 2 * d + 1, 0, lat(d, s))),
        pl.BlockSpec((1, L, hk), lambda d, b, s: (b, 0, 0)),
        pl.BlockSpec((1, L, hk), lambda d, b, s: (b, 0, 1)),
        pl.BlockSpec((1, L, hv), lambda d, b, s: (b, 0, 1)),
        pl.BlockSpec((1, 1, H, L), lambda d, b, s: (b, 2 * d, 0, 0)),
        pl.BlockSpec((1, 1, H, L), lambda d, b, s: (b, 2 * d + 1, 0, 0)),
        pl.BlockSpec((1, H, 1), lambda d, b, s: (2 * d, 0, 0)),
        pl.BlockSpec((1, H, 1), lambda d, b, s: (2 * d + 1, 0, 0)),
    ]
    return pl.pallas_call(
        _scan_kernel,
        grid=(2, bsz, n_lat + 1),
        in_specs=in_specs,
        out_specs=pl.BlockSpec((1, 1, L, hv), lambda d, b, s: (d, b, lat(d, s), 0)),
        out_shape=jax.ShapeDtypeStruct((2, bsz, t_len, hv), BF16),
        scratch_shapes=[pltpu.VMEM((H, ML_DK, ML_DV + LANES), F32), pltpu.VMEM((H, LANES), F32)],
        compiler_params=_params(("parallel", "parallel", "arbitrary")),
        name="mlstm_scan",
    )(qk_l, qk_l, z_l, gates_l, gates_l, qk_c, qk_c, z_c, gates_c, gates_c,
      gate_b.reshape(4, H, 1), gate_b.reshape(4, H, 1))


def kernel(x, c, ctx, c_ctx, ada_w, ada_b, g_pre_mix, g_post_mix, g_pre_ffn, g_post_ffn, ffn_w1, ffn_w2, hy_w_in, hy_w_out, diff_lq1, diff_lk1, diff_lq2, diff_lk2, diff_subln_g, conv_w, conv_b, conv_ln_g, conv_ln_b, ml_w_in, ml_conv_w, ml_conv_b, ml_gate_b, ml_norm_g, ml_w_out):
    bsz, t_len, d = x.shape
    c_len = ctx.shape[1]
    depth = ada_w.shape[0]
    assert depth == 2 and bsz <= 7 and c_len == SCAN_CHUNK
    ml = t_len * bsz
    mc = c_len * bsz

    cvec = jnp.concatenate([c, c_ctx[None, :], jnp.zeros((8 - bsz - 1, d), F32)], axis=0)
    mods = _ada(cvec, ada_w, ada_b)

    def mod_lat(l, k):
        return mods[l, :bsz, k * d:(k + 1) * d].reshape(bsz, 1, d)

    def mod_ctx(l, k):
        return mods[l, bsz:bsz + 1, k * d:(k + 1) * d].reshape(1, 1, d)

    xl = x.reshape(ml, d)
    xc = ctx.reshape(mc, d)
    w1 = ffn_w1.astype(BF16)
    w2 = ffn_w2.astype(BF16)

    l = 0
    lam_init = 0.8 - 0.6 * math.exp(-0.3 * l)
    w_in = hy_w_in[0].astype(BF16)
    n_in = w_in.shape[1]
    qk_w = 2 * DIFF_HEADS * 2 * DIFF_DH
    rope = _rope_tables(t_len)
    zl = _prenorm_mm(xl, g_pre_mix[l], mod_lat(l, 0), mod_lat(l, 1), w_in, n_in, tm=512, tn=2560,
                     rows_per_mod=t_len, out_dtype=BF16, rope=rope, rope_cols=qk_w, qscale_cols=qk_w // 2,
                     qscale=ATTN_QSCALE, lookahead=True, name="hy_in_proj_lat")
    zc = _prenorm_mm(xc, g_pre_mix[l], mod_ctx(l, 0), mod_ctx(l, 1), w_in, n_in, tm=mc, tn=1024,
                     rows_per_mod=mc, out_dtype=BF16, qscale_cols=qk_w // 2, qscale=ATTN_QSCALE,
                     name="hy_in_proj_ctx")
    zl3 = zl.reshape(bsz, t_len, n_in)
    zc3 = zc.reshape(bsz, c_len, n_in)
    lams = (diff_lq1[0], diff_lk1[0], diff_lq2[0], diff_lk2[0])
    attn_l = _diff_attn(zl3, zc3, lams, diff_subln_g[0], lam_init, tq=2048, sub=256, kchunk=512)
    attn_c = _diff_attn(zc3, None, lams, diff_subln_g[0], lam_init, tq=c_len, sub=c_len, kchunk=c_len)
    conv_l = _conv_module(zl3, conv_w[0], conv_b[0], conv_ln_g[0], conv_ln_b[0], tt=256)
    conv_c = _conv_module(zc3, conv_w[0], conv_b[0], conv_ln_g[0], conv_ln_b[0], tt=c_len)
    w_out = hy_w_out[0].astype(BF16)
    xl = _hy_out(attn_l.reshape(ml, -1), conv_l.reshape(ml, -1), xl, mod_lat(l, 2), g_post_mix[l], w_out,
                 tm=512, rows_per_mod=t_len)
    xc = _hy_out(attn_c.reshape(mc, -1), conv_c.reshape(mc, -1), xc, mod_ctx(l, 2), g_post_mix[l], w_out,
                 tm=512, rows_per_mod=mc)
    xl = _ffn(xl, g_pre_ffn[l], mod_lat(l, 3), mod_lat(l, 4), mod_lat(l, 5), g_post_ffn[l], w1, w2, l,
              tm=512, tf=1024, rows_per_mod=t_len)
    xc = _ffn(xc, g_pre_ffn[l], mod_ctx(l, 3), mod_ctx(l, 4), mod_ctx(l, 5), g_post_ffn[l], w1, w2, l,
              tm=512, tf=1024, rows_per_mod=mc)

    l = 1
    qkv_w = 2 * ML_HEADS * ML_DK + ML_HEADS * ML_DV
    n_gates = 4 * ML_HEADS
    wm = ml_w_in[0].astype(BF16)
    w_main = jnp.concatenate([wm[:, :qkv_w], wm[:, qkv_w + n_gates:]], axis=1)
    w_gate = jnp.pad(wm[:, qkv_w:qkv_w + n_gates], ((0, 0), (0, LANES - n_gates)))
    n_main = w_main.shape[1]
    zl, gl = _prenorm_mm(xl, g_pre_mix[l], mod_lat(l, 0), mod_lat(l, 1), w_main, n_main, tm=1024, tn=2048,
                         rows_per_mod=t_len, out_dtype=BF16, w_side=w_gate, name="ml_in_proj_lat")
    zc, gc = _prenorm_mm(xc, g_pre_mix[l], mod_ctx(l, 0), mod_ctx(l, 1), w_main, qkv_w, tm=mc, tn=2048,
                         rows_per_mod=mc, out_dtype=BF16, w_side=w_gate, name="ml_in_proj_ctx")
    zl3 = zl.reshape(bsz, t_len, n_main)
    zc3 = zc.reshape(bsz, c_len, qkv_w)
    qk_l = _conv3_silu(zl3, ml_conv_w[0], ml_conv_b[0], tt=256)
    qk_c = _conv3_silu(zc3, ml_conv_w[0], ml_conv_b[0], tt=c_len)

    def gates_t(gm, n):
        return gm[:, :n_gates].reshape(bsz, n, 4, ML_HEADS).transpose(0, 2, 3, 1)

    h2 = _mlstm_scan(qk_l, zl3, gates_t(gl, t_len), qk_c, zc3, gates_t(gc, c_len), ml_gate_b[0])
    xl = _ml_out(h2.reshape(2, ml, -1), zl, qkv_w // (ML_HEADS * ML_DV), xl, ml_norm_g[0], mod_lat(l, 2),
                 g_post_mix[l], ml_w_out[0].astype(BF16), tm=512, rows_per_mod=t_len)
    xl = _ffn(xl, g_pre_ffn[l], mod_lat(l, 3), mod_lat(l, 4), mod_lat(l, 5), g_post_ffn[l], w1, w2, l,
              tm=512, tf=1024, rows_per_mod=t_len)
    return xl.reshape(bsz, t_len, d)
```

```python
import functools
import math

import jax
import jax.numpy as jnp
from jax import lax
from jax.experimental import pallas as pl
from jax.experimental.pallas import tpu as pltpu

F32 = jnp.float32
BF16 = jnp.bfloat16

EPS = 1e-6
ROPE_BASE = 10000.0
GRID_W = 64
LANES = 128
SUBLANES = 8
HALO = 16
DIFF_DH = 64
DIFF_HEADS = 8
DIFF_DV = 128
CONV_W = 31
ML_HEADS = 8
ML_DK = 128
ML_DV = 256
ML_CONV_W = 3
SCAN_CHUNK = 256
ATTN_QSCALE = DIFF_DH ** -0.5 * math.log2(math.e)
VMEM_LIMIT = 56 * 1024 * 1024
TILES = dict(
    hy_in=dict(tm=512, tn=2560), ml_in=dict(tm=1024, tn=2048), ctx_in_tn=1024,
    attn=dict(tq=2048, sub=256, kchunk=512), conv_tt=256, out_tm=512, ffn=dict(tm=512, tf=1024),
)


def _params(sem):
    return pltpu.CompilerParams(dimension_semantics=sem, vmem_limit_bytes=VMEM_LIMIT)


def _sigmoid(x):
    return 1.0 / (1.0 + jnp.exp(-x))


def _runtime_zero():
    return jnp.minimum(pl.program_id(0), 0)


def _ada_kernel(c_ref, w_ref, b_ref, o_ref):
    c = c_ref[...]
    s = (c * _sigmoid(c)).astype(BF16)
    o_ref[0] = jnp.dot(s, w_ref[0].astype(BF16), preferred_element_type=F32) + b_ref[0]


def _ada(cvec, ada_w, ada_b):
    depth, d, n = ada_w.shape
    tn = 1024
    return pl.pallas_call(
        _ada_kernel,
        grid=(depth, n // tn),
        in_specs=[pl.BlockSpec((8, d), lambda l, j: (0, 0)),
                  pl.BlockSpec((1, d, tn), lambda l, j: (l, 0, j)),
                  pl.BlockSpec((1, 1, tn), lambda l, j: (l, 0, j))],
        out_specs=pl.BlockSpec((1, 8, tn), lambda l, j: (l, 0, j)),
        out_shape=jax.ShapeDtypeStruct((depth, 8, n), F32),
        compiler_params=_params(("parallel", "parallel")),
        name="ada_ln",
    )(cvec, ada_w, ada_b.reshape(depth, 1, n))


ROW_CHUNK = 16


def _prenorm_into(h_ref, x_ref, g, shift, scale):
    gs = g * (1.0 + scale)
    for r0 in range(0, x_ref.shape[0], ROW_CHUNK):
        x = x_ref[r0:r0 + ROW_CHUNK, :]
        r = lax.rsqrt(jnp.mean(x * x, axis=-1, keepdims=True) + EPS)
        h_ref[r0:r0 + ROW_CHUNK, :] = (x * r * gs + shift).astype(h_ref.dtype)


def _prenorm_mm_kernel(*refs, rope_cols, tn, qscale_cols, qscale, has_side):
    x_ref, g_ref, sh_ref, sc_ref, w_ref = refs[:5]
    rest = list(refs[5:])
    if rope_cols:
        cos_ref, sina_ref, sinb_ref = rest[:3]
        rest = rest[3:]
    if has_side:
        ws_ref, o_ref, os_ref, h_ref = rest
    else:
        o_ref, h_ref = rest
    j = pl.program_id(1)

    @pl.when(j == 0)
    def _():
        _prenorm_into(h_ref, x_ref, g_ref[...], sh_ref[0], sc_ref[0])
        if has_side:
            os_ref[...] = jnp.dot(h_ref[...], ws_ref[...], preferred_element_type=F32)

    acc = jnp.dot(h_ref[...], w_ref[...], preferred_element_type=F32)
    if not (rope_cols or qscale_cols):
        o_ref[...] = acc.astype(o_ref.dtype)
        return
    for c in range(tn // LANES):
        col0 = j * tn + c * LANES
        xs = acc[:, c * LANES:(c + 1) * LANES]
        if qscale_cols:
            xs = xs * jnp.where(col0 < qscale_cols, qscale, 1.0)
        if rope_cols:
            sel = jnp.where(col0 < rope_cols, 0, 1)
            rot = pltpu.roll(xs, LANES - 16, 1) * sina_ref[sel] + pltpu.roll(xs, 16, 1) * sinb_ref[sel]
            xs = xs * cos_ref[sel] + rot
        o_ref[:, c * LANES:(c + 1) * LANES] = xs.astype(o_ref.dtype)


def _prenorm_mm(x, g, shift, scale, w, n_out, *, tm, tn, rows_per_mod, out_dtype, rope=None, rope_cols=0, qscale_cols=0,
                qscale=1.0, w_side=None, name):
    m, d = x.shape
    mod_map = lambda i, j: ((i * tm) // rows_per_mod, 0, 0)
    in_specs = [pl.BlockSpec((tm, d), lambda i, j: (i, 0)),
                pl.BlockSpec((1, d), lambda i, j: (0, 0)),
                pl.BlockSpec((1, 1, d), mod_map),
                pl.BlockSpec((1, 1, d), mod_map),
                pl.BlockSpec((d, tn), lambda i, j: (0, j))]
    args = [x, g.reshape(1, d), shift, scale, w]
    if rope is not None:
        assert rope_cols % LANES == 0 and qscale_cols % LANES == 0
        t_len = rope[0].shape[0]
        assert t_len % tm == 0
        nblk = t_len // tm
        for tab, ident in zip(rope, (1.0, 0.0, 0.0)):
            in_specs.append(pl.BlockSpec((2, tm, LANES), lambda i, j: (0, i % nblk, 0)))
            args.append(jnp.stack([tab, jnp.full_like(tab, ident)]))
    else:
        rope_cols = 0
    out_specs = pl.BlockSpec((tm, tn), lambda i, j: (i, j))
    out_shape = jax.ShapeDtypeStruct((m, n_out), out_dtype)
    if w_side is not None:
        ns = w_side.shape[1]
        in_specs.append(pl.BlockSpec((d, ns), lambda i, j: (0, 0)))
        args.append(w_side)
        out_specs = (out_specs, pl.BlockSpec((tm, ns), lambda i, j: (i, 0)))
        out_shape = (out_shape, jax.ShapeDtypeStruct((m, ns), F32))
    return pl.pallas_call(
        functools.partial(_prenorm_mm_kernel, rope_cols=rope_cols, tn=tn, qscale_cols=qscale_cols, qscale=qscale,
                          has_side=w_side is not None),
        grid=(m // tm, n_out // tn),
        in_specs=in_specs,
        out_specs=out_specs,
        out_shape=out_shape,
        scratch_shapes=[pltpu.VMEM((tm, d), BF16)],
        compiler_params=_params(("parallel", "arbitrary")),
        name=name,
    )(*args)


def _rope_tables(t_len):
    rows = t_len // GRID_W
    r = jnp.repeat(jnp.arange(rows, dtype=F32), GRID_W)
    col = jnp.tile(jnp.arange(GRID_W, dtype=F32), rows)
    n_freq = DIFF_DH // 4
    inv = ROPE_BASE ** (-jnp.arange(n_freq, dtype=F32) / n_freq)
    ar = r[:, None] * inv
    ac = col[:, None] * inv
    ang = jnp.concatenate([ar, ar, ac, ac], axis=-1)
    cos = jnp.tile(jnp.cos(ang), (1, LANES // DIFF_DH))
    sin = jnp.tile(jnp.sin(ang), (1, LANES // DIFF_DH))
    first = (jnp.arange(LANES) % (2 * n_freq)) < n_freq
    return cos, jnp.where(first, -sin, 0.0), jnp.where(first, 0.0, sin)


def _attn_kernel(*refs, lam_init, two_sources, sub, kchunk):
    if two_sources:
        q_ref, k_ref, v_ref, kc_ref, vc_ref, lq1_ref, lk1_ref, lq2_ref, lk2_ref, g_ref, o_ref, vx_ref, *ss_refs = refs
    else:
        q_ref, k_ref, v_ref, lq1_ref, lk1_ref, lq2_ref, lk2_ref, g_ref, o_ref, vx_ref, *ss_refs = refs
    z = _runtime_zero()
    lam = (jnp.exp(jnp.sum(lq1_ref[...] * lk1_ref[...], axis=-1, keepdims=True))
           - jnp.exp(jnp.sum(lq2_ref[...] * lk2_ref[...], axis=-1, keepdims=True)) + lam_init)
    dn = (((1,), (1,)), ((), ()))
    tq = q_ref.shape[1]
    n_keys = k_ref.shape[1]
    chunks = [(k_ref, c0, min(kchunk, n_keys - c0), c0) for c0 in range(0, n_keys, kchunk)]
    if two_sources:
        chunks.append((kc_ref, 0, kc_ref.shape[1], n_keys))

    @pl.when(pl.program_id(2) == 0)
    def _():
        srcs = [(v_ref, 0)] + ([(vc_ref, n_keys)] if two_sources else [])
        for vr, off in srcs:
            n = vr.shape[1]
            vx_ref[off:off + n, 0:DIFF_DV] = vr[0]
            vx_ref[off:off + n, DIFF_DV:2 * DIFF_DV] = (
                lax.broadcasted_iota(jnp.int32, (n, DIFF_DV), 1) == 0).astype(BF16)

    nblk = tq // sub
    st = [dict() for _ in range(2 * nblk)]

    def qk(u, j):
        x, c = divmod(u, 2)
        d = st[u]
        if j == 0:
            q = q_ref[0, x * sub:(x + 1) * sub, :]
            lane = lax.broadcasted_iota(jnp.int32, q.shape, 1)
            keep = (lane < DIFF_DH) if c == 0 else (lane >= DIFF_DH)
            d["q"] = jnp.where(keep, q, jnp.zeros_like(q))
        kr, c0, n, off = chunks[j]
        s = lax.dot_general(d["q"], kr[0, c0:c0 + n, :], dn, preferred_element_type=F32)
        ss_refs[u % 2][z, :, off:off + n] = s
        mj = jnp.max(s, axis=-1, keepdims=True)
        d["m"] = mj if j == 0 else jnp.maximum(d["m"], mj)

    def ev(u, j):
        x, c = divmod(u, 2)
        d = st[u]
        _, _, n, voff = chunks[j]
        p = jnp.exp2(ss_refs[u % 2][z, :, voff:voff + n] - d["m"]).astype(BF16)
        part = jnp.dot(p, vx_ref[voff:voff + n, :], preferred_element_type=F32)
        d["acc"] = part if j == 0 else d["acc"] + part
        if j == len(chunks) - 1:
            acc = d["acc"]
            on = acc[:, 0:DIFF_DV] * (1.0 / acc[:, DIFF_DV:DIFF_DV + 1])
            o1 = d.get("o1")
            d.clear()
            if c == 0:
                st[u + 1]["o1"] = on
            else:
                o = o1 - lam * on
                r = lax.rsqrt(jnp.mean(o * o, axis=-1, keepdims=True) + EPS)
                o_ref[0, x * sub:(x + 1) * sub, :] = (o * r * g_ref[...] * (1.0 - lam_init)).astype(o_ref.dtype)

    nch = len(chunks)
    for u in range(2 * nblk + 1):
        for j in range(nch):
            if u < 2 * nblk:
                qk(u, j)
            if u >= 1:
                ev(u - 1, j)


def _diff_attn(zq, zkv_extra, lams, subln_g, lam_init, *, tq, sub, kchunk):
    bsz, t_len, _ = zq.shape
    nh = DIFF_HEADS
    two = zkv_extra is not None
    n_all = t_len + (zkv_extra.shape[1] if two else 0)
    in_specs = [pl.BlockSpec((1, tq, LANES), lambda b, h, i: (b, i, h)),
                pl.BlockSpec((1, t_len, LANES), lambda b, h, i: (b, 0, nh + h)),
                pl.BlockSpec((1, t_len, LANES), lambda b, h, i: (b, 0, 2 * nh + h))]
    args = [zq, zq, zq]
    if two:
        c_len = zkv_extra.shape[1]
        in_specs += [pl.BlockSpec((1, c_len, LANES), lambda b, h, i: (b, 0, nh + h)),
                     pl.BlockSpec((1, c_len, LANES), lambda b, h, i: (b, 0, 2 * nh + h))]
        args += [zkv_extra, zkv_extra]
    for v in lams:
        in_specs.append(pl.BlockSpec((1, DIFF_DH), lambda b, h, i: (0, 0)))
        args.append(v.reshape(1, DIFF_DH))
    in_specs.append(pl.BlockSpec((1, DIFF_DV), lambda b, h, i: (0, 0)))
    args.append(subln_g.reshape(1, DIFF_DV))
    return pl.pallas_call(
        functools.partial(_attn_kernel, lam_init=lam_init, two_sources=two, sub=sub, kchunk=kchunk),
        grid=(bsz, nh, t_len // tq),
        in_specs=in_specs,
        out_specs=pl.BlockSpec((1, tq, LANES), lambda b, h, i: (b, i, h)),
        out_shape=jax.ShapeDtypeStruct((bsz, t_len, nh * DIFF_DV), BF16),
        scratch_shapes=[pltpu.VMEM((n_all, 2 * DIFF_DV), BF16), pltpu.VMEM((1, sub, n_all), F32),
                        pltpu.VMEM((1, sub, n_all), F32)],
        compiler_params=_params(("parallel", "parallel", "arbitrary")),
        name="diff_attn" + ("_lat" if two else "_ctx"),
    )(*args)


def _dwconv_phases(width):
    offs = [HALO - width // 2 + w for w in range(width)]
    return sorted({o % SUBLANES for o in offs} - {0}), max(offs) // SUBLANES * SUBLANES


def _dwconv_scratch(tt, width, nch):
    phases, reach = _dwconv_phases(width)
    return pltpu.VMEM((max(len(phases), 1), tt + reach, nch), F32)


def _dwconv_into(y_ref, ysh_ref, w_ref, b_ref, out_ref, *, tt, width, nch):
    phases, reach = _dwconv_phases(width)
    for idx, p in enumerate(phases):
        ysh_ref[idx] = y_ref[p:p + tt + reach, :]
    rc = 64
    for c0 in range(0, nch, LANES):
        for r0 in range(0, tt, rc):
            acc = jnp.broadcast_to(b_ref[:, c0:c0 + LANES], (rc, LANES))
            for w in range(width):
                off = HALO - width // 2 + w
                p, start = off % SUBLANES, r0 + off // SUBLANES * SUBLANES
                if p == 0:
                    tap = y_ref[start:start + rc, c0:c0 + LANES]
                else:
                    tap = ysh_ref[phases.index(p), start:start + rc, c0:c0 + LANES]
                acc = acc + tap * w_ref[w:w + 1, c0:c0 + LANES]
            out_ref[r0:r0 + rc, c0:c0 + LANES] = acc


def _convmod_kernel(ap_ref, a_ref, an_ref, gp_ref, g_ref, gn_ref, cw_ref, cb_ref, lng_ref, lnb_ref, o_ref,
                    y_ref, c_ref, ysh_ref, *, tt, nch):
    i = pl.program_id(1)
    n = pl.num_programs(1)

    def glu(a, g):
        return a.astype(F32) * _sigmoid(g.astype(F32))

    y_ref[0:HALO] = jnp.where(i > 0, glu(ap_ref[0], gp_ref[0]), 0.0)
    y_ref[HALO:HALO + tt] = glu(a_ref[0], g_ref[0])
    y_ref[HALO + tt:2 * HALO + tt] = jnp.where(i < n - 1, glu(an_ref[0], gn_ref[0]), 0.0)
    _dwconv_into(y_ref, ysh_ref, cw_ref, cb_ref, c_ref, tt=tt, width=CONV_W, nch=nch)
    c = c_ref[...]
    mu = jnp.mean(c, axis=-1, keepdims=True)
    xc = c - mu
    var = jnp.mean(xc * xc, axis=-1, keepdims=True)
    y = xc * lax.rsqrt(var + EPS) * lng_ref[...] + lnb_ref[...]
    o_ref[0] = (y * _sigmoid(y)).astype(o_ref.dtype)


def _halo_specs(tt, t_len, width, colblk):
    per = tt // HALO
    last = t_len // HALO - 1
    return (pl.BlockSpec((1, HALO, width), lambda b, i: (b, jnp.maximum(i * per - 1, 0), colblk)),
            pl.BlockSpec((1, tt, width), lambda b, i: (b, i, colblk)),
            pl.BlockSpec((1, HALO, width), lambda b, i: (b, jnp.minimum((i + 1) * per, last), colblk)))


def _conv_module(z, conv_w, conv_b, ln_g, ln_b, *, tt):
    bsz, t_len, n = z.shape
    nch = conv_w.shape[1]
    a_blk = (n - 2 * nch) // nch
    vec = lambda: pl.BlockSpec((1, nch), lambda b, i: (0, 0))
    return pl.pallas_call(
        functools.partial(_convmod_kernel, tt=tt, nch=nch),
        grid=(bsz, t_len // tt),
        in_specs=[*_halo_specs(tt, t_len, nch, a_blk), *_halo_specs(tt, t_len, nch, a_blk + 1),
                  pl.BlockSpec((CONV_W, nch), lambda b, i: (0, 0)), vec(), vec(), vec()],
        out_specs=pl.BlockSpec((1, tt, nch), lambda b, i: (b, i, 0)),
        out_shape=jax.ShapeDtypeStruct((bsz, t_len, nch), BF16),
        scratch_shapes=[pltpu.VMEM((tt + 2 * HALO, nch), F32), pltpu.VMEM((tt, nch), F32),
                        _dwconv_scratch(tt, CONV_W, nch)],
        compiler_params=_params(("parallel", "arbitrary")),
        name="conformer_conv",
    )(z, z, z, z, z, z, conv_w, conv_b.reshape(1, nch), ln_g.reshape(1, nch), ln_b.reshape(1, nch))


def _conv3_kernel(zp_ref, z_ref, zn_ref, cw_ref, cb_ref, o_ref, y_ref, c_ref, ysh_ref, *, tt, nch, kscale):
    i = pl.program_id(1)
    n = pl.num_programs(1)
    y_ref[0:HALO] = jnp.where(i > 0, zp_ref[0].astype(F32), 0.0)
    y_ref[HALO:HALO + tt] = z_ref[0].astype(F32)
    y_ref[HALO + tt:2 * HALO + tt] = jnp.where(i < n - 1, zn_ref[0].astype(F32), 0.0)
    _dwconv_into(y_ref, ysh_ref, cw_ref, cb_ref, c_ref, tt=tt, width=ML_CONV_W, nch=nch)
    c = c_ref[...]
    s = c * _sigmoid(c)
    half = nch // 2
    o_ref[0, :, 0:half] = s[:, 0:half].astype(o_ref.dtype)
    o_ref[0, :, half:nch] = (s[:, half:nch] * kscale).astype(o_ref.dtype)


def _conv3_silu(z, conv_w, conv_b, *, tt):
    bsz, t_len, _ = z.shape
    nch = conv_w.shape[1]
    return pl.pallas_call(
        functools.partial(_conv3_kernel, tt=tt, nch=nch, kscale=ML_DK ** -0.5),
        grid=(bsz, t_len // tt),
        in_specs=[*_halo_specs(tt, t_len, nch, 0),
                  pl.BlockSpec((ML_CONV_W, nch), lambda b, i: (0, 0)),
                  pl.BlockSpec((1, nch), lambda b, i: (0, 0))],
        out_specs=pl.BlockSpec((1, tt, nch), lambda b, i: (b, i, 0)),
        out_shape=jax.ShapeDtypeStruct((bsz, t_len, nch), BF16),
        scratch_shapes=[pltpu.VMEM((tt + 2 * HALO, nch), F32), pltpu.VMEM((tt, nch), F32),
                        _dwconv_scratch(tt, ML_CONV_W, nch)],
        compiler_params=_params(("parallel", "arbitrary")),
        name="ml_conv3_silu",
    )(z, z, z, conv_w, conv_b.reshape(1, nch))


def _post_residual(y, x, gate, gpost):
    r = lax.rsqrt(jnp.mean(y * y, axis=-1, keepdims=True) + EPS)
    return x + gate * (y * r * gpost)


OUT_SUB = 256


def _hy_out_kernel(a_ref, c_ref, x_ref, gate_ref, gpost_ref, w_ref, o_ref):
    ka = a_ref.shape[1]
    y = (jnp.dot(a_ref[...], w_ref[0:ka, :], preferred_element_type=F32)
         + jnp.dot(c_ref[...], w_ref[ka:, :], preferred_element_type=F32))
    o_ref[...] = _post_residual(y, x_ref[...], gate_ref[0], gpost_ref[...])


def _hy_out(attn, conv, x, gate, gpost, w, *, tm, rows_per_mod):
    m, d = x.shape
    ka, kc = attn.shape[1], conv.shape[1]
    return pl.pallas_call(
        _hy_out_kernel,
        grid=(m // tm,),
        in_specs=[pl.BlockSpec((tm, ka), lambda i: (i, 0)),
                  pl.BlockSpec((tm, kc), lambda i: (i, 0)),
                  pl.BlockSpec((tm, d), lambda i: (i, 0)),
                  pl.BlockSpec((1, 1, d), lambda i: ((i * tm) // rows_per_mod, 0, 0)),
                  pl.BlockSpec((1, d), lambda i: (0, 0)),
                  pl.BlockSpec((ka + kc, d), lambda i: (0, 0), pipeline_mode=pl.Buffered(1))],
        out_specs=pl.BlockSpec((tm, d), lambda i: (i, 0)),
        out_shape=jax.ShapeDtypeStruct((m, d), F32),
        compiler_params=_params(("parallel",)),
        name="hybrid_out_proj",
    )(attn, conv, x, gate, gpost.reshape(1, d), w)


def _ml_out_kernel(hf_ref, hb_ref, og_ref, x_ref, ng_ref, gate_ref, gpost_ref, w_ref, o_ref):
    def gated(rows):
        hsum = hf_ref[0, rows, :].astype(F32) + hb_ref[0, rows, :].astype(F32)
        parts = []
        for h in range(ML_HEADS):
            hh = hsum[:, h * ML_DV:(h + 1) * ML_DV]
            r = lax.rsqrt(jnp.mean(hh * hh, axis=-1, keepdims=True) + EPS)
            parts.append(hh * r * ng_ref[:, h * ML_DV:(h + 1) * ML_DV])
        hn = jnp.concatenate(parts, axis=1)
        return (hn * _sigmoid(og_ref[rows, :].astype(F32))).astype(BF16)

    subs = [slice(r0, r0 + OUT_SUB) for r0 in range(0, x_ref.shape[0], OUT_SUB)]
    a = gated(subs[0])
    y_prev = None
    for idx, rows in enumerate(subs):
        y = jnp.dot(a, w_ref[...], preferred_element_type=F32)
        if idx + 1 < len(subs):
            a = gated(subs[idx + 1])
        if y_prev is not None:
            prev = subs[idx - 1]
            o_ref[prev, :] = _post_residual(y_prev, x_ref[prev, :], gate_ref[0], gpost_ref[...])
        y_prev = y
    o_ref[subs[-1], :] = _post_residual(y_prev, x_ref[subs[-1], :], gate_ref[0], gpost_ref[...])


def _ml_out(h2, z, o_blk, x, norm_g, gate, gpost, w, *, tm, rows_per_mod):
    m, d = x.shape
    kv = h2.shape[2]
    return pl.pallas_call(
        _ml_out_kernel,
        grid=(m // tm,),
        in_specs=[pl.BlockSpec((1, tm, kv), lambda i: (0, i, 0)),
                  pl.BlockSpec((1, tm, kv), lambda i: (1, i, 0)),
                  pl.BlockSpec((tm, kv), lambda i: (i, o_blk)),
                  pl.BlockSpec((tm, d), lambda i: (i, 0)),
                  pl.BlockSpec((1, kv), lambda i: (0, 0)),
                  pl.BlockSpec((1, 1, d), lambda i: ((i * tm) // rows_per_mod, 0, 0)),
                  pl.BlockSpec((1, d), lambda i: (0, 0)),
                  pl.BlockSpec((kv, d), lambda i: (0, 0), pipeline_mode=pl.Buffered(1))],
        out_specs=pl.BlockSpec((tm, d), lambda i: (i, 0)),
        out_shape=jax.ShapeDtypeStruct((m, d), F32),
        compiler_params=_params(("parallel",)),
        name="mlstm_out_proj",
    )(h2, h2, z, x, norm_g.reshape(1, kv), gate, gpost.reshape(1, d), w)


def _ffn_kernel(x_ref, g_ref, sh_ref, sc_ref, gate_ref, gpost_ref, w1_ref, w2_ref, o_ref, h_ref, acc_ref):
    f = pl.program_id(1)
    nf = pl.num_programs(1)

    @pl.when(f == 0)
    def _():
        _prenorm_into(h_ref, x_ref, g_ref[...], sh_ref[0], sc_ref[0])
        acc_ref[...] = jnp.zeros_like(acc_ref)

    t = jnp.maximum(jnp.dot(h_ref[...], w1_ref[0], preferred_element_type=F32), 0.0)
    acc_ref[...] += jnp.dot((t * t).astype(BF16), w2_ref[0], preferred_element_type=F32)

    @pl.when(f == nf - 1)
    def _():
        for r0 in range(0, x_ref.shape[0], ROW_CHUNK):
            rows = slice(r0, r0 + ROW_CHUNK)
            o_ref[rows, :] = _post_residual(acc_ref[rows, :], x_ref[rows, :], gate_ref[0], gpost_ref[...])


def _ffn(x, g, shift, scale, gate, gpost, w1, w2, layer, *, tm, tf, rows_per_mod):
    m, d = x.shape
    dff = w1.shape[2]
    mod_map = lambda i, f: ((i * tm) // rows_per_mod, 0, 0)
    vec = lambda: pl.BlockSpec((1, d), lambda i, f: (0, 0))
    return pl.pallas_call(
        _ffn_kernel,
        grid=(m // tm, dff // tf),
        in_specs=[pl.BlockSpec((tm, d), lambda i, f: (i, 0)), vec(),
                  pl.BlockSpec((1, 1, d), mod_map), pl.BlockSpec((1, 1, d), mod_map),
                  pl.BlockSpec((1, 1, d), mod_map), vec(),
                  pl.BlockSpec((1, d, tf), lambda i, f: (layer, 0, f)),
                  pl.BlockSpec((1, tf, d), lambda i, f: (layer, f, 0))],
        out_specs=pl.BlockSpec((tm, d), lambda i, f: (i, 0)),
        out_shape=jax.ShapeDtypeStruct((m, d), F32),
        scratch_shapes=[pltpu.VMEM((tm, d), BF16), pltpu.VMEM((tm, d), F32)],
        compiler_params=_params(("parallel", "arbitrary")),
        name="ffn",
    )(x, g.reshape(1, d), shift, scale, gate, gpost.reshape(1, d), w1, w2)


def _scan_kernel(ql_ref, kl_ref, vl_ref, gil_ref, gfl_ref, qc_ref, kc_ref, vc_ref, gic_ref, gfc_ref,
                 bi_ref, bf_ref, o_ref, c_ref, m_ref):
    L, H, DK, DV = SCAN_CHUNK, ML_HEADS, ML_DK, ML_DV
    fwd = pl.program_id(0) == 0
    s = pl.program_id(2)

    @pl.when(s == 0)
    def _():
        c_ref[...] = jnp.zeros_like(c_ref)
        m_ref[...] = jnp.zeros_like(m_ref)

    is_ctx = s == 0
    q = jnp.where(is_ctx, qc_ref[0], ql_ref[0])
    k = jnp.where(is_ctx, kc_ref[0], kl_ref[0])
    v = jnp.where(is_ctx, vc_ref[0], vl_ref[0])
    ipre = jnp.where(is_ctx, gic_ref[0, 0], gil_ref[0, 0]) + bi_ref[0]
    fpre = jnp.where(is_ctx, gfc_ref[0, 0], gfl_ref[0, 0]) + bf_ref[0]
    lf = jnp.minimum(fpre, 0.0) - jnp.log1p(jnp.exp(-jnp.abs(fpre)))

    row = lax.broadcasted_iota(jnp.int32, (L, L), 0)
    col = lax.broadcasted_iota(jnp.int32, (L, L), 1)
    delta = (row - col) * jnp.where(fwd, 1, -1)
    tri = jnp.where(delta <= 0, 1.0, 0.0).astype(F32)
    b_all = jnp.dot(lf, tri, precision=lax.Precision.HIGHEST, preferred_element_type=F32)
    r_all = ipre - b_all
    b_last_all = jnp.where(fwd, b_all[:, L - 1:L], b_all[:, 0:1])
    mask = delta >= 0
    ones_blk = (lax.broadcasted_iota(jnp.int32, (L, LANES), 1) == 0).astype(BF16)
    dn_t = (((1,), (1,)), ((), ()))

    for h in range(H):
        b_row = b_all[h:h + 1, :]
        r_row = r_all[h:h + 1, :]
        bcol = jnp.transpose(jnp.broadcast_to(b_row, (LANES, L)))
        bcol_l = jnp.concatenate([bcol] * (L // LANES), axis=1)
        dm = jnp.where(mask, bcol_l + r_row, -jnp.inf)
        m_loc = jnp.max(dm, axis=1, keepdims=True)
        m_prev = m_ref[h:h + 1, 0:1]
        g = bcol[:, 0:1] + m_prev
        m_row = jnp.maximum(g, m_loc)
        inter = jnp.exp(g - m_row)
        p = jnp.exp(dm - m_row)
        qh = q[:, h * DK:(h + 1) * DK]
        kh = k[:, h * DK:(h + 1) * DK]
        vext = jnp.concatenate([v[:, h * DV:(h + 1) * DV], ones_blk], axis=1)
        sm = (lax.dot_general(qh, kh, dn_t, preferred_element_type=F32) * p).astype(BF16)
        cst = c_ref[h]
        nd = (inter * jnp.dot(qh, cst.astype(BF16), preferred_element_type=F32)
              + jnp.dot(sm, vext, preferred_element_type=F32))
        den = jnp.maximum(jnp.abs(nd[:, DV:DV + 1]), jnp.exp(-m_row))
        o_ref[0, 0, :, h * DV:(h + 1) * DV] = (nd[:, 0:DV] * (1.0 / den)).astype(o_ref.dtype)

        b_last = b_last_all[h:h + 1, :]
        wlog = b_last + r_row
        m_new = jnp.maximum(b_last + m_prev, jnp.max(wlog, axis=1, keepdims=True))
        decay = jnp.exp(b_last + m_prev - m_new)
        kt = jnp.transpose(kh.astype(F32))
        ktw = (kt * jnp.exp(wlog - m_new)).astype(BF16)
        c_ref[h] = decay * cst + jnp.dot(ktw, vext, preferred_element_type=F32)
        m_ref[h:h + 1, :] = jnp.broadcast_to(m_new, (1, LANES))


def _mlstm_scan(qk_l, z_l, gates_l, qk_c, z_c, gates_c, gate_b):
    L, H = SCAN_CHUNK, ML_HEADS
    bsz, t_len, _ = qk_l.shape
    assert qk_c.shape[1] == L
    n_lat = t_len // L
    hk, hv = H * ML_DK, H * ML_DV

    def lat(d, s):
        return jnp.where(d == 0, jnp.maximum(s - 1, 0), n_lat - jnp.maximum(s, 1))

    in_specs = [
        pl.BlockSpec((1, L, hk), lambda d, b, s: (b, lat(d, s), 0)),
        pl.BlockSpec((1, L, hk), lambda d, b, s: (b, lat(d, s), 1)),
        pl.BlockSpec((1, L, hv), lambda d, b, s: (b, lat(d, s), 1)),
        pl.BlockSpec((1, 1, H, L), lambda d, b, s: (b, 2 * d, 0, lat(d, s))),
        pl.BlockSpec((1, 1, H, L), lambda d, b, s: (b, 2 * d + 1, 0, lat(d, s))),
        pl.BlockSpec((1, L, hk), lambda d, b, s: (b, 0, 0)),
        pl.BlockSpec((1, L, hk), lambda d, b, s: (b, 0, 1)),
        pl.BlockSpec((1, L, hv), lambda d, b, s: (b, 0, 1)),
        pl.BlockSpec((1, 1, H, L), lambda d, b, s: (b, 2 * d, 0, 0)),
        pl.BlockSpec((1, 1, H, L), lambda d, b, s: (b, 2 * d + 1, 0, 0)),
        pl.BlockSpec((1, H, 1), lambda d, b, s: (2 * d, 0, 0)),
        pl.BlockSpec((1, H, 1), lambda d, b, s: (2 * d + 1, 0, 0)),
    ]
    return pl.pallas_call(
        _scan_kernel,
        grid=(2, bsz, n_lat + 1),
        in_specs=in_specs,
        out_specs=pl.BlockSpec((1, 1, L, hv), lambda d, b, s: (d, b, lat(d, s), 0)),
        out_shape=jax.ShapeDtypeStruct((2, bsz, t_len, hv), BF16),
        scratch_shapes=[pltpu.VMEM((H, ML_DK, ML_DV + LANES), F32), pltpu.VMEM((H, LANES), F32)],
        compiler_params=_params(("parallel", "parallel", "arbitrary")),
        name="mlstm_scan",
    )(qk_l, qk_l, z_l, gates_l, gates_l, qk_c, qk_c, z_c, gates_c, gates_c,
      gate_b.reshape(4, H, 1), gate_b.reshape(4, H, 1))


def kernel(x, c, ctx, c_ctx, ada_w, ada_b, g_pre_mix, g_post_mix, g_pre_ffn, g_post_ffn, ffn_w1, ffn_w2, hy_w_in, hy_w_out, diff_lq1, diff_lk1, diff_lq2, diff_lk2, diff_subln_g, conv_w, conv_b, conv_ln_g, conv_ln_b, ml_w_in, ml_conv_w, ml_conv_b, ml_gate_b, ml_norm_g, ml_w_out):
    bsz, t_len, d = x.shape
    c_len = ctx.shape[1]
    depth = ada_w.shape[0]
    assert depth == 2 and bsz <= 7 and c_len == SCAN_CHUNK
    ml = t_len * bsz
    mc = c_len * bsz

    cvec = jnp.concatenate([c, c_ctx[None, :], jnp.zeros((8 - bsz - 1, d), F32)], axis=0)
    mods = _ada(cvec, ada_w, ada_b)

    def mod_lat(l, k):
        return mods[l, :bsz, k * d:(k + 1) * d].reshape(bsz, 1, d)

    def mod_ctx(l, k):
        return mods[l, bsz:bsz + 1, k * d:(k + 1) * d].reshape(1, 1, d)

    xl = x.reshape(ml, d)
    xc = ctx.reshape(mc, d)
    w1 = ffn_w1.astype(BF16)
    w2 = ffn_w2.astype(BF16)

    l = 0
    lam_init = 0.8 - 0.6 * math.exp(-0.3 * l)
    w_in = hy_w_in[0].astype(BF16)
    n_in = w_in.shape[1]
    qk_w = 2 * DIFF_HEADS * 2 * DIFF_DH
    rope = _rope_tables(t_len)
    zl = _prenorm_mm(xl, g_pre_mix[l], mod_lat(l, 0), mod_lat(l, 1), w_in, n_in, **TILES["hy_in"],
                     rows_per_mod=t_len, out_dtype=BF16, rope=rope, rope_cols=qk_w, qscale_cols=qk_w // 2,
                     qscale=ATTN_QSCALE, name="hy_in_proj_lat")
    zc = _prenorm_mm(xc, g_pre_mix[l], mod_ctx(l, 0), mod_ctx(l, 1), w_in, n_in, tm=mc, tn=TILES["ctx_in_tn"],
                     rows_per_mod=mc, out_dtype=BF16, qscale_cols=qk_w // 2, qscale=ATTN_QSCALE,
                     name="hy_in_proj_ctx")
    zl3 = zl.reshape(bsz, t_len, n_in)
    zc3 = zc.reshape(bsz, c_len, n_in)
    lams = (diff_lq1[0], diff_lk1[0], diff_lq2[0], diff_lk2[0])
    attn_l = _diff_attn(zl3, zc3, lams, diff_subln_g[0], lam_init, **TILES["attn"])
    attn_c = _diff_attn(zc3, None, lams, diff_subln_g[0], lam_init, tq=c_len, sub=c_len, kchunk=c_len)
    conv_l = _conv_module(zl3, conv_w[0], conv_b[0], conv_ln_g[0], conv_ln_b[0], tt=TILES["conv_tt"])
    conv_c = _conv_module(zc3, conv_w[0], conv_b[0], conv_ln_g[0], conv_ln_b[0], tt=c_len)
    w_out = hy_w_out[0].astype(BF16)
    xl = _hy_out(attn_l.reshape(ml, -1), conv_l.reshape(ml, -1), xl, mod_lat(l, 2), g_post_mix[l], w_out,
                 tm=TILES["out_tm"], rows_per_mod=t_len)
    xc = _hy_out(attn_c.reshape(mc, -1), conv_c.reshape(mc, -1), xc, mod_ctx(l, 2), g_post_mix[l], w_out,
                 tm=TILES["out_tm"], rows_per_mod=mc)
    xl = _ffn(xl, g_pre_ffn[l], mod_lat(l, 3), mod_lat(l, 4), mod_lat(l, 5), g_post_ffn[l], w1, w2, l,
              **TILES["ffn"], rows_per_mod=t_len)
    xc = _ffn(xc, g_pre_ffn[l], mod_ctx(l, 3), mod_ctx(l, 4), mod_ctx(l, 5), g_post_ffn[l], w1, w2, l,
              **TILES["ffn"], rows_per_mod=mc)

    l = 1
    qkv_w = 2 * ML_HEADS * ML_DK + ML_HEADS * ML_DV
    n_gates = 4 * ML_HEADS
    wm = ml_w_in[0].astype(BF16)
    w_main = jnp.concatenate([wm[:, :qkv_w], wm[:, qkv_w + n_gates:]], axis=1)
    w_gate = jnp.pad(wm[:, qkv_w:qkv_w + n_gates], ((0, 0), (0, LANES - n_gates)))
    n_main = w_main.shape[1]
    zl, gl = _prenorm_mm(xl, g_pre_mix[l], mod_lat(l, 0), mod_lat(l, 1), w_main, n_main, **TILES["ml_in"],
                         rows_per_mod=t_len, out_dtype=BF16, w_side=w_gate, name="ml_in_proj_lat")
    zc, gc = _prenorm_mm(xc, g_pre_mix[l], mod_ctx(l, 0), mod_ctx(l, 1), w_main, qkv_w, tm=mc, tn=TILES["ml_in"]["tn"],
                         rows_per_mod=mc, out_dtype=BF16, w_side=w_gate, name="ml_in_proj_ctx")
    zl3 = zl.reshape(bsz, t_len, n_main)
    zc3 = zc.reshape(bsz, c_len, qkv_w)
    qk_l = _conv3_silu(zl3, ml_conv_w[0], ml_conv_b[0], tt=TILES["conv_tt"])
    qk_c = _conv3_silu(zc3, ml_conv_w[0], ml_conv_b[0], tt=c_len)

    def gates_t(gm, n):
        return gm[:, :n_gates].reshape(bsz, n, 4, ML_HEADS).transpose(0, 2, 3, 1)

    h2 = _mlstm_scan(qk_l, zl3, gates_t(gl, t_len), qk_c, zc3, gates_t(gc, c_len), ml_gate_b[0])
    xl = _ml_out(h2.reshape(2, ml, -1), zl, qkv_w // (ML_HEADS * ML_DV), xl, ml_norm_g[0], mod_lat(l, 2),
                 g_post_mix[l], ml_w_out[0].astype(BF16), tm=TILES["out_tm"], rows_per_mod=t_len)
    xl = _ffn(xl, g_pre_ffn[l], mod_lat(l, 3), mod_lat(l, 4), mod_lat(l, 5), g_post_ffn[l], w1, w2, l,
              **TILES["ffn"], rows_per_mod=t_len)
    return xl.reshape(bsz, t_len, d)
```

```python
import functools
import math

import jax
import jax.numpy as jnp
from jax import lax
from jax.experimental import pallas as pl
from jax.experimental.pallas import tpu as pltpu

F32 = jnp.float32
BF16 = jnp.bfloat16

EPS = 1e-6
ROPE_BASE = 10000.0
GRID_W = 64
LANES = 128
SUBLANES = 8
HALO = 16
DIFF_DH = 64
DIFF_HEADS = 8
DIFF_DV = 128
CONV_W = 31
ML_HEADS = 8
ML_DK = 128
ML_DV = 256
ML_CONV_W = 3
SCAN_CHUNK = 256
ATTN_QSCALE = DIFF_DH ** -0.5 * math.log2(math.e)
VMEM_LIMIT = 56 * 1024 * 1024
TILES = dict(
    hy_in=dict(tm=512, tn=2560), ml_in=dict(tm=1024, tn=2048), ctx_in_tn=1024,
    attn=dict(tq=2048, sub=256, kchunk=1024), conv_tt=256, out_tm=512, ffn=dict(tm=512, tf=1024),
)


def _params(sem):
    return pltpu.CompilerParams(dimension_semantics=sem, vmem_limit_bytes=VMEM_LIMIT)


def _sigmoid(x):
    return 1.0 / (1.0 + jnp.exp(-x))


def _runtime_zero():
    return jnp.minimum(pl.program_id(0), 0)


def _ada_kernel(c_ref, w_ref, b_ref, o_ref):
    c = c_ref[...]
    s = (c * _sigmoid(c)).astype(BF16)
    o_ref[0] = jnp.dot(s, w_ref[0].astype(BF16), preferred_element_type=F32) + b_ref[0]


def _ada(cvec, ada_w, ada_b):
    depth, d, n = ada_w.shape
    tn = 1024
    return pl.pallas_call(
        _ada_kernel,
        grid=(depth, n // tn),
        in_specs=[pl.BlockSpec((8, d), lambda l, j: (0, 0)),
                  pl.BlockSpec((1, d, tn), lambda l, j: (l, 0, j)),
                  pl.BlockSpec((1, 1, tn), lambda l, j: (l, 0, j))],
        out_specs=pl.BlockSpec((1, 8, tn), lambda l, j: (l, 0, j)),
        out_shape=jax.ShapeDtypeStruct((depth, 8, n), F32),
        compiler_params=_params(("parallel", "parallel")),
        name="ada_ln",
    )(cvec, ada_w, ada_b.reshape(depth, 1, n))


ROW_CHUNK = 16


def _prenorm_into(h_ref, x_ref, g, shift, scale):
    gs = g * (1.0 + scale)
    for r0 in range(0, x_ref.shape[0], ROW_CHUNK):
        x = x_ref[r0:r0 + ROW_CHUNK, :]
        r = lax.rsqrt(jnp.mean(x * x, axis=-1, keepdims=True) + EPS)
        h_ref[r0:r0 + ROW_CHUNK, :] = (x * r * gs + shift).astype(h_ref.dtype)


def _prenorm_mm_kernel(*refs, rope_cols, tn, qscale_cols, qscale, has_side):
    x_ref, g_ref, sh_ref, sc_ref, w_ref = refs[:5]
    rest = list(refs[5:])
    if rope_cols:
        cos_ref, sina_ref, sinb_ref = rest[:3]
        rest = rest[3:]
    if has_side:
        ws_ref, o_ref, os_ref, h_ref = rest
    else:
        o_ref, h_ref = rest
    j = pl.program_id(1)

    @pl.when(j == 0)
    def _():
        _prenorm_into(h_ref, x_ref, g_ref[...], sh_ref[0], sc_ref[0])
        if has_side:
            os_ref[...] = jnp.dot(h_ref[...], ws_ref[...], preferred_element_type=F32)

    acc = jnp.dot(h_ref[...], w_ref[...], preferred_element_type=F32)
    if not (rope_cols or qscale_cols):
        o_ref[...] = acc.astype(o_ref.dtype)
        return
    for c in range(tn // LANES):
        col0 = j * tn + c * LANES
        xs = acc[:, c * LANES:(c + 1) * LANES]
        if qscale_cols:
            xs = xs * jnp.where(col0 < qscale_cols, qscale, 1.0)
        if rope_cols:
            sel = jnp.where(col0 < rope_cols, 0, 1)
            rot = pltpu.roll(xs, LANES - 16, 1) * sina_ref[sel] + pltpu.roll(xs, 16, 1) * sinb_ref[sel]
            xs = xs * cos_ref[sel] + rot
        o_ref[:, c * LANES:(c + 1) * LANES] = xs.astype(o_ref.dtype)


def _prenorm_mm(x, g, shift, scale, w, n_out, *, tm, tn, rows_per_mod, out_dtype, rope=None, rope_cols=0, qscale_cols=0,
                qscale=1.0, w_side=None, name):
    m, d = x.shape
    mod_map = lambda i, j: ((i * tm) // rows_per_mod, 0, 0)
    in_specs = [pl.BlockSpec((tm, d), lambda i, j: (i, 0)),
                pl.BlockSpec((1, d), lambda i, j: (0, 0)),
                pl.BlockSpec((1, 1, d), mod_map),
                pl.BlockSpec((1, 1, d), mod_map),
                pl.BlockSpec((d, tn), lambda i, j: (0, j))]
    args = [x, g.reshape(1, d), shift, scale, w]
    if rope is not None:
        assert rope_cols % LANES == 0 and qscale_cols % LANES == 0
        t_len = rope[0].shape[0]
        assert t_len % tm == 0
        nblk = t_len // tm
        for tab, ident in zip(rope, (1.0, 0.0, 0.0)):
            in_specs.append(pl.BlockSpec((2, tm, LANES), lambda i, j: (0, i % nblk, 0)))
            args.append(jnp.stack([tab, jnp.full_like(tab, ident)]))
    else:
        rope_cols = 0
    out_specs = pl.BlockSpec((tm, tn), lambda i, j: (i, j))
    out_shape = jax.ShapeDtypeStruct((m, n_out), out_dtype)
    if w_side is not None:
        ns = w_side.shape[1]
        in_specs.append(pl.BlockSpec((d, ns), lambda i, j: (0, 0)))
        args.append(w_side)
        out_specs = (out_specs, pl.BlockSpec((tm, ns), lambda i, j: (i, 0)))
        out_shape = (out_shape, jax.ShapeDtypeStruct((m, ns), F32))
    return pl.pallas_call(
        functools.partial(_prenorm_mm_kernel, rope_cols=rope_cols, tn=tn, qscale_cols=qscale_cols, qscale=qscale,
                          has_side=w_side is not None),
        grid=(m // tm, n_out // tn),
        in_specs=in_specs,
        out_specs=out_specs,
        out_shape=out_shape,
        scratch_shapes=[pltpu.VMEM((tm, d), BF16)],
        compiler_params=_params(("parallel", "arbitrary")),
        name=name,
    )(*args)


def _rope_tables(t_len):
    rows = t_len // GRID_W
    r = jnp.repeat(jnp.arange(rows, dtype=F32), GRID_W)
    col = jnp.tile(jnp.arange(GRID_W, dtype=F32), rows)
    n_freq = DIFF_DH // 4
    inv = ROPE_BASE ** (-jnp.arange(n_freq, dtype=F32) / n_freq)
    ar = r[:, None] * inv
    ac = col[:, None] * inv
    ang = jnp.concatenate([ar, ar, ac, ac], axis=-1)
    cos = jnp.tile(jnp.cos(ang), (1, LANES // DIFF_DH))
    sin = jnp.tile(jnp.sin(ang), (1, LANES // DIFF_DH))
    first = (jnp.arange(LANES) % (2 * n_freq)) < n_freq
    return cos, jnp.where(first, -sin, 0.0), jnp.where(first, 0.0, sin)


def _attn_kernel(*refs, lam_init, two_sources, sub, kchunk):
    if two_sources:
        q_ref, k_ref, v_ref, kc_ref, vc_ref, lq1_ref, lk1_ref, lq2_ref, lk2_ref, g_ref, o_ref, vx_ref, *ss_refs = refs
    else:
        q_ref, k_ref, v_ref, lq1_ref, lk1_ref, lq2_ref, lk2_ref, g_ref, o_ref, vx_ref, *ss_refs = refs
    z = _runtime_zero()
    lam = (jnp.exp(jnp.sum(lq1_ref[...] * lk1_ref[...], axis=-1, keepdims=True))
           - jnp.exp(jnp.sum(lq2_ref[...] * lk2_ref[...], axis=-1, keepdims=True)) + lam_init)
    dn = (((1,), (1,)), ((), ()))
    tq = q_ref.shape[1]
    n_keys = k_ref.shape[1]
    chunks = [(k_ref, c0, min(kchunk, n_keys - c0), c0) for c0 in range(0, n_keys, kchunk)]
    if two_sources:
        chunks.append((kc_ref, 0, kc_ref.shape[1], n_keys))

    @pl.when(pl.program_id(2) == 0)
    def _():
        srcs = [(v_ref, 0)] + ([(vc_ref, n_keys)] if two_sources else [])
        for vr, off in srcs:
            n = vr.shape[1]
            vx_ref[off:off + n, 0:DIFF_DV] = vr[0]
            vx_ref[off:off + n, DIFF_DV:2 * DIFF_DV] = (
                lax.broadcasted_iota(jnp.int32, (n, DIFF_DV), 1) == 0).astype(BF16)

    nblk = tq // sub
    st = [dict() for _ in range(2 * nblk)]

    def qk(u, j):
        x, c = divmod(u, 2)
        d = st[u]
        if j == 0:
            q = q_ref[0, x * sub:(x + 1) * sub, :]
            lane = lax.broadcasted_iota(jnp.int32, q.shape, 1)
            keep = (lane < DIFF_DH) if c == 0 else (lane >= DIFF_DH)
            d["q"] = jnp.where(keep, q, jnp.zeros_like(q))
        kr, c0, n, off = chunks[j]
        s = lax.dot_general(d["q"], kr[0, c0:c0 + n, :], dn, preferred_element_type=F32)
        ss_refs[u % 2][z, :, off:off + n] = s
        mj = jnp.max(s, axis=-1, keepdims=True)
        d["m"] = mj if j == 0 else jnp.maximum(d["m"], mj)

    def ev(u, j):
        x, c = divmod(u, 2)
        d = st[u]
        _, _, n, voff = chunks[j]
        p = jnp.exp2(ss_refs[u % 2][z, :, voff:voff + n] - d["m"]).astype(BF16)
        part = jnp.dot(p, vx_ref[voff:voff + n, :], preferred_element_type=F32)
        d["acc"] = part if j == 0 else d["acc"] + part
        if j == len(chunks) - 1:
            acc = d["acc"]
            on = acc[:, 0:DIFF_DV] * (1.0 / acc[:, DIFF_DV:DIFF_DV + 1])
            o1 = d.get("o1")
            d.clear()
            if c == 0:
                st[u + 1]["o1"] = on
            else:
                o = o1 - lam * on
                r = lax.rsqrt(jnp.mean(o * o, axis=-1, keepdims=True) + EPS)
                o_ref[0, x * sub:(x + 1) * sub, :] = (o * r * g_ref[...] * (1.0 - lam_init)).astype(o_ref.dtype)

    nch = len(chunks)
    for u in range(2 * nblk + 1):
        for j in range(nch):
            if u < 2 * nblk:
                qk(u, j)
            if u >= 1:
                ev(u - 1, j)


def _diff_attn(zq, zkv_extra, lams, subln_g, lam_init, *, tq, sub, kchunk):
    bsz, t_len, _ = zq.shape
    nh = DIFF_HEADS
    two = zkv_extra is not None
    n_all = t_len + (zkv_extra.shape[1] if two else 0)
    in_specs = [pl.BlockSpec((1, tq, LANES), lambda b, h, i: (b, i, h)),
                pl.BlockSpec((1, t_len, LANES), lambda b, h, i: (b, 0, nh + h)),
                pl.BlockSpec((1, t_len, LANES), lambda b, h, i: (b, 0, 2 * nh + h))]
    args = [zq, zq, zq]
    if two:
        c_len = zkv_extra.shape[1]
        in_specs += [pl.BlockSpec((1, c_len, LANES), lambda b, h, i: (b, 0, nh + h)),
                     pl.BlockSpec((1, c_len, LANES), lambda b, h, i: (b, 0, 2 * nh + h))]
        args += [zkv_extra, zkv_extra]
    for v in lams:
        in_specs.append(pl.BlockSpec((1, DIFF_DH), lambda b, h, i: (0, 0)))
        args.append(v.reshape(1, DIFF_DH))
    in_specs.append(pl.BlockSpec((1, DIFF_DV), lambda b, h, i: (0, 0)))
    args.append(subln_g.reshape(1, DIFF_DV))
    return pl.pallas_call(
        functools.partial(_attn_kernel, lam_init=lam_init, two_sources=two, sub=sub, kchunk=kchunk),
        grid=(bsz, nh, t_len // tq),
        in_specs=in_specs,
        out_specs=pl.BlockSpec((1, tq, LANES), lambda b, h, i: (b, i, h)),
        out_shape=jax.ShapeDtypeStruct((bsz, t_len, nh * DIFF_DV), BF16),
        scratch_shapes=[pltpu.VMEM((n_all, 2 * DIFF_DV), BF16), pltpu.VMEM((1, sub, n_all), F32),
                        pltpu.VMEM((1, sub, n_all), F32)],
        compiler_params=_params(("parallel", "parallel", "arbitrary")),
        name="diff_attn" + ("_lat" if two else "_ctx"),
    )(*args)


def _dwconv_phases(width):
    offs = [HALO - width // 2 + w for w in range(width)]
    return sorted({o % SUBLANES for o in offs} - {0}), max(offs) // SUBLANES * SUBLANES


def _dwconv_scratch(tt, width, nch):
    phases, reach = _dwconv_phases(width)
    return pltpu.VMEM((max(len(phases), 1), tt + reach, nch), F32)


def _dwconv_into(y_ref, ysh_ref, w_ref, b_ref, out_ref, *, tt, width, nch):
    phases, reach = _dwconv_phases(width)
    for idx, p in enumerate(phases):
        ysh_ref[idx] = y_ref[p:p + tt + reach, :]
    rc = 64
    for c0 in range(0, nch, LANES):
        for r0 in range(0, tt, rc):
            acc = jnp.broadcast_to(b_ref[:, c0:c0 + LANES], (rc, LANES))
            for w in range(width):
                off = HALO - width // 2 + w
                p, start = off % SUBLANES, r0 + off // SUBLANES * SUBLANES
                if p == 0:
                    tap = y_ref[start:start + rc, c0:c0 + LANES]
                else:
                    tap = ysh_ref[phases.index(p), start:start + rc, c0:c0 + LANES]
                acc = acc + tap * w_ref[w:w + 1, c0:c0 + LANES]
            out_ref[r0:r0 + rc, c0:c0 + LANES] = acc


def _convmod_kernel(ap_ref, a_ref, an_ref, gp_ref, g_ref, gn_ref, cw_ref, cb_ref, lng_ref, lnb_ref, o_ref,
                    y_ref, c_ref, ysh_ref, *, tt, nch):
    i = pl.program_id(1)
    n = pl.num_programs(1)

    def glu(a, g):
        return a.astype(F32) * _sigmoid(g.astype(F32))

    y_ref[0:HALO] = jnp.where(i > 0, glu(ap_ref[0], gp_ref[0]), 0.0)
    y_ref[HALO:HALO + tt] = glu(a_ref[0], g_ref[0])
    y_ref[HALO + tt:2 * HALO + tt] = jnp.where(i < n - 1, glu(an_ref[0], gn_ref[0]), 0.0)
    _dwconv_into(y_ref, ysh_ref, cw_ref, cb_ref, c_ref, tt=tt, width=CONV_W, nch=nch)
    c = c_ref[...]
    mu = jnp.mean(c, axis=-1, keepdims=True)
    xc = c - mu
    var = jnp.mean(xc * xc, axis=-1, keepdims=True)
    y = xc * lax.rsqrt(var + EPS) * lng_ref[...] + lnb_ref[...]
    o_ref[0] = (y * _sigmoid(y)).astype(o_ref.dtype)


def _halo_specs(tt, t_len, width, colblk):
    per = tt // HALO
    last = t_len // HALO - 1
    return (pl.BlockSpec((1, HALO, width), lambda b, i: (b, jnp.maximum(i * per - 1, 0), colblk)),
            pl.BlockSpec((1, tt, width), lambda b, i: (b, i, colblk)),
            pl.BlockSpec((1, HALO, width), lambda b, i: (b, jnp.minimum((i + 1) * per, last), colblk)))


def _conv_module(z, conv_w, conv_b, ln_g, ln_b, *, tt):
    bsz, t_len, n = z.shape
    nch = conv_w.shape[1]
    a_blk = (n - 2 * nch) // nch
    vec = lambda: pl.BlockSpec((1, nch), lambda b, i: (0, 0))
    return pl.pallas_call(
        functools.partial(_convmod_kernel, tt=tt, nch=nch),
        grid=(bsz, t_len // tt),
        in_specs=[*_halo_specs(tt, t_len, nch, a_blk), *_halo_specs(tt, t_len, nch, a_blk + 1),
                  pl.BlockSpec((CONV_W, nch), lambda b, i: (0, 0)), vec(), vec(), vec()],
        out_specs=pl.BlockSpec((1, tt, nch), lambda b, i: (b, i, 0)),
        out_shape=jax.ShapeDtypeStruct((bsz, t_len, nch), BF16),
        scratch_shapes=[pltpu.VMEM((tt + 2 * HALO, nch), F32), pltpu.VMEM((tt, nch), F32),
                        _dwconv_scratch(tt, CONV_W, nch)],
        compiler_params=_params(("parallel", "arbitrary")),
        name="conformer_conv",
    )(z, z, z, z, z, z, conv_w, conv_b.reshape(1, nch), ln_g.reshape(1, nch), ln_b.reshape(1, nch))


def _conv3_kernel(zp_ref, z_ref, zn_ref, cw_ref, cb_ref, o_ref, y_ref, c_ref, ysh_ref, *, tt, nch, kscale):
    i = pl.program_id(1)
    n = pl.num_programs(1)
    y_ref[0:HALO] = jnp.where(i > 0, zp_ref[0].astype(F32), 0.0)
    y_ref[HALO:HALO + tt] = z_ref[0].astype(F32)
    y_ref[HALO + tt:2 * HALO + tt] = jnp.where(i < n - 1, zn_ref[0].astype(F32), 0.0)
    _dwconv_into(y_ref, ysh_ref, cw_ref, cb_ref, c_ref, tt=tt, width=ML_CONV_W, nch=nch)
    c = c_ref[...]
    s = c * _sigmoid(c)
    half = nch // 2
    o_ref[0, :, 0:half] = s[:, 0:half].astype(o_ref.dtype)
    o_ref[0, :, half:nch] = (s[:, half:nch] * kscale).astype(o_ref.dtype)


def _conv3_silu(z, conv_w, conv_b, *, tt):
    bsz, t_len, _ = z.shape
    nch = conv_w.shape[1]
    return pl.pallas_call(
        functools.partial(_conv3_kernel, tt=tt, nch=nch, kscale=ML_DK ** -0.5),
        grid=(bsz, t_len // tt),
        in_specs=[*_halo_specs(tt, t_len, nch, 0),
                  pl.BlockSpec((ML_CONV_W, nch), lambda b, i: (0, 0)),
                  pl.BlockSpec((1, nch), lambda b, i: (0, 0))],
        out_specs=pl.BlockSpec((1, tt, nch), lambda b, i: (b, i, 0)),
        out_shape=jax.ShapeDtypeStruct((bsz, t_len, nch), BF16),
        scratch_shapes=[pltpu.VMEM((tt + 2 * HALO, nch), F32), pltpu.VMEM((tt, nch), F32),
                        _dwconv_scratch(tt, ML_CONV_W, nch)],
        compiler_params=_params(("parallel", "arbitrary")),
        name="ml_conv3_silu",
    )(z, z, z, conv_w, conv_b.reshape(1, nch))


def _post_residual(y, x, gate, gpost):
    r = lax.rsqrt(jnp.mean(y * y, axis=-1, keepdims=True) + EPS)
    return x + gate * (y * r * gpost)


OUT_SUB = 256


def _hy_out_kernel(a_ref, c_ref, x_ref, gate_ref, gpost_ref, w_ref, o_ref):
    ka = a_ref.shape[1]
    y = (jnp.dot(a_ref[...], w_ref[0:ka, :], preferred_element_type=F32)
         + jnp.dot(c_ref[...], w_ref[ka:, :], preferred_element_type=F32))
    o_ref[...] = _post_residual(y, x_ref[...], gate_ref[0], gpost_ref[...])


def _hy_out(attn, conv, x, gate, gpost, w, *, tm, rows_per_mod):
    m, d = x.shape
    ka, kc = attn.shape[1], conv.shape[1]
    return pl.pallas_call(
        _hy_out_kernel,
        grid=(m // tm,),
        in_specs=[pl.BlockSpec((tm, ka), lambda i: (i, 0)),
                  pl.BlockSpec((tm, kc), lambda i: (i, 0)),
                  pl.BlockSpec((tm, d), lambda i: (i, 0)),
                  pl.BlockSpec((1, 1, d), lambda i: ((i * tm) // rows_per_mod, 0, 0)),
                  pl.BlockSpec((1, d), lambda i: (0, 0)),
                  pl.BlockSpec((ka + kc, d), lambda i: (0, 0), pipeline_mode=pl.Buffered(1))],
        out_specs=pl.BlockSpec((tm, d), lambda i: (i, 0)),
        out_shape=jax.ShapeDtypeStruct((m, d), F32),
        compiler_params=_params(("parallel",)),
        name="hybrid_out_proj",
    )(attn, conv, x, gate, gpost.reshape(1, d), w)


def _ml_out_kernel(hf_ref, hb_ref, og_ref, x_ref, ng_ref, gate_ref, gpost_ref, w_ref, o_ref):
    def gated(rows):
        hsum = hf_ref[0, rows, :].astype(F32) + hb_ref[0, rows, :].astype(F32)
        parts = []
        for h in range(ML_HEADS):
            hh = hsum[:, h * ML_DV:(h + 1) * ML_DV]
            r = lax.rsqrt(jnp.mean(hh * hh, axis=-1, keepdims=True) + EPS)
            parts.append(hh * r * ng_ref[:, h * ML_DV:(h + 1) * ML_DV])
        hn = jnp.concatenate(parts, axis=1)
        return (hn * _sigmoid(og_ref[rows, :].astype(F32))).astype(BF16)

    subs = [slice(r0, r0 + OUT_SUB) for r0 in range(0, x_ref.shape[0], OUT_SUB)]
    a = gated(subs[0])
    y_prev = None
    for idx, rows in enumerate(subs):
        y = jnp.dot(a, w_ref[...], preferred_element_type=F32)
        if idx + 1 < len(subs):
            a = gated(subs[idx + 1])
        if y_prev is not None:
            prev = subs[idx - 1]
            o_ref[prev, :] = _post_residual(y_prev, x_ref[prev, :], gate_ref[0], gpost_ref[...])
        y_prev = y
    o_ref[subs[-1], :] = _post_residual(y_prev, x_ref[subs[-1], :], gate_ref[0], gpost_ref[...])


def _ml_out(h2, z, o_blk, x, norm_g, gate, gpost, w, *, tm, rows_per_mod):
    m, d = x.shape
    kv = h2.shape[2]
    return pl.pallas_call(
        _ml_out_kernel,
        grid=(m // tm,),
        in_specs=[pl.BlockSpec((1, tm, kv), lambda i: (0, i, 0)),
                  pl.BlockSpec((1, tm, kv), lambda i: (1, i, 0)),
                  pl.BlockSpec((tm, kv), lambda i: (i, o_blk)),
                  pl.BlockSpec((tm, d), lambda i: (i, 0)),
                  pl.BlockSpec((1, kv), lambda i: (0, 0)),
                  pl.BlockSpec((1, 1, d), lambda i: ((i * tm) // rows_per_mod, 0, 0)),
                  pl.BlockSpec((1, d), lambda i: (0, 0)),
                  pl.BlockSpec((kv, d), lambda i: (0, 0), pipeline_mode=pl.Buffered(1))],
        out_specs=pl.BlockSpec((tm, d), lambda i: (i, 0)),
        out_shape=jax.ShapeDtypeStruct((m, d), F32),
        compiler_params=_params(("parallel",)),
        name="mlstm_out_proj",
    )(h2, h2, z, x, norm_g.reshape(1, kv), gate, gpost.reshape(1, d), w)


def _ffn_kernel(x_ref, g_ref, sh_ref, sc_ref, gate_ref, gpost_ref, w1_ref, w2_ref, o_ref, h_ref, acc_ref):
    f = pl.program_id(1)
    nf = pl.num_programs(1)

    @pl.when(f == 0)
    def _():
        _prenorm_into(h_ref, x_ref, g_ref[...], sh_ref[0], sc_ref[0])
        acc_ref[...] = jnp.zeros_like(acc_ref)

    t = jnp.maximum(jnp.dot(h_ref[...], w1_ref[0], preferred_element_type=F32), 0.0)
    acc_ref[...] += jnp.dot((t * t).astype(BF16), w2_ref[0], preferred_element_type=F32)

    @pl.when(f == nf - 1)
    def _():
        for r0 in range(0, x_ref.shape[0], ROW_CHUNK):
            rows = slice(r0, r0 + ROW_CHUNK)
            o_ref[rows, :] = _post_residual(acc_ref[rows, :], x_ref[rows, :], gate_ref[0], gpost_ref[...])


def _ffn(x, g, shift, scale, gate, gpost, w1, w2, layer, *, tm, tf, rows_per_mod):
    m, d = x.shape
    dff = w1.shape[2]
    mod_map = lambda i, f: ((i * tm) // rows_per_mod, 0, 0)
    vec = lambda: pl.BlockSpec((1, d), lambda i, f: (0, 0))
    return pl.pallas_call(
        _ffn_kernel,
        grid=(m // tm, dff // tf),
        in_specs=[pl.BlockSpec((tm, d), lambda i, f: (i, 0)), vec(),
                  pl.BlockSpec((1, 1, d), mod_map), pl.BlockSpec((1, 1, d), mod_map),
                  pl.BlockSpec((1, 1, d), mod_map), vec(),
                  pl.BlockSpec((1, d, tf), lambda i, f: (layer, 0, f)),
                  pl.BlockSpec((1, tf, d), lambda i, f: (layer, f, 0))],
        out_specs=pl.BlockSpec((tm, d), lambda i, f: (i, 0)),
        out_shape=jax.ShapeDtypeStruct((m, d), F32),
        scratch_shapes=[pltpu.VMEM((tm, d), BF16), pltpu.VMEM((tm, d), F32)],
        compiler_params=_params(("parallel", "arbitrary")),
        name="ffn",
    )(x, g.reshape(1, d), shift, scale, gate, gpost.reshape(1, d), w1, w2)


def _scan_kernel(ql_ref, kl_ref, vl_ref, gil_ref, gfl_ref, qc_ref, kc_ref, vc_ref, gic_ref, gfc_ref,
                 bi_ref, bf_ref, o_ref, c_ref, m_ref):
    L, H, DK, DV = SCAN_CHUNK, ML_HEADS, ML_DK, ML_DV
    fwd = pl.program_id(0) == 0
    s = pl.program_id(2)

    @pl.when(s == 0)
    def _():
        c_ref[...] = jnp.zeros_like(c_ref)
        m_ref[...] = jnp.zeros_like(m_ref)

    is_ctx = s == 0
    q = jnp.where(is_ctx, qc_ref[0], ql_ref[0])
    k = jnp.where(is_ctx, kc_ref[0], kl_ref[0])
    v = jnp.where(is_ctx, vc_ref[0], vl_ref[0])
    ipre = jnp.where(is_ctx, gic_ref[0, 0], gil_ref[0, 0]) + bi_ref[0]
    fpre = jnp.where(is_ctx, gfc_ref[0, 0], gfl_ref[0, 0]) + bf_ref[0]
    lf = jnp.minimum(fpre, 0.0) - jnp.log1p(jnp.exp(-jnp.abs(fpre)))

    row = lax.broadcasted_iota(jnp.int32, (L, L), 0)
    col = lax.broadcasted_iota(jnp.int32, (L, L), 1)
    delta = (row - col) * jnp.where(fwd, 1, -1)
    tri = jnp.where(delta <= 0, 1.0, 0.0).astype(F32)
    b_all = jnp.dot(lf, tri, precision=lax.Precision.HIGHEST, preferred_element_type=F32)
    r_all = ipre - b_all
    b_last_all = jnp.where(fwd, b_all[:, L - 1:L], b_all[:, 0:1])
    mask = delta >= 0
    ones_blk = (lax.broadcasted_iota(jnp.int32, (L, LANES), 1) == 0).astype(BF16)
    dn_t = (((1,), (1,)), ((), ()))

    for h in range(H):
        b_row = b_all[h:h + 1, :]
        r_row = r_all[h:h + 1, :]
        bcol = jnp.transpose(jnp.broadcast_to(b_row, (LANES, L)))
        bcol_l = jnp.concatenate([bcol] * (L // LANES), axis=1)
        dm = jnp.where(mask, bcol_l + r_row, -jnp.inf)
        m_loc = jnp.max(dm, axis=1, keepdims=True)
        m_prev = m_ref[h:h + 1, 0:1]
        g = bcol[:, 0:1] + m_prev
        m_row = jnp.maximum(g, m_loc)
        inter = jnp.exp(g - m_row)
        p = jnp.exp(dm - m_row)
        qh = q[:, h * DK:(h + 1) * DK]
        kh = k[:, h * DK:(h + 1) * DK]
        vext = jnp.concatenate([v[:, h * DV:(h + 1) * DV], ones_blk], axis=1)
        sm = (lax.dot_general(qh, kh, dn_t, preferred_element_type=F32) * p).astype(BF16)
        cst = c_ref[h]
        nd = (inter * jnp.dot(qh, cst.astype(BF16), preferred_element_type=F32)
              + jnp.dot(sm, vext, preferred_element_type=F32))
        den = jnp.maximum(jnp.abs(nd[:, DV:DV + 1]), jnp.exp(-m_row))
        o_ref[0, 0, :, h * DV:(h + 1) * DV] = (nd[:, 0:DV] * (1.0 / den)).astype(o_ref.dtype)

        b_last = b_last_all[h:h + 1, :]
        wlog = b_last + r_row
        m_new = jnp.maximum(b_last + m_prev, jnp.max(wlog, axis=1, keepdims=True))
        decay = jnp.exp(b_last + m_prev - m_new)
        kt = jnp.transpose(kh.astype(F32))
        ktw = (kt * jnp.exp(wlog - m_new)).astype(BF16)
        c_ref[h] = decay * cst + jnp.dot(ktw, vext, preferred_element_type=F32)
        m_ref[h:h + 1, :] = jnp.broadcast_to(m_new, (1, LANES))


def _mlstm_scan(qk_l, z_l, gates_l, qk_c, z_c, gates_c, gate_b):
    L, H = SCAN_CHUNK, ML_HEADS
    bsz, t_len, _ = qk_l.shape
    assert qk_c.shape[1] == L
    n_lat = t_len // L
    hk, hv = H * ML_DK, H * ML_DV

    def lat(d, s):
        return jnp.where(d == 0, jnp.maximum(s - 1, 0), n_lat - jnp.maximum(s, 1))

    in_specs = [
        pl.BlockSpec((1, L, hk), lambda d, b, s: (b, lat(d, s), 0)),
        pl.BlockSpec((1, L, hk), lambda d, b, s: (b, lat(d, s), 1)),
        pl.BlockSpec((1, L, hv), lambda d, b, s: (b, lat(d, s), 1)),
        pl.BlockSpec((1, 1, H, L), lambda d, b, s: (b, 2 * d, 0, lat(d, s))),
        pl.BlockSpec((1, 1, H, L), lambda d, b, s: (b, 2 * d + 1, 0, lat(d, s))),
        pl.BlockSpec((1, L, hk), lambda d, b, s: (b, 0, 0)),
        pl.BlockSpec((1, L, hk), lambda d, b, s: (b, 0, 1)),
        pl.BlockSpec((1, L, hv), lambda d, b, s: (b, 0, 1)),
        pl.BlockSpec((1, 1, H, L), lambda d, b, s: (b, 2 * d, 0, 0)),
        pl.BlockSpec((1, 1, H, L), lambda d, b, s: (b, 2 * d + 1, 0, 0)),
        pl.BlockSpec((1, H, 1), lambda d, b, s: (2 * d, 0, 0)),
        pl.BlockSpec((1, H, 1), lambda d, b, s: (2 * d + 1, 0, 0)),
    ]
    return pl.pallas_call(
        _scan_kernel,
        grid=(2, bsz, n_lat + 1),
        in_specs=in_specs,
        out_specs=pl.BlockSpec((1, 1, L, hv), lambda d, b, s: (d, b, lat(d, s), 0)),
        out_shape=jax.ShapeDtypeStruct((2, bsz, t_len, hv), BF16),
        scratch_shapes=[pltpu.VMEM((H, ML_DK, ML_DV + LANES), F32), pltpu.VMEM((H, LANES), F32)],
        compiler_params=_params(("parallel", "parallel", "arbitrary")),
        name="mlstm_scan",
    )(qk_l, qk_l, z_l, gates_l, gates_l, qk_c, qk_c, z_c, gates_c, gates_c,
      gate_b.reshape(4, H, 1), gate_b.reshape(4, H, 1))


def kernel(x, c, ctx, c_ctx, ada_w, ada_b, g_pre_mix, g_post_mix, g_pre_ffn, g_post_ffn, ffn_w1, ffn_w2, hy_w_in, hy_w_out, diff_lq1, diff_lk1, diff_lq2, diff_lk2, diff_subln_g, conv_w, conv_b, conv_ln_g, conv_ln_b, ml_w_in, ml_conv_w, ml_conv_b, ml_gate_b, ml_norm_g, ml_w_out):
    bsz, t_len, d = x.shape
    c_len = ctx.shape[1]
    depth = ada_w.shape[0]
    assert depth == 2 and bsz <= 7 and c_len == SCAN_CHUNK
    ml = t_len * bsz
    mc = c_len * bsz

    cvec = jnp.concatenate([c, c_ctx[None, :], jnp.zeros((8 - bsz - 1, d), F32)], axis=0)
    mods = _ada(cvec, ada_w, ada_b)

    def mod_lat(l, k):
        return mods[l, :bsz, k * d:(k + 1) * d].reshape(bsz, 1, d)

    def mod_ctx(l, k):
        return mods[l, bsz:bsz + 1, k * d:(k + 1) * d].reshape(1, 1, d)

    xl = x.reshape(ml, d)
    xc = ctx.reshape(mc, d)
    w1 = ffn_w1.astype(BF16)
    w2 = ffn_w2.astype(BF16)

    l = 0
    lam_init = 0.8 - 0.6 * math.exp(-0.3 * l)
    w_in = hy_w_in[0].astype(BF16)
    n_in = w_in.shape[1]
    qk_w = 2 * DIFF_HEADS * 2 * DIFF_DH
    rope = _rope_tables(t_len)
    zl = _prenorm_mm(xl, g_pre_mix[l], mod_lat(l, 0), mod_lat(l, 1), w_in, n_in, **TILES["hy_in"],
                     rows_per_mod=t_len, out_dtype=BF16, rope=rope, rope_cols=qk_w, qscale_cols=qk_w // 2,
                     qscale=ATTN_QSCALE, name="hy_in_proj_lat")
    zc = _prenorm_mm(xc, g_pre_mix[l], mod_ctx(l, 0), mod_ctx(l, 1), w_in, n_in, tm=mc, tn=TILES["ctx_in_tn"],
                     rows_per_mod=mc, out_dtype=BF16, qscale_cols=qk_w // 2, qscale=ATTN_QSCALE,
                     name="hy_in_proj_ctx")
    zl3 = zl.reshape(bsz, t_len, n_in)
    zc3 = zc.reshape(bsz, c_len, n_in)
    lams = (diff_lq1[0], diff_lk1[0], diff_lq2[0], diff_lk2[0])
    attn_l = _diff_attn(zl3, zc3, lams, diff_subln_g[0], lam_init, **TILES["attn"])
    attn_c = _diff_attn(zc3, None, lams, diff_subln_g[0], lam_init, tq=c_len, sub=c_len, kchunk=c_len)
    conv_l = _conv_module(zl3, conv_w[0], conv_b[0], conv_ln_g[0], conv_ln_b[0], tt=TILES["conv_tt"])
    conv_c = _conv_module(zc3, conv_w[0], conv_b[0], conv_ln_g[0], conv_ln_b[0], tt=c_len)
    w_out = hy_w_out[0].astype(BF16)
    xl = _hy_out(attn_l.reshape(ml, -1), conv_l.reshape(ml, -1), xl, mod_lat(l, 2), g_post_mix[l], w_out,
                 tm=TILES["out_tm"], rows_per_mod=t_len)
    xc = _hy_out(attn_c.reshape(mc, -1), conv_c.reshape(mc, -1), xc, mod_ctx(l, 2), g_post_mix[l], w_out,
                 tm=TILES["out_tm"], rows_per_mod=mc)
    xl = _ffn(xl, g_pre_ffn[l], mod_lat(l, 3), mod_lat(l, 4), mod_lat(l, 5), g_post_ffn[l], w1, w2, l,
              **TILES["ffn"], rows_per_mod=t_len)
    xc = _ffn(xc, g_pre_ffn[l], mod_ctx(l, 3), mod_ctx(l, 4), mod_ctx(l, 5), g_post_ffn[l], w1, w2, l,
              **TILES["ffn"], rows_per_mod=mc)

    l = 1
    qkv_w = 2 * ML_HEADS * ML_DK + ML_HEADS * ML_DV
    n_gates = 4 * ML_HEADS
    wm = ml_w_in[0].astype(BF16)
    w_main = jnp.concatenate([wm[:, :qkv_w], wm[:, qkv_w + n_gates:]], axis=1)
    w_gate = jnp.pad(wm[:, qkv_w:qkv_w + n_gates], ((0, 0), (0, LANES - n_gates)))
    n_main = w_main.shape[1]
    zl, gl = _prenorm_mm(xl, g_pre_mix[l], mod_lat(l, 0), mod_lat(l, 1), w_main, n_main, **TILES["ml_in"],
                         rows_per_mod=t_len, out_dtype=BF16, w_side=w_gate, name="ml_in_proj_lat")
    zc, gc = _prenorm_mm(xc, g_pre_mix[l], mod_ctx(l, 0), mod_ctx(l, 1), w_main, qkv_w, tm=mc, tn=TILES["ml_in"]["tn"],
                         rows_per_mod=mc, out_dtype=BF16, w_side=w_gate, name="ml_in_proj_ctx")
    zl3 = zl.reshape(bsz, t_len, n_main)
    zc3 = zc.reshape(bsz, c_len, qkv_w)
    qk_l = _conv3_silu(zl3, ml_conv_w[0], ml_conv_b[0], tt=TILES["conv_tt"])
    qk_c = _conv3_silu(zc3, ml_conv_w[0], ml_conv_b[0], tt=c_len)

    def gates_t(gm, n):
        return gm[:, :n_gates].reshape(bsz, n, 4, ML_HEADS).transpose(0, 2, 3, 1)

    h2 = _mlstm_scan(qk_l, zl3, gates_t(gl, t_len), qk_c, zc3, gates_t(gc, c_len), ml_gate_b[0])
    xl = _ml_out(h2.reshape(2, ml, -1), zl, qkv_w // (ML_HEADS * ML_DV), xl, ml_norm_g[0], mod_lat(l, 2),
                 g_post_mix[l], ml_w_out[0].astype(BF16), tm=TILES["out_tm"], rows_per_mod=t_len)
    xl = _ffn(xl, g_pre_ffn[l], mod_lat(l, 3), mod_lat(l, 4), mod_lat(l, 5), g_post_ffn[l], w1, w2, l,
              **TILES["ffn"], rows_per_mod=t_len)
    return xl.reshape(bsz, t_len, d)
```

```python
import functools
import math

import jax
import jax.numpy as jnp
from jax import lax
from jax.experimental import pallas as pl
from jax.experimental.pallas import tpu as pltpu

F32 = jnp.float32
BF16 = jnp.bfloat16

EPS = 1e-6
ROPE_BASE = 10000.0
GRID_W = 64
LANES = 128
SUBLANES = 8
HALO = 16
DIFF_DH = 64
DIFF_HEADS = 8
DIFF_DV = 128
CONV_W = 31
ML_HEADS = 8
ML_DK = 128
ML_DV = 256
ML_CONV_W = 3
SCAN_CHUNK = 256
ATTN_QSCALE = DIFF_DH ** -0.5 * math.log2(math.e)
VMEM_LIMIT = 56 * 1024 * 1024
TILES = dict(
    hy_in=dict(tm=512, tn=2560), ml_in=dict(tm=1024, tn=2048), ctx_in_tn=1024, ctx_ml_tn=2048,
    attn=dict(tq=2048, sub=256, kchunk=2048), conv_tt=512, out_tm=512, ffn=dict(tm=512, tf=1024),
)


def _params(sem):
    return pltpu.CompilerParams(dimension_semantics=sem, vmem_limit_bytes=VMEM_LIMIT)


def _sigmoid(x):
    return 1.0 / (1.0 + jnp.exp(-x))


def _runtime_zero():
    return jnp.minimum(pl.program_id(0), 0)


def _ada_kernel(c_ref, w_ref, b_ref, o_ref):
    c = c_ref[...]
    s = (c * _sigmoid(c)).astype(BF16)
    o_ref[0] = jnp.dot(s, w_ref[0].astype(BF16), preferred_element_type=F32) + b_ref[0]


def _ada(cvec, ada_w, ada_b):
    depth, d, n = ada_w.shape
    tn = 1024
    return pl.pallas_call(
        _ada_kernel,
        grid=(depth, n // tn),
        in_specs=[pl.BlockSpec((8, d), lambda l, j: (0, 0)),
                  pl.BlockSpec((1, d, tn), lambda l, j: (l, 0, j)),
                  pl.BlockSpec((1, 1, tn), lambda l, j: (l, 0, j))],
        out_specs=pl.BlockSpec((1, 8, tn), lambda l, j: (l, 0, j)),
        out_shape=jax.ShapeDtypeStruct((depth, 8, n), F32),
        compiler_params=_params(("parallel", "parallel")),
        name="ada_ln",
    )(cvec, ada_w, ada_b.reshape(depth, 1, n))


ROW_CHUNK = 16


def _prenorm_into(h_ref, x_ref, g, shift, scale):
    gs = g * (1.0 + scale)
    for r0 in range(0, x_ref.shape[0], ROW_CHUNK):
        x = x_ref[r0:r0 + ROW_CHUNK, :]
        r = lax.rsqrt(jnp.mean(x * x, axis=-1, keepdims=True) + EPS)
        h_ref[r0:r0 + ROW_CHUNK, :] = (x * r * gs + shift).astype(h_ref.dtype)


def _prenorm_mm_kernel(*refs, rope_cols, tn, qscale_cols, qscale, has_side):
    x_ref, g_ref, sh_ref, sc_ref, w_ref = refs[:5]
    rest = list(refs[5:])
    if rope_cols:
        cos_ref, sina_ref, sinb_ref = rest[:3]
        rest = rest[3:]
    if has_side:
        ws_ref, o_ref, os_ref, h_ref = rest
    else:
        o_ref, h_ref = rest
    j = pl.program_id(1)

    @pl.when(j == 0)
    def _():
        _prenorm_into(h_ref, x_ref, g_ref[...], sh_ref[0], sc_ref[0])
        if has_side:
            os_ref[...] = jnp.dot(h_ref[...], ws_ref[...], preferred_element_type=F32)

    acc = jnp.dot(h_ref[...], w_ref[...], preferred_element_type=F32)
    if not (rope_cols or qscale_cols):
        o_ref[...] = acc.astype(o_ref.dtype)
        return
    for c in range(tn // LANES):
        col0 = j * tn + c * LANES
        xs = acc[:, c * LANES:(c + 1) * LANES]
        if qscale_cols:
            xs = xs * jnp.where(col0 < qscale_cols, qscale, 1.0)
        if rope_cols:
            sel = jnp.where(col0 < rope_cols, 0, 1)
            rot = pltpu.roll(xs, LANES - 16, 1) * sina_ref[sel] + pltpu.roll(xs, 16, 1) * sinb_ref[sel]
            xs = xs * cos_ref[sel] + rot
        o_ref[:, c * LANES:(c + 1) * LANES] = xs.astype(o_ref.dtype)


def _prenorm_mm(x, g, shift, scale, w, n_out, *, tm, tn, rows_per_mod, out_dtype, rope=None, rope_cols=0, qscale_cols=0,
                qscale=1.0, w_side=None, name):
    m, d = x.shape
    mod_map = lambda i, j: ((i * tm) // rows_per_mod, 0, 0)
    in_specs = [pl.BlockSpec((tm, d), lambda i, j: (i, 0)),
                pl.BlockSpec((1, d), lambda i, j: (0, 0)),
                pl.BlockSpec((1, 1, d), mod_map),
                pl.BlockSpec((1, 1, d), mod_map),
                pl.BlockSpec((d, tn), lambda i, j: (0, j))]
    args = [x, g.reshape(1, d), shift, scale, w]
    if rope is not None:
        assert rope_cols % LANES == 0 and qscale_cols % LANES == 0
        t_len = rope[0].shape[0]
        assert t_len % tm == 0
        nblk = t_len // tm
        for tab, ident in zip(rope, (1.0, 0.0, 0.0)):
            in_specs.append(pl.BlockSpec((2, tm, LANES), lambda i, j: (0, i % nblk, 0)))
            args.append(jnp.stack([tab, jnp.full_like(tab, ident)]))
    else:
        rope_cols = 0
    out_specs = pl.BlockSpec((tm, tn), lambda i, j: (i, j))
    out_shape = jax.ShapeDtypeStruct((m, n_out), out_dtype)
    if w_side is not None:
        ns = w_side.shape[1]
        in_specs.append(pl.BlockSpec((d, ns), lambda i, j: (0, 0)))
        args.append(w_side)
        out_specs = (out_specs, pl.BlockSpec((tm, ns), lambda i, j: (i, 0)))
        out_shape = (out_shape, jax.ShapeDtypeStruct((m, ns), F32))
    return pl.pallas_call(
        functools.partial(_prenorm_mm_kernel, rope_cols=rope_cols, tn=tn, qscale_cols=qscale_cols, qscale=qscale,
                          has_side=w_side is not None),
        grid=(m // tm, n_out // tn),
        in_specs=in_specs,
        out_specs=out_specs,
        out_shape=out_shape,
        scratch_shapes=[pltpu.VMEM((tm, d), BF16)],
        compiler_params=_params(("parallel", "arbitrary")),
        name=name,
    )(*args)


def _rope_tables(t_len):
    rows = t_len // GRID_W
    r = jnp.repeat(jnp.arange(rows, dtype=F32), GRID_W)
    col = jnp.tile(jnp.arange(GRID_W, dtype=F32), rows)
    n_freq = DIFF_DH // 4
    inv = ROPE_BASE ** (-jnp.arange(n_freq, dtype=F32) / n_freq)
    ar = r[:, None] * inv
    ac = col[:, None] * inv
    ang = jnp.concatenate([ar, ar, ac, ac], axis=-1)
    cos = jnp.tile(jnp.cos(ang), (1, LANES // DIFF_DH))
    sin = jnp.tile(jnp.sin(ang), (1, LANES // DIFF_DH))
    first = (jnp.arange(LANES) % (2 * n_freq)) < n_freq
    return cos, jnp.where(first, -sin, 0.0), jnp.where(first, 0.0, sin)


def _attn_kernel(*refs, lam_init, two_sources, sub, kchunk, n_side):
    side_in, refs = refs[:n_side], refs[n_side:]
    if two_sources:
        q_ref, k_ref, v_ref, kc_ref, vc_ref, lq1_ref, lk1_ref, lq2_ref, lk2_ref, g_ref, o_ref, *rest = refs
    else:
        q_ref, k_ref, v_ref, lq1_ref, lk1_ref, lq2_ref, lk2_ref, g_ref, o_ref, *rest = refs
    side_out, (vx_ref, *ss_refs) = rest[:n_side], rest[n_side:]
    for src_ref, dst_ref in zip(side_in, side_out):
        dst_ref[...] = src_ref[...].astype(dst_ref.dtype)
    z = _runtime_zero()
    lam = (jnp.exp(jnp.sum(lq1_ref[...] * lk1_ref[...], axis=-1, keepdims=True))
           - jnp.exp(jnp.sum(lq2_ref[...] * lk2_ref[...], axis=-1, keepdims=True)) + lam_init)
    dn = (((1,), (1,)), ((), ()))
    tq = q_ref.shape[1]
    n_keys = k_ref.shape[1]
    chunks = [(k_ref, c0, min(kchunk, n_keys - c0), c0) for c0 in range(0, n_keys, kchunk)]
    if two_sources:
        chunks.append((kc_ref, 0, kc_ref.shape[1], n_keys))

    @pl.when(pl.program_id(2) == 0)
    def _():
        srcs = [(v_ref, 0)] + ([(vc_ref, n_keys)] if two_sources else [])
        for vr, off in srcs:
            n = vr.shape[1]
            vx_ref[off:off + n, 0:DIFF_DV] = vr[0]
            vx_ref[off:off + n, DIFF_DV:2 * DIFF_DV] = (
                lax.broadcasted_iota(jnp.int32, (n, DIFF_DV), 1) == 0).astype(BF16)

    nblk = tq // sub
    st = [dict() for _ in range(2 * nblk)]

    def qk(u, j):
        x, c = divmod(u, 2)
        d = st[u]
        if j == 0:
            q = q_ref[0, x * sub:(x + 1) * sub, :]
            lane = lax.broadcasted_iota(jnp.int32, q.shape, 1)
            keep = (lane < DIFF_DH) if c == 0 else (lane >= DIFF_DH)
            d["q"] = jnp.where(keep, q, jnp.zeros_like(q))
        kr, c0, n, off = chunks[j]
        s = lax.dot_general(d["q"], kr[0, c0:c0 + n, :], dn, preferred_element_type=F32)
        ss_refs[u % 2][z, :, off:off + n] = s
        mj = jnp.max(s, axis=-1, keepdims=True)
        d["m"] = mj if j == 0 else jnp.maximum(d["m"], mj)

    def ev(u, j):
        x, c = divmod(u, 2)
        d = st[u]
        _, _, n, voff = chunks[j]
        p = jnp.exp2(ss_refs[u % 2][z, :, voff:voff + n] - d["m"]).astype(BF16)
        part = jnp.dot(p, vx_ref[voff:voff + n, :], preferred_element_type=F32)
        d["acc"] = part if j == 0 else d["acc"] + part
        if j == len(chunks) - 1:
            acc = d["acc"]
            on = acc[:, 0:DIFF_DV] * (1.0 / acc[:, DIFF_DV:DIFF_DV + 1])
            o1 = d.get("o1")
            d.clear()
            if c == 0:
                st[u + 1]["o1"] = on
            else:
                o = o1 - lam * on
                r = lax.rsqrt(jnp.mean(o * o, axis=-1, keepdims=True) + EPS)
                o_ref[0, x * sub:(x + 1) * sub, :] = (o * r * g_ref[...] * (1.0 - lam_init)).astype(o_ref.dtype)

    nch = len(chunks)
    for u in range(2 * nblk + 1):
        for j in range(nch):
            if u < 2 * nblk:
                qk(u, j)
            if u >= 1:
                ev(u - 1, j)


def _diff_attn(zq, zkv_extra, lams, subln_g, lam_init, *, tq, sub, kchunk, side_casts=()):
    bsz, t_len, _ = zq.shape
    nh = DIFF_HEADS
    two = zkv_extra is not None
    n_all = t_len + (zkv_extra.shape[1] if two else 0)
    n_q = t_len // tq
    steps = bsz * nh * n_q
    step_map = lambda b, h, i: ((b * nh + h) * n_q + i, 0)
    side_specs = []
    for a in side_casts:
        assert a.shape[0] % (steps * ROW_CHUNK) == 0
        side_specs.append(pl.BlockSpec((a.shape[0] // steps, a.shape[1]), step_map))
    in_specs = side_specs + [pl.BlockSpec((1, tq, LANES), lambda b, h, i: (b, i, h)),
                pl.BlockSpec((1, t_len, LANES), lambda b, h, i: (b, 0, nh + h)),
                pl.BlockSpec((1, t_len, LANES), lambda b, h, i: (b, 0, 2 * nh + h))]
    args = [*side_casts, zq, zq, zq]
    if two:
        c_len = zkv_extra.shape[1]
        in_specs += [pl.BlockSpec((1, c_len, LANES), lambda b, h, i: (b, 0, nh + h)),
                     pl.BlockSpec((1, c_len, LANES), lambda b, h, i: (b, 0, 2 * nh + h))]
        args += [zkv_extra, zkv_extra]
    for v in lams:
        in_specs.append(pl.BlockSpec((1, DIFF_DH), lambda b, h, i: (0, 0)))
        args.append(v.reshape(1, DIFF_DH))
    in_specs.append(pl.BlockSpec((1, DIFF_DV), lambda b, h, i: (0, 0)))
    args.append(subln_g.reshape(1, DIFF_DV))
    out_specs = [pl.BlockSpec((1, tq, LANES), lambda b, h, i: (b, i, h))] + side_specs
    out_shape = [jax.ShapeDtypeStruct((bsz, t_len, nh * DIFF_DV), BF16)]
    out_shape += [jax.ShapeDtypeStruct(a.shape, BF16) for a in side_casts]
    out = pl.pallas_call(
        functools.partial(_attn_kernel, lam_init=lam_init, two_sources=two, sub=sub, kchunk=kchunk,
                          n_side=len(side_casts)),
        grid=(bsz, nh, n_q),
        in_specs=in_specs,
        out_specs=out_specs,
        out_shape=out_shape,
        scratch_shapes=[pltpu.VMEM((n_all, 2 * DIFF_DV), BF16), pltpu.VMEM((1, sub, n_all), F32),
                        pltpu.VMEM((1, sub, n_all), F32)],
        compiler_params=_params(("parallel", "parallel", "arbitrary")),
        name="diff_attn" + ("_lat" if two else "_ctx"),
    )(*args)
    return out if side_casts else out[0]


def _dwconv_phases(width):
    offs = [HALO - width // 2 + w for w in range(width)]
    return sorted({o % SUBLANES for o in offs} - {0}), max(offs) // SUBLANES * SUBLANES


def _dwconv_scratch(tt, width, nch):
    phases, reach = _dwconv_phases(width)
    return pltpu.VMEM((max(len(phases), 1), tt + reach, nch), F32)


def _dwconv_into(y_ref, ysh_ref, w_ref, b_ref, out_ref, *, tt, width, nch):
    phases, reach = _dwconv_phases(width)
    for idx, p in enumerate(phases):
        ysh_ref[idx] = y_ref[p:p + tt + reach, :]
    rc = 64
    for c0 in range(0, nch, LANES):
        for r0 in range(0, tt, rc):
            acc = jnp.broadcast_to(b_ref[:, c0:c0 + LANES], (rc, LANES))
            for w in range(width):
                off = HALO - width // 2 + w
                p, start = off % SUBLANES, r0 + off // SUBLANES * SUBLANES
                if p == 0:
                    tap = y_ref[start:start + rc, c0:c0 + LANES]
                else:
                    tap = ysh_ref[phases.index(p), start:start + rc, c0:c0 + LANES]
                acc = acc + tap * w_ref[w:w + 1, c0:c0 + LANES]
            out_ref[r0:r0 + rc, c0:c0 + LANES] = acc


def _convmod_kernel(ap_ref, a_ref, an_ref, gp_ref, g_ref, gn_ref, cw_ref, cb_ref, lng_ref, lnb_ref, o_ref,
                    y_ref, c_ref, ysh_ref, *, tt, nch):
    i = pl.program_id(1)
    n = pl.num_programs(1)

    def glu(a, g):
        return a.astype(F32) * _sigmoid(g.astype(F32))

    y_ref[0:HALO] = jnp.where(i > 0, glu(ap_ref[0], gp_ref[0]), 0.0)
    y_ref[HALO:HALO + tt] = glu(a_ref[0], g_ref[0])
    y_ref[HALO + tt:2 * HALO + tt] = jnp.where(i < n - 1, glu(an_ref[0], gn_ref[0]), 0.0)
    _dwconv_into(y_ref, ysh_ref, cw_ref, cb_ref, c_ref, tt=tt, width=CONV_W, nch=nch)
    c = c_ref[...]
    mu = jnp.mean(c, axis=-1, keepdims=True)
    xc = c - mu
    var = jnp.mean(xc * xc, axis=-1, keepdims=True)
    y = xc * lax.rsqrt(var + EPS) * lng_ref[...] + lnb_ref[...]
    o_ref[0] = (y * _sigmoid(y)).astype(o_ref.dtype)


def _halo_specs(tt, t_len, width, colblk):
    per = tt // HALO
    last = t_len // HALO - 1
    return (pl.BlockSpec((1, HALO, width), lambda b, i: (b, jnp.maximum(i * per - 1, 0), colblk)),
            pl.BlockSpec((1, tt, width), lambda b, i: (b, i, colblk)),
            pl.BlockSpec((1, HALO, width), lambda b, i: (b, jnp.minimum((i + 1) * per, last), colblk)))


def _conv_module(z, conv_w, conv_b, ln_g, ln_b, *, tt):
    bsz, t_len, n = z.shape
    nch = conv_w.shape[1]
    a_blk = (n - 2 * nch) // nch
    vec = lambda: pl.BlockSpec((1, nch), lambda b, i: (0, 0))
    return pl.pallas_call(
        functools.partial(_convmod_kernel, tt=tt, nch=nch),
        grid=(bsz, t_len // tt),
        in_specs=[*_halo_specs(tt, t_len, nch, a_blk), *_halo_specs(tt, t_len, nch, a_blk + 1),
                  pl.BlockSpec((CONV_W, nch), lambda b, i: (0, 0)), vec(), vec(), vec()],
        out_specs=pl.BlockSpec((1, tt, nch), lambda b, i: (b, i, 0)),
        out_shape=jax.ShapeDtypeStruct((bsz, t_len, nch), BF16),
        scratch_shapes=[pltpu.VMEM((tt + 2 * HALO, nch), F32), pltpu.VMEM((tt, nch), F32),
                        _dwconv_scratch(tt, CONV_W, nch)],
        compiler_params=_params(("parallel", "arbitrary")),
        name="conformer_conv",
    )(z, z, z, z, z, z, conv_w, conv_b.reshape(1, nch), ln_g.reshape(1, nch), ln_b.reshape(1, nch))


def _conv3_kernel(zp_ref, z_ref, zn_ref, cw_ref, cb_ref, o_ref, y_ref, c_ref, ysh_ref, *, tt, nch, kscale):
    i = pl.program_id(1)
    n = pl.num_programs(1)
    y_ref[0:HALO] = jnp.where(i > 0, zp_ref[0].astype(F32), 0.0)
    y_ref[HALO:HALO + tt] = z_ref[0].astype(F32)
    y_ref[HALO + tt:2 * HALO + tt] = jnp.where(i < n - 1, zn_ref[0].astype(F32), 0.0)
    _dwconv_into(y_ref, ysh_ref, cw_ref, cb_ref, c_ref, tt=tt, width=ML_CONV_W, nch=nch)
    c = c_ref[...]
    s = c * _sigmoid(c)
    half = nch // 2
    o_ref[0, :, 0:half] = s[:, 0:half].astype(o_ref.dtype)
    o_ref[0, :, half:nch] = (s[:, half:nch] * kscale).astype(o_ref.dtype)


def _conv3_silu(z, conv_w, conv_b, *, tt):
    bsz, t_len, _ = z.shape
    nch = conv_w.shape[1]
    return pl.pallas_call(
        functools.partial(_conv3_kernel, tt=tt, nch=nch, kscale=ML_DK ** -0.5),
        grid=(bsz, t_len // tt),
        in_specs=[*_halo_specs(tt, t_len, nch, 0),
                  pl.BlockSpec((ML_CONV_W, nch), lambda b, i: (0, 0)),
                  pl.BlockSpec((1, nch), lambda b, i: (0, 0))],
        out_specs=pl.BlockSpec((1, tt, nch), lambda b, i: (b, i, 0)),
        out_shape=jax.ShapeDtypeStruct((bsz, t_len, nch), BF16),
        scratch_shapes=[pltpu.VMEM((tt + 2 * HALO, nch), F32), pltpu.VMEM((tt, nch), F32),
                        _dwconv_scratch(tt, ML_CONV_W, nch)],
        compiler_params=_params(("parallel", "arbitrary")),
        name="ml_conv3_silu",
    )(z, z, z, conv_w, conv_b.reshape(1, nch))


def _post_residual(y, x, gate, gpost):
    r = lax.rsqrt(jnp.mean(y * y, axis=-1, keepdims=True) + EPS)
    return x + gate * (y * r * gpost)


OUT_SUB = 256


def _hy_out_kernel(a_ref, c_ref, x_ref, gate_ref, gpost_ref, w_ref, o_ref):
    ka = a_ref.shape[1]
    y = (jnp.dot(a_ref[...], w_ref[0:ka, :], preferred_element_type=F32)
         + jnp.dot(c_ref[...], w_ref[ka:, :], preferred_element_type=F32))
    o_ref[...] = _post_residual(y, x_ref[...], gate_ref[0], gpost_ref[...])


def _hy_out(attn, conv, x, gate, gpost, w, *, tm, rows_per_mod):
    m, d = x.shape
    ka, kc = attn.shape[1], conv.shape[1]
    return pl.pallas_call(
        _hy_out_kernel,
        grid=(m // tm,),
        in_specs=[pl.BlockSpec((tm, ka), lambda i: (i, 0)),
                  pl.BlockSpec((tm, kc), lambda i: (i, 0)),
                  pl.BlockSpec((tm, d), lambda i: (i, 0)),
                  pl.BlockSpec((1, 1, d), lambda i: ((i * tm) // rows_per_mod, 0, 0)),
                  pl.BlockSpec((1, d), lambda i: (0, 0)),
                  pl.BlockSpec((ka + kc, d), lambda i: (0, 0), pipeline_mode=pl.Buffered(1))],
        out_specs=pl.BlockSpec((tm, d), lambda i: (i, 0)),
        out_shape=jax.ShapeDtypeStruct((m, d), F32),
        compiler_params=_params(("parallel",)),
        name="hybrid_out_proj",
    )(attn, conv, x, gate, gpost.reshape(1, d), w)


def _ml_out_kernel(hf_ref, hb_ref, og_ref, x_ref, ng_ref, gate_ref, gpost_ref, w_ref, o_ref):
    def gated(rows):
        hsum = hf_ref[0, rows, :].astype(F32) + hb_ref[0, rows, :].astype(F32)
        parts = []
        for h in range(ML_HEADS):
            hh = hsum[:, h * ML_DV:(h + 1) * ML_DV]
            r = lax.rsqrt(jnp.mean(hh * hh, axis=-1, keepdims=True) + EPS)
            parts.append(hh * r * ng_ref[:, h * ML_DV:(h + 1) * ML_DV])
        hn = jnp.concatenate(parts, axis=1)
        return (hn * _sigmoid(og_ref[rows, :].astype(F32))).astype(BF16)

    subs = [slice(r0, r0 + OUT_SUB) for r0 in range(0, x_ref.shape[0], OUT_SUB)]
    a = gated(subs[0])
    y_prev = None
    for idx, rows in enumerate(subs):
        y = jnp.dot(a, w_ref[...], preferred_element_type=F32)
        if idx + 1 < len(subs):
            a = gated(subs[idx + 1])
        if y_prev is not None:
            prev = subs[idx - 1]
            o_ref[prev, :] = _post_residual(y_prev, x_ref[prev, :], gate_ref[0], gpost_ref[...])
        y_prev = y
    o_ref[subs[-1], :] = _post_residual(y_prev, x_ref[subs[-1], :], gate_ref[0], gpost_ref[...])


def _ml_out(h2, z, o_blk, x, norm_g, gate, gpost, w, *, tm, rows_per_mod):
    m, d = x.shape
    kv = h2.shape[2]
    return pl.pallas_call(
        _ml_out_kernel,
        grid=(m // tm,),
        in_specs=[pl.BlockSpec((1, tm, kv), lambda i: (0, i, 0)),
                  pl.BlockSpec((1, tm, kv), lambda i: (1, i, 0)),
                  pl.BlockSpec((tm, kv), lambda i: (i, o_blk)),
                  pl.BlockSpec((tm, d), lambda i: (i, 0)),
                  pl.BlockSpec((1, kv), lambda i: (0, 0)),
                  pl.BlockSpec((1, 1, d), lambda i: ((i * tm) // rows_per_mod, 0, 0)),
                  pl.BlockSpec((1, d), lambda i: (0, 0)),
                  pl.BlockSpec((kv, d), lambda i: (0, 0), pipeline_mode=pl.Buffered(1))],
        out_specs=pl.BlockSpec((tm, d), lambda i: (i, 0)),
        out_shape=jax.ShapeDtypeStruct((m, d), F32),
        compiler_params=_params(("parallel",)),
        name="mlstm_out_proj",
    )(h2, h2, z, x, norm_g.reshape(1, kv), gate, gpost.reshape(1, d), w)


def _ffn_kernel(x_ref, g_ref, sh_ref, sc_ref, gate_ref, gpost_ref, w1_ref, w2_ref, o_ref, h_ref, acc_ref):
    f = pl.program_id(1)
    nf = pl.num_programs(1)

    @pl.when(f == 0)
    def _():
        _prenorm_into(h_ref, x_ref, g_ref[...], sh_ref[0], sc_ref[0])
        acc_ref[...] = jnp.zeros_like(acc_ref)

    t = jnp.maximum(jnp.dot(h_ref[...], w1_ref[0], preferred_element_type=F32), 0.0)
    acc_ref[...] += jnp.dot((t * t).astype(BF16), w2_ref[0], preferred_element_type=F32)

    @pl.when(f == nf - 1)
    def _():
        for r0 in range(0, x_ref.shape[0], ROW_CHUNK):
            rows = slice(r0, r0 + ROW_CHUNK)
            o_ref[rows, :] = _post_residual(acc_ref[rows, :], x_ref[rows, :], gate_ref[0], gpost_ref[...])


def _ffn(x, g, shift, scale, gate, gpost, w1, w2, layer, *, tm, tf, rows_per_mod):
    m, d = x.shape
    dff = w1.shape[2]
    mod_map = lambda i, f: ((i * tm) // rows_per_mod, 0, 0)
    vec = lambda: pl.BlockSpec((1, d), lambda i, f: (0, 0))
    return pl.pallas_call(
        _ffn_kernel,
        grid=(m // tm, dff // tf),
        in_specs=[pl.BlockSpec((tm, d), lambda i, f: (i, 0)), vec(),
                  pl.BlockSpec((1, 1, d), mod_map), pl.BlockSpec((1, 1, d), mod_map),
                  pl.BlockSpec((1, 1, d), mod_map), vec(),
                  pl.BlockSpec((1, d, tf), lambda i, f: (layer, 0, f)),
                  pl.BlockSpec((1, tf, d), lambda i, f: (layer, f, 0))],
        out_specs=pl.BlockSpec((tm, d), lambda i, f: (i, 0)),
        out_shape=jax.ShapeDtypeStruct((m, d), F32),
        scratch_shapes=[pltpu.VMEM((tm, d), BF16), pltpu.VMEM((tm, d), F32)],
        compiler_params=_params(("parallel", "arbitrary")),
        name="ffn",
    )(x, g.reshape(1, d), shift, scale, gate, gpost.reshape(1, d), w1, w2)


def _scan_kernel(ql_ref, kl_ref, vl_ref, gil_ref, gfl_ref, qc_ref, kc_ref, vc_ref, gic_ref, gfc_ref,
                 bi_ref, bf_ref, o_ref, c_ref, m_ref):
    L, H, DK, DV = SCAN_CHUNK, ML_HEADS, ML_DK, ML_DV
    fwd = pl.program_id(0) == 0
    s = pl.program_id(2)

    @pl.when(s == 0)
    def _():
        c_ref[...] = jnp.zeros_like(c_ref)
        m_ref[...] = jnp.zeros_like(m_ref)

    is_ctx = s == 0
    q = jnp.where(is_ctx, qc_ref[0], ql_ref[0])
    k = jnp.where(is_ctx, kc_ref[0], kl_ref[0])
    v = jnp.where(is_ctx, vc_ref[0], vl_ref[0])
    ipre = jnp.where(is_ctx, gic_ref[0, 0], gil_ref[0, 0]) + bi_ref[0]
    fpre = jnp.where(is_ctx, gfc_ref[0, 0], gfl_ref[0, 0]) + bf_ref[0]
    lf = jnp.minimum(fpre, 0.0) - jnp.log1p(jnp.exp(-jnp.abs(fpre)))

    row = lax.broadcasted_iota(jnp.int32, (L, L), 0)
    col = lax.broadcasted_iota(jnp.int32, (L, L), 1)
    delta = (row - col) * jnp.where(fwd, 1, -1)
    tri = jnp.where(delta <= 0, 1.0, 0.0).astype(F32)
    b_all = jnp.dot(lf, tri, precision=lax.Precision.HIGHEST, preferred_element_type=F32)
    r_all = ipre - b_all
    b_last_all = jnp.where(fwd, b_all[:, L - 1:L], b_all[:, 0:1])
    mask = delta >= 0
    ones_blk = (lax.broadcasted_iota(jnp.int32, (L, LANES), 1) == 0).astype(BF16)
    dn_t = (((1,), (1,)), ((), ()))

    for h in range(H):
        b_row = b_all[h:h + 1, :]
        r_row = r_all[h:h + 1, :]
        bcol = jnp.transpose(jnp.broadcast_to(b_row, (LANES, L)))
        bcol_l = jnp.concatenate([bcol] * (L // LANES), axis=1)
        dm = jnp.where(mask, bcol_l + r_row, -jnp.inf)
        m_loc = jnp.max(dm, axis=1, keepdims=True)
        m_prev = m_ref[h:h + 1, 0:1]
        g = bcol[:, 0:1] + m_prev
        m_row = jnp.maximum(g, m_loc)
        inter = jnp.exp(g - m_row)
        p = jnp.exp(dm - m_row)
        qh = q[:, h * DK:(h + 1) * DK]
        kh = k[:, h * DK:(h + 1) * DK]
        vext = jnp.concatenate([v[:, h * DV:(h + 1) * DV], ones_blk], axis=1)
        sm = (lax.dot_general(qh, kh, dn_t, preferred_element_type=F32) * p).astype(BF16)
        cst = c_ref[h]
        nd = (inter * jnp.dot(qh, cst.astype(BF16), preferred_element_type=F32)
              + jnp.dot(sm, vext, preferred_element_type=F32))
        den = jnp.maximum(jnp.abs(nd[:, DV:DV + 1]), jnp.exp(-m_row))
        o_ref[0, 0, :, h * DV:(h + 1) * DV] = (nd[:, 0:DV] * (1.0 / den)).astype(o_ref.dtype)

        b_last = b_last_all[h:h + 1, :]
        wlog = b_last + r_row
        m_new = jnp.maximum(b_last + m_prev, jnp.max(wlog, axis=1, keepdims=True))
        decay = jnp.exp(b_last + m_prev - m_new)
        kt = jnp.transpose(kh.astype(F32))
        ktw = (kt * jnp.exp(wlog - m_new)).astype(BF16)
        c_ref[h] = decay * cst + jnp.dot(ktw, vext, preferred_element_type=F32)
        m_ref[h:h + 1, :] = jnp.broadcast_to(m_new, (1, LANES))


def _mlstm_scan(qk_l, z_l, gates_l, qk_c, z_c, gates_c, gate_b):
    L, H = SCAN_CHUNK, ML_HEADS
    bsz, t_len, _ = qk_l.shape
    assert qk_c.shape[1] == L
    n_lat = t_len // L
    hk, hv = H * ML_DK, H * ML_DV

    def lat(d, s):
        return jnp.where(d == 0, jnp.maximum(s - 1, 0), n_lat - jnp.maximum(s, 1))

    in_specs = [
        pl.BlockSpec((1, L, hk), lambda d, b, s: (b, lat(d, s), 0)),
        pl.BlockSpec((1, L, hk), lambda d, b, s: (b, lat(d, s), 1)),
        pl.BlockSpec((1, L, hv), lambda d, b, s: (b, lat(d, s), 1)),
        pl.BlockSpec((1, 1, H, L), lambda d, b, s: (b, 2 * d, 0, lat(d, s))),
        pl.BlockSpec((1, 1, H, L), lambda d, b, s: (b, 2 * d + 1, 0, lat(d, s))),
        pl.BlockSpec((1, L, hk), lambda d, b, s: (b, 0, 0)),
        pl.BlockSpec((1, L, hk), lambda d, b, s: (b, 0, 1)),
        pl.BlockSpec((1, L, hv), lambda d, b, s: (b, 0, 1)),
        pl.BlockSpec((1, 1, H, L), lambda d, b, s: (b, 2 * d, 0, 0)),
        pl.BlockSpec((1, 1, H, L), lambda d, b, s: (b, 2 * d + 1, 0, 0)),
        pl.BlockSpec((1, H, 1), lambda d, b, s: (2 * d, 0, 0)),
        pl.BlockSpec((1, H, 1), lambda d, b, s: (2 * d + 1, 0, 0)),
    ]
    return pl.pallas_call(
        _scan_kernel,
        grid=(2, bsz, n_lat + 1),
        in_specs=in_specs,
        out_specs=pl.BlockSpec((1, 1, L, hv), lambda d, b, s: (d, b, lat(d, s), 0)),
        out_shape=jax.ShapeDtypeStruct((2, bsz, t_len, hv), BF16),
        scratch_shapes=[pltpu.VMEM((H, ML_DK, ML_DV + LANES), F32), pltpu.VMEM((H, LANES), F32)],
        compiler_params=_params(("parallel", "parallel", "arbitrary")),
        name="mlstm_scan",
    )(qk_l, qk_l, z_l, gates_l, gates_l, qk_c, qk_c, z_c, gates_c, gates_c,
      gate_b.reshape(4, H, 1), gate_b.reshape(4, H, 1))


def kernel(x, c, ctx, c_ctx, ada_w, ada_b, g_pre_mix, g_post_mix, g_pre_ffn, g_post_ffn, ffn_w1, ffn_w2, hy_w_in, hy_w_out, diff_lq1, diff_lk1, diff_lq2, diff_lk2, diff_subln_g, conv_w, conv_b, conv_ln_g, conv_ln_b, ml_w_in, ml_conv_w, ml_conv_b, ml_gate_b, ml_norm_g, ml_w_out):
    bsz, t_len, d = x.shape
    c_len = ctx.shape[1]
    depth = ada_w.shape[0]
    assert depth == 2 and bsz <= 7 and c_len == SCAN_CHUNK
    ml = t_len * bsz
    mc = c_len * bsz

    cvec = jnp.concatenate([c, c_ctx[None, :], jnp.zeros((8 - bsz - 1, d), F32)], axis=0)
    mods = _ada(cvec, ada_w, ada_b)

    def mod_lat(l, k):
        return mods[l, :bsz, k * d:(k + 1) * d].reshape(bsz, 1, d)

    def mod_ctx(l, k):
        return mods[l, bsz:bsz + 1, k * d:(k + 1) * d].reshape(1, 1, d)

    xl = x.reshape(ml, d)
    xc = ctx.reshape(mc, d)

    l = 0
    lam_init = 0.8 - 0.6 * math.exp(-0.3 * l)
    w_in = hy_w_in[0].astype(BF16)
    n_in = w_in.shape[1]
    qk_w = 2 * DIFF_HEADS * 2 * DIFF_DH
    rope = _rope_tables(t_len)
    zl = _prenorm_mm(xl, g_pre_mix[l], mod_lat(l, 0), mod_lat(l, 1), w_in, n_in, **TILES["hy_in"],
                     rows_per_mod=t_len, out_dtype=BF16, rope=rope, rope_cols=qk_w, qscale_cols=qk_w // 2,
                     qscale=ATTN_QSCALE, name="hy_in_proj_lat")
    zc = _prenorm_mm(xc, g_pre_mix[l], mod_ctx(l, 0), mod_ctx(l, 1), w_in, n_in, tm=mc, tn=TILES["ctx_in_tn"],
                     rows_per_mod=mc, out_dtype=BF16, qscale_cols=qk_w // 2, qscale=ATTN_QSCALE,
                     name="hy_in_proj_ctx")
    zl3 = zl.reshape(bsz, t_len, n_in)
    zc3 = zc.reshape(bsz, c_len, n_in)
    lams = (diff_lq1[0], diff_lk1[0], diff_lq2[0], diff_lk2[0])
    dff = ffn_w1.shape[2]
    attn_l, w1, w2 = _diff_attn(zl3, zc3, lams, diff_subln_g[0], lam_init, **TILES["attn"],
                                side_casts=(ffn_w1.reshape(depth * d, dff), ffn_w2.reshape(depth * dff, d)))
    w1 = w1.reshape(depth, d, dff)
    w2 = w2.reshape(depth, dff, d)
    attn_c = _diff_attn(zc3, None, lams, diff_subln_g[0], lam_init, tq=c_len, sub=c_len, kchunk=c_len)
    conv_l = _conv_module(zl3, conv_w[0], conv_b[0], conv_ln_g[0], conv_ln_b[0], tt=TILES["conv_tt"])
    conv_c = _conv_module(zc3, conv_w[0], conv_b[0], conv_ln_g[0], conv_ln_b[0], tt=c_len)
    w_out = hy_w_out[0].astype(BF16)
    xl = _hy_out(attn_l.reshape(ml, -1), conv_l.reshape(ml, -1), xl, mod_lat(l, 2), g_post_mix[l], w_out,
                 tm=TILES["out_tm"], rows_per_mod=t_len)
    xc = _hy_out(attn_c.reshape(mc, -1), conv_c.reshape(mc, -1), xc, mod_ctx(l, 2), g_post_mix[l], w_out,
                 tm=TILES["out_tm"], rows_per_mod=mc)
    xl = _ffn(xl, g_pre_ffn[l], mod_lat(l, 3), mod_lat(l, 4), mod_lat(l, 5), g_post_ffn[l], w1, w2, l,
              **TILES["ffn"], rows_per_mod=t_len)
    xc = _ffn(xc, g_pre_ffn[l], mod_ctx(l, 3), mod_ctx(l, 4), mod_ctx(l, 5), g_post_ffn[l], w1, w2, l,
              **TILES["ffn"], rows_per_mod=mc)

    l = 1
    qkv_w = 2 * ML_HEADS * ML_DK + ML_HEADS * ML_DV
    n_gates = 4 * ML_HEADS
    wm = ml_w_in[0].astype(BF16)
    w_main = jnp.concatenate([wm[:, :qkv_w], wm[:, qkv_w + n_gates:]], axis=1)
    w_gate = jnp.pad(wm[:, qkv_w:qkv_w + n_gates], ((0, 0), (0, LANES - n_gates)))
    n_main = w_main.shape[1]
    zl, gl = _prenorm_mm(xl, g_pre_mix[l], mod_lat(l, 0), mod_lat(l, 1), w_main, n_main, **TILES["ml_in"],
                         rows_per_mod=t_len, out_dtype=BF16, w_side=w_gate, name="ml_in_proj_lat")
    zc, gc = _prenorm_mm(xc, g_pre_mix[l], mod_ctx(l, 0), mod_ctx(l, 1), w_main, qkv_w, tm=mc, tn=TILES["ctx_ml_tn"],
                         rows_per_mod=mc, out_dtype=BF16, w_side=w_gate, name="ml_in_proj_ctx")
    zl3 = zl.reshape(bsz, t_len, n_main)
    zc3 = zc.reshape(bsz, c_len, qkv_w)
    qk_l = _conv3_silu(zl3, ml_conv_w[0], ml_conv_b[0], tt=TILES["conv_tt"])
    qk_c = _conv3_silu(zc3, ml_conv_w[0], ml_conv_b[0], tt=c_len)

    def gates_t(gm, n):
        return gm[:, :n_gates].reshape(bsz, n, 4, ML_HEADS).transpose(0, 2, 3, 1)

    h2 = _mlstm_scan(qk_l, zl3, gates_t(gl, t_len), qk_c, zc3, gates_t(gc, c_len), ml_gate_b[0])
    xl = _ml_out(h2.reshape(2, ml, -1), zl, qkv_w // (ML_HEADS * ML_DV), xl, ml_norm_g[0], mod_lat(l, 2),
                 g_post_mix[l], ml_w_out[0].astype(BF16), tm=TILES["out_tm"], rows_per_mod=t_len)
    xl = _ffn(xl, g_pre_ffn[l], mod_lat(l, 3), mod_lat(l, 4), mod_lat(l, 5), g_post_ffn[l], w1, w2, l,
              **TILES["ffn"], rows_per_mod=t_len)
    return xl.reshape(bsz, t_len, d)
```

```python
import functools
import math

import jax
import jax.numpy as jnp
from jax import lax
from jax.experimental import pallas as pl
from jax.experimental.pallas import tpu as pltpu

F32 = jnp.float32
BF16 = jnp.bfloat16

EPS = 1e-6
ROPE_BASE = 10000.0
GRID_W = 64
LANES = 128
SUBLANES = 8
HALO = 16
DIFF_DH = 64
DIFF_HEADS = 8
DIFF_DV = 128
CONV_W = 31
ML_HEADS = 8
ML_DK = 128
ML_DV = 256
ML_CONV_W = 3
SCAN_CHUNK = 256
ATTN_QSCALE = DIFF_DH ** -0.5 * math.log2(math.e)
VMEM_LIMIT = 56 * 1024 * 1024
TILES = dict(
    hy_in=dict(tm=512, tn=2560), ml_in=dict(tm=1024, tn=2048), ctx_in_tn=1024, ctx_ml_tn=2048,
    attn=dict(tq=2048, sub=256, kchunk=2048), conv_tt=512, out_tm=512, ffn=dict(tm=512, tf=1024),
)


def _params(sem):
    return pltpu.CompilerParams(dimension_semantics=sem, vmem_limit_bytes=VMEM_LIMIT)


def _sigmoid(x):
    return 1.0 / (1.0 + jnp.exp(-x))


def _runtime_zero():
    return jnp.minimum(pl.program_id(0), 0)


def _ada_kernel(c_ref, w_ref, b_ref, o_ref):
    c = c_ref[...]
    s = (c * _sigmoid(c)).astype(BF16)
    o_ref[0] = jnp.dot(s, w_ref[0].astype(BF16), preferred_element_type=F32) + b_ref[0]


def _ada(cvec, ada_w, ada_b):
    depth, d, n = ada_w.shape
    tn = 1024
    return pl.pallas_call(
        _ada_kernel,
        grid=(depth, n // tn),
        in_specs=[pl.BlockSpec((8, d), lambda l, j: (0, 0)),
                  pl.BlockSpec((1, d, tn), lambda l, j: (l, 0, j)),
                  pl.BlockSpec((1, 1, tn), lambda l, j: (l, 0, j))],
        out_specs=pl.BlockSpec((1, 8, tn), lambda l, j: (l, 0, j)),
        out_shape=jax.ShapeDtypeStruct((depth, 8, n), F32),
        compiler_params=_params(("parallel", "parallel")),
        name="ada_ln",
    )(cvec, ada_w, ada_b.reshape(depth, 1, n))


ROW_CHUNK = 16


def _prenorm_into(h_ref, x_ref, g, shift, scale):
    gs = g * (1.0 + scale)
    for r0 in range(0, x_ref.shape[0], ROW_CHUNK):
        x = x_ref[r0:r0 + ROW_CHUNK, :]
        r = lax.rsqrt(jnp.mean(x * x, axis=-1, keepdims=True) + EPS)
        h_ref[r0:r0 + ROW_CHUNK, :] = (x * r * gs + shift).astype(h_ref.dtype)


def _prenorm_mm_kernel(*refs, rope_cols, tn, qscale_cols, qscale, has_side):
    x_ref, g_ref, sh_ref, sc_ref, w_ref = refs[:5]
    rest = list(refs[5:])
    if rope_cols:
        cos_ref, sina_ref, sinb_ref = rest[:3]
        rest = rest[3:]
    if has_side:
        ws_ref, o_ref, os_ref, h_ref = rest
    else:
        o_ref, h_ref = rest
    j = pl.program_id(1)

    @pl.when(j == 0)
    def _():
        _prenorm_into(h_ref, x_ref, g_ref[...], sh_ref[0], sc_ref[0])
        if has_side:
            os_ref[...] = jnp.dot(h_ref[...], ws_ref[...], preferred_element_type=F32)

    acc = jnp.dot(h_ref[...], w_ref[...], preferred_element_type=F32)
    if not (rope_cols or qscale_cols):
        o_ref[...] = acc.astype(o_ref.dtype)
        return
    for c in range(tn // LANES):
        col0 = j * tn + c * LANES
        xs = acc[:, c * LANES:(c + 1) * LANES]
        if qscale_cols:
            xs = xs * jnp.where(col0 < qscale_cols, qscale, 1.0)
        if rope_cols:
            sel = jnp.where(col0 < rope_cols, 0, 1)
            rot = pltpu.roll(xs, LANES - 16, 1) * sina_ref[sel] + pltpu.roll(xs, 16, 1) * sinb_ref[sel]
            xs = xs * cos_ref[sel] + rot
        o_ref[:, c * LANES:(c + 1) * LANES] = xs.astype(o_ref.dtype)


def _prenorm_mm(x, g, shift, scale, w, n_out, *, tm, tn, rows_per_mod, out_dtype, rope=None, rope_cols=0, qscale_cols=0,
                qscale=1.0, w_side=None, name):
    m, d = x.shape
    mod_map = lambda i, j: ((i * tm) // rows_per_mod, 0, 0)
    in_specs = [pl.BlockSpec((tm, d), lambda i, j: (i, 0)),
                pl.BlockSpec((1, d), lambda i, j: (0, 0)),
                pl.BlockSpec((1, 1, d), mod_map),
                pl.BlockSpec((1, 1, d), mod_map),
                pl.BlockSpec((d, tn), lambda i, j: (0, j))]
    args = [x, g.reshape(1, d), shift, scale, w]
    if rope is not None:
        assert rope_cols % LANES == 0 and qscale_cols % LANES == 0
        t_len = rope[0].shape[0]
        assert t_len % tm == 0
        nblk = t_len // tm
        for tab, ident in zip(rope, (1.0, 0.0, 0.0)):
            in_specs.append(pl.BlockSpec((2, tm, LANES), lambda i, j: (0, i % nblk, 0)))
            args.append(jnp.stack([tab, jnp.full_like(tab, ident)]))
    else:
        rope_cols = 0
    out_specs = pl.BlockSpec((tm, tn), lambda i, j: (i, j))
    out_shape = jax.ShapeDtypeStruct((m, n_out), out_dtype)
    if w_side is not None:
        ns = w_side.shape[1]
        in_specs.append(pl.BlockSpec((d, ns), lambda i, j: (0, 0)))
        args.append(w_side)
        out_specs = (out_specs, pl.BlockSpec((tm, ns), lambda i, j: (i, 0)))
        out_shape = (out_shape, jax.ShapeDtypeStruct((m, ns), F32))
    return pl.pallas_call(
        functools.partial(_prenorm_mm_kernel, rope_cols=rope_cols, tn=tn, qscale_cols=qscale_cols, qscale=qscale,
                          has_side=w_side is not None),
        grid=(m // tm, n_out // tn),
        in_specs=in_specs,
        out_specs=out_specs,
        out_shape=out_shape,
        scratch_shapes=[pltpu.VMEM((tm, d), BF16)],
        compiler_params=_params(("parallel", "arbitrary")),
        name=name,
    )(*args)


def _rope_tables(t_len):
    rows = t_len // GRID_W
    r = jnp.repeat(jnp.arange(rows, dtype=F32), GRID_W)
    col = jnp.tile(jnp.arange(GRID_W, dtype=F32), rows)
    n_freq = DIFF_DH // 4
    inv = ROPE_BASE ** (-jnp.arange(n_freq, dtype=F32) / n_freq)
    ar = r[:, None] * inv
    ac = col[:, None] * inv
    ang = jnp.concatenate([ar, ar, ac, ac], axis=-1)
    cos = jnp.tile(jnp.cos(ang), (1, LANES // DIFF_DH))
    sin = jnp.tile(jnp.sin(ang), (1, LANES // DIFF_DH))
    first = (jnp.arange(LANES) % (2 * n_freq)) < n_freq
    return cos, jnp.where(first, -sin, 0.0), jnp.where(first, 0.0, sin)


def _attn_kernel(*refs, lam_init, two_sources, sub, kchunk, n_side):
    side_in, refs = refs[:n_side], refs[n_side:]
    if two_sources:
        q_ref, k_ref, v_ref, kc_ref, vc_ref, lq1_ref, lk1_ref, lq2_ref, lk2_ref, g_ref, o_ref, *rest = refs
    else:
        q_ref, k_ref, v_ref, lq1_ref, lk1_ref, lq2_ref, lk2_ref, g_ref, o_ref, *rest = refs
    side_out, (vx_ref, *ss_refs) = rest[:n_side], rest[n_side:]
    for src_ref, dst_ref in zip(side_in, side_out):
        dst_ref[...] = src_ref[...].astype(dst_ref.dtype)
    z = _runtime_zero()
    lam = (jnp.exp(jnp.sum(lq1_ref[...] * lk1_ref[...], axis=-1, keepdims=True))
           - jnp.exp(jnp.sum(lq2_ref[...] * lk2_ref[...], axis=-1, keepdims=True)) + lam_init)
    dn = (((1,), (1,)), ((), ()))
    tq = q_ref.shape[1]
    n_keys = k_ref.shape[1]
    chunks = [(k_ref, c0, min(kchunk, n_keys - c0), c0) for c0 in range(0, n_keys, kchunk)]
    if two_sources:
        chunks.append((kc_ref, 0, kc_ref.shape[1], n_keys))

    @pl.when(pl.program_id(2) == 0)
    def _():
        srcs = [(v_ref, 0)] + ([(vc_ref, n_keys)] if two_sources else [])
        for vr, off in srcs:
            n = vr.shape[1]
            vx_ref[off:off + n, 0:DIFF_DV] = vr[0]
            vx_ref[off:off + n, DIFF_DV:2 * DIFF_DV] = (
                lax.broadcasted_iota(jnp.int32, (n, DIFF_DV), 1) == 0).astype(BF16)

    nblk = tq // sub
    st = [dict() for _ in range(2 * nblk)]

    def qk(u, j):
        x, c = divmod(u, 2)
        d = st[u]
        if j == 0:
            q = q_ref[0, x * sub:(x + 1) * sub, :]
            lane = lax.broadcasted_iota(jnp.int32, q.shape, 1)
            keep = (lane < DIFF_DH) if c == 0 else (lane >= DIFF_DH)
            d["q"] = jnp.where(keep, q, jnp.zeros_like(q))
        kr, c0, n, off = chunks[j]
        s = lax.dot_general(d["q"], kr[0, c0:c0 + n, :], dn, preferred_element_type=F32)
        ss_refs[u % 2][z, :, off:off + n] = s
        mj = jnp.max(s, axis=-1, keepdims=True)
        d["m"] = mj if j == 0 else jnp.maximum(d["m"], mj)

    def ev(u, j):
        x, c = divmod(u, 2)
        d = st[u]
        _, _, n, voff = chunks[j]
        p = jnp.exp2(ss_refs[u % 2][z, :, voff:voff + n] - d["m"]).astype(BF16)
        part = jnp.dot(p, vx_ref[voff:voff + n, :], preferred_element_type=F32)
        d["acc"] = part if j == 0 else d["acc"] + part
        if j == len(chunks) - 1:
            acc = d["acc"]
            on = acc[:, 0:DIFF_DV] * (1.0 / acc[:, DIFF_DV:DIFF_DV + 1])
            o1 = d.get("o1")
            d.clear()
            if c == 0:
                st[u + 1]["o1"] = on
            else:
                o = o1 - lam * on
                r = lax.rsqrt(jnp.mean(o * o, axis=-1, keepdims=True) + EPS)
                o_ref[0, x * sub:(x + 1) * sub, :] = (o * r * g_ref[...] * (1.0 - lam_init)).astype(o_ref.dtype)

    nch = len(chunks)
    for u in range(2 * nblk + 1):
        for j in range(nch):
            if u < 2 * nblk:
                qk(u, j)
            if u >= 1:
                ev(u - 1, j)


def _diff_attn(zq, zkv_extra, lams, subln_g, lam_init, *, tq, sub, kchunk, side_casts=()):
    bsz, t_len, _ = zq.shape
    nh = DIFF_HEADS
    two = zkv_extra is not None
    n_all = t_len + (zkv_extra.shape[1] if two else 0)
    n_q = t_len // tq
    steps = bsz * nh * n_q
    step_map = lambda b, h, i: ((b * nh + h) * n_q + i, 0)
    side_specs = []
    for a in side_casts:
        assert a.shape[0] % (steps * ROW_CHUNK) == 0
        side_specs.append(pl.BlockSpec((a.shape[0] // steps, a.shape[1]), step_map))
    in_specs = side_specs + [pl.BlockSpec((1, tq, LANES), lambda b, h, i: (b, i, h)),
                pl.BlockSpec((1, t_len, LANES), lambda b, h, i: (b, 0, nh + h)),
                pl.BlockSpec((1, t_len, LANES), lambda b, h, i: (b, 0, 2 * nh + h))]
    args = [*side_casts, zq, zq, zq]
    if two:
        c_len = zkv_extra.shape[1]
        in_specs += [pl.BlockSpec((1, c_len, LANES), lambda b, h, i: (b, 0, nh + h)),
                     pl.BlockSpec((1, c_len, LANES), lambda b, h, i: (b, 0, 2 * nh + h))]
        args += [zkv_extra, zkv_extra]
    for v in lams:
        in_specs.append(pl.BlockSpec((1, DIFF_DH), lambda b, h, i: (0, 0)))
        args.append(v.reshape(1, DIFF_DH))
    in_specs.append(pl.BlockSpec((1, DIFF_DV), lambda b, h, i: (0, 0)))
    args.append(subln_g.reshape(1, DIFF_DV))
    out_specs = [pl.BlockSpec((1, tq, LANES), lambda b, h, i: (b, i, h))] + side_specs
    out_shape = [jax.ShapeDtypeStruct((bsz, t_len, nh * DIFF_DV), BF16)]
    out_shape += [jax.ShapeDtypeStruct(a.shape, BF16) for a in side_casts]
    out = pl.pallas_call(
        functools.partial(_attn_kernel, lam_init=lam_init, two_sources=two, sub=sub, kchunk=kchunk,
                          n_side=len(side_casts)),
        grid=(bsz, nh, n_q),
        in_specs=in_specs,
        out_specs=out_specs,
        out_shape=out_shape,
        scratch_shapes=[pltpu.VMEM((n_all, 2 * DIFF_DV), BF16), pltpu.VMEM((1, sub, n_all), F32),
                        pltpu.VMEM((1, sub, n_all), F32)],
        compiler_params=_params(("parallel", "parallel", "arbitrary")),
        name="diff_attn" + ("_lat" if two else "_ctx"),
    )(*args)
    return out if side_casts else out[0]


def _dwconv_phases(width):
    offs = [HALO - width // 2 + w for w in range(width)]
    return sorted({o % SUBLANES for o in offs} - {0}), max(offs) // SUBLANES * SUBLANES


def _dwconv_scratch(tt, width, nch):
    phases, reach = _dwconv_phases(width)
    return pltpu.VMEM((max(len(phases), 1), tt + reach, nch), F32)


def _dwconv_into(y_ref, ysh_ref, w_ref, b_ref, out_ref, *, tt, width, nch):
    phases, reach = _dwconv_phases(width)
    for idx, p in enumerate(phases):
        ysh_ref[idx] = y_ref[p:p + tt + reach, :]
    rc = 64
    for c0 in range(0, nch, LANES):
        for r0 in range(0, tt, rc):
            acc = jnp.broadcast_to(b_ref[:, c0:c0 + LANES], (rc, LANES))
            for w in range(width):
                off = HALO - width // 2 + w
                p, start = off % SUBLANES, r0 + off // SUBLANES * SUBLANES
                if p == 0:
                    tap = y_ref[start:start + rc, c0:c0 + LANES]
                else:
                    tap = ysh_ref[phases.index(p), start:start + rc, c0:c0 + LANES]
                acc = acc + tap * w_ref[w:w + 1, c0:c0 + LANES]
            out_ref[r0:r0 + rc, c0:c0 + LANES] = acc


def _convmod_kernel(ap_ref, a_ref, an_ref, gp_ref, g_ref, gn_ref, cw_ref, cb_ref, lng_ref, lnb_ref, o_ref,
                    y_ref, c_ref, ysh_ref, *, tt, nch):
    i = pl.program_id(1)
    n = pl.num_programs(1)

    def glu(a, g):
        return a.astype(F32) * _sigmoid(g.astype(F32))

    y_ref[0:HALO] = jnp.where(i > 0, glu(ap_ref[0], gp_ref[0]), 0.0)
    y_ref[HALO:HALO + tt] = glu(a_ref[0], g_ref[0])
    y_ref[HALO + tt:2 * HALO + tt] = jnp.where(i < n - 1, glu(an_ref[0], gn_ref[0]), 0.0)
    _dwconv_into(y_ref, ysh_ref, cw_ref, cb_ref, c_ref, tt=tt, width=CONV_W, nch=nch)
    c = c_ref[...]
    mu = jnp.mean(c, axis=-1, keepdims=True)
    xc = c - mu
    var = jnp.mean(xc * xc, axis=-1, keepdims=True)
    y = xc * lax.rsqrt(var + EPS) * lng_ref[...] + lnb_ref[...]
    o_ref[0] = (y * _sigmoid(y)).astype(o_ref.dtype)


def _halo_specs(tt, t_len, width, colblk):
    per = tt // HALO
    last = t_len // HALO - 1
    return (pl.BlockSpec((1, HALO, width), lambda b, i: (b, jnp.maximum(i * per - 1, 0), colblk)),
            pl.BlockSpec((1, tt, width), lambda b, i: (b, i, colblk)),
            pl.BlockSpec((1, HALO, width), lambda b, i: (b, jnp.minimum((i + 1) * per, last), colblk)))


def _conv_module(z, conv_w, conv_b, ln_g, ln_b, *, tt):
    bsz, t_len, n = z.shape
    nch = conv_w.shape[1]
    a_blk = (n - 2 * nch) // nch
    vec = lambda: pl.BlockSpec((1, nch), lambda b, i: (0, 0))
    return pl.pallas_call(
        functools.partial(_convmod_kernel, tt=tt, nch=nch),
        grid=(bsz, t_len // tt),
        in_specs=[*_halo_specs(tt, t_len, nch, a_blk), *_halo_specs(tt, t_len, nch, a_blk + 1),
                  pl.BlockSpec((CONV_W, nch), lambda b, i: (0, 0)), vec(), vec(), vec()],
        out_specs=pl.BlockSpec((1, tt, nch), lambda b, i: (b, i, 0)),
        out_shape=jax.ShapeDtypeStruct((bsz, t_len, nch), BF16),
        scratch_shapes=[pltpu.VMEM((tt + 2 * HALO, nch), F32), pltpu.VMEM((tt, nch), F32),
                        _dwconv_scratch(tt, CONV_W, nch)],
        compiler_params=_params(("parallel", "arbitrary")),
        name="conformer_conv",
    )(z, z, z, z, z, z, conv_w, conv_b.reshape(1, nch), ln_g.reshape(1, nch), ln_b.reshape(1, nch))


def _conv3_kernel(zp_ref, z_ref, zn_ref, cw_ref, cb_ref, o_ref, y_ref, c_ref, ysh_ref, *, tt, nch, kscale):
    i = pl.program_id(1)
    n = pl.num_programs(1)
    y_ref[0:HALO] = jnp.where(i > 0, zp_ref[0].astype(F32), 0.0)
    y_ref[HALO:HALO + tt] = z_ref[0].astype(F32)
    y_ref[HALO + tt:2 * HALO + tt] = jnp.where(i < n - 1, zn_ref[0].astype(F32), 0.0)
    _dwconv_into(y_ref, ysh_ref, cw_ref, cb_ref, c_ref, tt=tt, width=ML_CONV_W, nch=nch)
    c = c_ref[...]
    s = c * _sigmoid(c)
    half = nch // 2
    o_ref[0, :, 0:half] = s[:, 0:half].astype(o_ref.dtype)
    o_ref[0, :, half:nch] = (s[:, half:nch] * kscale).astype(o_ref.dtype)


def _conv3_silu(z, conv_w, conv_b, *, tt):
    bsz, t_len, _ = z.shape
    nch = conv_w.shape[1]
    return pl.pallas_call(
        functools.partial(_conv3_kernel, tt=tt, nch=nch, kscale=ML_DK ** -0.5),
        grid=(bsz, t_len // tt),
        in_specs=[*_halo_specs(tt, t_len, nch, 0),
                  pl.BlockSpec((ML_CONV_W, nch), lambda b, i: (0, 0)),
                  pl.BlockSpec((1, nch), lambda b, i: (0, 0))],
        out_specs=pl.BlockSpec((1, tt, nch), lambda b, i: (b, i, 0)),
        out_shape=jax.ShapeDtypeStruct((bsz, t_len, nch), BF16),
        scratch_shapes=[pltpu.VMEM((tt + 2 * HALO, nch), F32), pltpu.VMEM((tt, nch), F32),
                        _dwconv_scratch(tt, ML_CONV_W, nch)],
        compiler_params=_params(("parallel", "arbitrary")),
        name="ml_conv3_silu",
    )(z, z, z, conv_w, conv_b.reshape(1, nch))


def _post_residual(y, x, gate, gpost):
    r = lax.rsqrt(jnp.mean(y * y, axis=-1, keepdims=True) + EPS)
    return x + gate * (y * r * gpost)


OUT_SUB = 256


def _hy_out_kernel(a_ref, c_ref, x_ref, gate_ref, gpost_ref, w_ref, o_ref):
    ka = a_ref.shape[1]
    y = (jnp.dot(a_ref[...], w_ref[0:ka, :], preferred_element_type=F32)
         + jnp.dot(c_ref[...], w_ref[ka:, :], preferred_element_type=F32))
    o_ref[...] = _post_residual(y, x_ref[...], gate_ref[0], gpost_ref[...])


def _hy_out(attn, conv, x, gate, gpost, w, *, tm, rows_per_mod):
    m, d = x.shape
    ka, kc = attn.shape[1], conv.shape[1]
    return pl.pallas_call(
        _hy_out_kernel,
        grid=(m // tm,),
        in_specs=[pl.BlockSpec((tm, ka), lambda i: (i, 0)),
                  pl.BlockSpec((tm, kc), lambda i: (i, 0)),
                  pl.BlockSpec((tm, d), lambda i: (i, 0)),
                  pl.BlockSpec((1, 1, d), lambda i: ((i * tm) // rows_per_mod, 0, 0)),
                  pl.BlockSpec((1, d), lambda i: (0, 0)),
                  pl.BlockSpec((ka + kc, d), lambda i: (0, 0), pipeline_mode=pl.Buffered(1))],
        out_specs=pl.BlockSpec((tm, d), lambda i: (i, 0)),
        out_shape=jax.ShapeDtypeStruct((m, d), F32),
        compiler_params=_params(("parallel",)),
        name="hybrid_out_proj",
    )(attn, conv, x, gate, gpost.reshape(1, d), w)


def _ml_out_kernel(hf_ref, hb_ref, og_ref, x_ref, ng_ref, gate_ref, gpost_ref, w_ref, o_ref):
    def gated(rows):
        hsum = hf_ref[0, rows, :].astype(F32) + hb_ref[0, rows, :].astype(F32)
        parts = []
        for h in range(ML_HEADS):
            hh = hsum[:, h * ML_DV:(h + 1) * ML_DV]
            r = lax.rsqrt(jnp.mean(hh * hh, axis=-1, keepdims=True) + EPS)
            parts.append(hh * r * ng_ref[:, h * ML_DV:(h + 1) * ML_DV])
        hn = jnp.concatenate(parts, axis=1)
        return (hn * _sigmoid(og_ref[rows, :].astype(F32))).astype(BF16)

    subs = [slice(r0, r0 + OUT_SUB) for r0 in range(0, x_ref.shape[0], OUT_SUB)]
    a = gated(subs[0])
    y_prev = None
    for idx, rows in enumerate(subs):
        y = jnp.dot(a, w_ref[...], preferred_element_type=F32)
        if idx + 1 < len(subs):
            a = gated(subs[idx + 1])
        if y_prev is not None:
            prev = subs[idx - 1]
            o_ref[prev, :] = _post_residual(y_prev, x_ref[prev, :], gate_ref[0], gpost_ref[...])
        y_prev = y
    o_ref[subs[-1], :] = _post_residual(y_prev, x_ref[subs[-1], :], gate_ref[0], gpost_ref[...])


def _ml_out(h2, z, o_blk, x, norm_g, gate, gpost, w, *, tm, rows_per_mod):
    m, d = x.shape
    kv = h2.shape[2]
    return pl.pallas_call(
        _ml_out_kernel,
        grid=(m // tm,),
        in_specs=[pl.BlockSpec((1, tm, kv), lambda i: (0, i, 0)),
                  pl.BlockSpec((1, tm, kv), lambda i: (1, i, 0)),
                  pl.BlockSpec((tm, kv), lambda i: (i, o_blk)),
                  pl.BlockSpec((tm, d), lambda i: (i, 0)),
                  pl.BlockSpec((1, kv), lambda i: (0, 0)),
                  pl.BlockSpec((1, 1, d), lambda i: ((i * tm) // rows_per_mod, 0, 0)),
                  pl.BlockSpec((1, d), lambda i: (0, 0)),
                  pl.BlockSpec((kv, d), lambda i: (0, 0), pipeline_mode=pl.Buffered(1))],
        out_specs=pl.BlockSpec((tm, d), lambda i: (i, 0)),
        out_shape=jax.ShapeDtypeStruct((m, d), F32),
        compiler_params=_params(("parallel",)),
        name="mlstm_out_proj",
    )(h2, h2, z, x, norm_g.reshape(1, kv), gate, gpost.reshape(1, d), w)


def _ffn_kernel(x_ref, g_ref, sh_ref, sc_ref, gate_ref, gpost_ref, w1_ref, w2_ref, o_ref, h_ref, acc_ref):
    f = pl.program_id(1)
    nf = pl.num_programs(1)

    @pl.when(f == 0)
    def _():
        _prenorm_into(h_ref, x_ref, g_ref[...], sh_ref[0], sc_ref[0])
        acc_ref[...] = jnp.zeros_like(acc_ref)

    t = jnp.maximum(jnp.dot(h_ref[...], w1_ref[0], preferred_element_type=F32), 0.0)
    acc_ref[...] += jnp.dot((t * t).astype(BF16), w2_ref[0], preferred_element_type=F32)

    @pl.when(f == nf - 1)
    def _():
        for r0 in range(0, x_ref.shape[0], ROW_CHUNK):
            rows = slice(r0, r0 + ROW_CHUNK)
            o_ref[rows, :] = _post_residual(acc_ref[rows, :], x_ref[rows, :], gate_ref[0], gpost_ref[...])


def _ffn(x, g, shift, scale, gate, gpost, w1, w2, layer, *, tm, tf, rows_per_mod):
    m, d = x.shape
    dff = w1.shape[2]
    mod_map = lambda i, f: ((i * tm) // rows_per_mod, 0, 0)
    vec = lambda: pl.BlockSpec((1, d), lambda i, f: (0, 0))
    return pl.pallas_call(
        _ffn_kernel,
        grid=(m // tm, dff // tf),
        in_specs=[pl.BlockSpec((tm, d), lambda i, f: (i, 0)), vec(),
                  pl.BlockSpec((1, 1, d), mod_map), pl.BlockSpec((1, 1, d), mod_map),
                  pl.BlockSpec((1, 1, d), mod_map), vec(),
                  pl.BlockSpec((1, d, tf), lambda i, f: (layer, 0, f)),
                  pl.BlockSpec((1, tf, d), lambda i, f: (layer, f, 0))],
        out_specs=pl.BlockSpec((tm, d), lambda i, f: (i, 0)),
        out_shape=jax.ShapeDtypeStruct((m, d), F32),
        scratch_shapes=[pltpu.VMEM((tm, d), BF16), pltpu.VMEM((tm, d), F32)],
        compiler_params=_params(("parallel", "arbitrary")),
        name="ffn",
    )(x, g.reshape(1, d), shift, scale, gate, gpost.reshape(1, d), w1, w2)


def _scan_kernel(ql_ref, kl_ref, vl_ref, gil_ref, gfl_ref, qc_ref, kc_ref, vc_ref, gic_ref, gfc_ref,
                 bi_ref, bf_ref, o_ref, c_ref, m_ref):
    L, H, DK, DV = SCAN_CHUNK, ML_HEADS, ML_DK, ML_DV
    fwd = pl.program_id(0) == 0
    s = pl.program_id(2)

    @pl.when(s == 0)
    def _():
        c_ref[...] = jnp.zeros_like(c_ref)
        m_ref[...] = jnp.zeros_like(m_ref)

    is_ctx = s == 0
    q = jnp.where(is_ctx, qc_ref[0], ql_ref[0])
    k = jnp.where(is_ctx, kc_ref[0], kl_ref[0])
    v = jnp.where(is_ctx, vc_ref[0], vl_ref[0])
    ipre = jnp.where(is_ctx, gic_ref[0, 0], gil_ref[0, 0]) + bi_ref[0]
    fpre = jnp.where(is_ctx, gfc_ref[0, 0], gfl_ref[0, 0]) + bf_ref[0]
    lf = jnp.minimum(fpre, 0.0) - jnp.log1p(jnp.exp(-jnp.abs(fpre)))

    row = lax.broadcasted_iota(jnp.int32, (L, L), 0)
    col = lax.broadcasted_iota(jnp.int32, (L, L), 1)
    delta = (row - col) * jnp.where(fwd, 1, -1)
    tri = jnp.where(delta <= 0, 1.0, 0.0).astype(F32)
    b_all = jnp.dot(lf, tri, precision=lax.Precision.HIGHEST, preferred_element_type=F32)
    r_all = ipre - b_all
    b_last_all = jnp.where(fwd, b_all[:, L - 1:L], b_all[:, 0:1])
    mask = delta >= 0
    ones_blk = (lax.broadcasted_iota(jnp.int32, (L, LANES), 1) == 0).astype(BF16)
    dn_t = (((1,), (1,)), ((), ()))

    for h in range(H):
        b_row = b_all[h:h + 1, :]
        r_row = r_all[h:h + 1, :]
        bcol = jnp.transpose(jnp.broadcast_to(b_row, (LANES, L)))
        bcol_l = jnp.concatenate([bcol] * (L // LANES), axis=1)
        dm = jnp.where(mask, bcol_l + r_row, -jnp.inf)
        m_loc = jnp.max(dm, axis=1, keepdims=True)
        m_prev = m_ref[h:h + 1, 0:1]
        g = bcol[:, 0:1] + m_prev
        m_row = jnp.maximum(g, m_loc)
        inter = jnp.exp(g - m_row)
        p = jnp.exp(dm - m_row)
        qh = q[:, h * DK:(h + 1) * DK]
        kh = k[:, h * DK:(h + 1) * DK]
        vext = jnp.concatenate([v[:, h * DV:(h + 1) * DV], ones_blk], axis=1)
        sm = (lax.dot_general(qh, kh, dn_t, preferred_element_type=F32) * p).astype(BF16)
        cst = c_ref[h]
        nd = (inter * jnp.dot(qh, cst.astype(BF16), preferred_element_type=F32)
              + jnp.dot(sm, vext, preferred_element_type=F32))
        den = jnp.maximum(jnp.abs(nd[:, DV:DV + 1]), jnp.exp(-m_row))
        o_ref[0, 0, :, h * DV:(h + 1) * DV] = (nd[:, 0:DV] * (1.0 / den)).astype(o_ref.dtype)

        b_last = b_last_all[h:h + 1, :]
        wlog = b_last + r_row
        m_new = jnp.maximum(b_last + m_prev, jnp.max(wlog, axis=1, keepdims=True))
        decay = jnp.exp(b_last + m_prev - m_new)
        kt = jnp.transpose(kh.astype(F32))
        ktw = (kt * jnp.exp(wlog - m_new)).astype(BF16)
        c_ref[h] = decay * cst + jnp.dot(ktw, vext, preferred_element_type=F32)
        m_ref[h:h + 1, :] = jnp.broadcast_to(m_new, (1, LANES))


def _mlstm_scan(qk_l, z_l, gates_l, qk_c, z_c, gates_c, gate_b):
    L, H = SCAN_CHUNK, ML_HEADS
    bsz, t_len, _ = qk_l.shape
    assert qk_c.shape[1] == L
    n_lat = t_len // L
    hk, hv = H * ML_DK, H * ML_DV

    def lat(d, s):
        return jnp.where(d == 0, jnp.maximum(s - 1, 0), n_lat - jnp.maximum(s, 1))

    in_specs = [
        pl.BlockSpec((1, L, hk), lambda d, b, s: (b, lat(d, s), 0)),
        pl.BlockSpec((1, L, hk), lambda d, b, s: (b, lat(d, s), 1)),
        pl.BlockSpec((1, L, hv), lambda d, b, s: (b, lat(d, s), 1)),
        pl.BlockSpec((1, 1, H, L), lambda d, b, s: (b, 2 * d, 0, lat(d, s))),
        pl.BlockSpec((1, 1, H, L), lambda d, b, s: (b, 2 * d + 1, 0, lat(d, s))),
        pl.BlockSpec((1, L, hk), lambda d, b, s: (b, 0, 0)),
        pl.BlockSpec((1, L, hk), lambda d, b, s: (b, 0, 1)),
        pl.BlockSpec((1, L, hv), lambda d, b, s: (b, 0, 1)),
        pl.BlockSpec((1, 1, H, L), lambda d, b, s: (b, 2 * d, 0, 0)),
        pl.BlockSpec((1, 1, H, L), lambda d, b, s: (b, 2 * d + 1, 0, 0)),
        pl.BlockSpec((1, H, 1), lambda d, b, s: (2 * d, 0, 0)),
        pl.BlockSpec((1, H, 1), lambda d, b, s: (2 * d + 1, 0, 0)),
    ]
    return pl.pallas_call(
        _scan_kernel,
        grid=(2, bsz, n_lat + 1),
        in_specs=in_specs,
        out_specs=pl.BlockSpec((1, 1, L, hv), lambda d, b, s: (d, b, lat(d, s), 0)),
        out_shape=jax.ShapeDtypeStruct((2, bsz, t_len, hv), BF16),
        scratch_shapes=[pltpu.VMEM((H, ML_DK, ML_DV + LANES), F32), pltpu.VMEM((H, LANES), F32)],
        compiler_params=_params(("parallel", "parallel", "arbitrary")),
        name="mlstm_scan",
    )(qk_l, qk_l, z_l, gates_l, gates_l, qk_c, qk_c, z_c, gates_c, gates_c,
      gate_b.reshape(4, H, 1), gate_b.reshape(4, H, 1))


def kernel(x, c, ctx, c_ctx, ada_w, ada_b, g_pre_mix, g_post_mix, g_pre_ffn, g_post_ffn, ffn_w1, ffn_w2, hy_w_in, hy_w_out, diff_lq1, diff_lk1, diff_lq2, diff_lk2, diff_subln_g, conv_w, conv_b, conv_ln_g, conv_ln_b, ml_w_in, ml_conv_w, ml_conv_b, ml_gate_b, ml_norm_g, ml_w_out):
    bsz, t_len, d = x.shape
    c_len = ctx.shape[1]
    depth = ada_w.shape[0]
    assert depth == 2 and bsz <= 7 and c_len == SCAN_CHUNK
    ml = t_len * bsz
    mc = c_len * bsz

    cvec = jnp.concatenate([c, c_ctx[None, :], jnp.zeros((8 - bsz - 1, d), F32)], axis=0)
    mods = _ada(cvec, ada_w, ada_b)

    def mod_lat(l, k):
        return mods[l, :bsz, k * d:(k + 1) * d].reshape(bsz, 1, d)

    def mod_ctx(l, k):
        return mods[l, bsz:bsz + 1, k * d:(k + 1) * d].reshape(1, 1, d)

    xl = x.reshape(ml, d)
    xc = ctx.reshape(mc, d)

    l = 0
    lam_init = 0.8 - 0.6 * math.exp(-0.3 * l)
    w_in = hy_w_in[0].astype(BF16)
    n_in = w_in.shape[1]
    qk_w = 2 * DIFF_HEADS * 2 * DIFF_DH
    rope = _rope_tables(t_len)
    zl = _prenorm_mm(xl, g_pre_mix[l], mod_lat(l, 0), mod_lat(l, 1), w_in, n_in, **TILES["hy_in"],
                     rows_per_mod=t_len, out_dtype=BF16, rope=rope, rope_cols=qk_w, qscale_cols=qk_w // 2,
                     qscale=ATTN_QSCALE, name="hy_in_proj_lat")
    zc = _prenorm_mm(xc, g_pre_mix[l], mod_ctx(l, 0), mod_ctx(l, 1), w_in, n_in, tm=mc, tn=TILES["ctx_in_tn"],
                     rows_per_mod=mc, out_dtype=BF16, qscale_cols=qk_w // 2, qscale=ATTN_QSCALE,
                     name="hy_in_proj_ctx")
    zl3 = zl.reshape(bsz, t_len, n_in)
    zc3 = zc.reshape(bsz, c_len, n_in)
    lams = (diff_lq1[0], diff_lk1[0], diff_lq2[0], diff_lk2[0])
    dff = ffn_w1.shape[2]
    attn_l, w1, w2, w_out, w_ml_out, wm = _diff_attn(
        zl3, zc3, lams, diff_subln_g[0], lam_init, **TILES["attn"],
        side_casts=(ffn_w1.reshape(depth * d, dff), ffn_w2.reshape(depth * dff, d), hy_w_out[0], ml_w_out[0],
                    ml_w_in[0]))
    w1 = w1.reshape(depth, d, dff)
    w2 = w2.reshape(depth, dff, d)
    attn_c = _diff_attn(zc3, None, lams, diff_subln_g[0], lam_init, tq=c_len, sub=c_len, kchunk=c_len)
    conv_l = _conv_module(zl3, conv_w[0], conv_b[0], conv_ln_g[0], conv_ln_b[0], tt=TILES["conv_tt"])
    conv_c = _conv_module(zc3, conv_w[0], conv_b[0], conv_ln_g[0], conv_ln_b[0], tt=c_len)
    xl = _hy_out(attn_l.reshape(ml, -1), conv_l.reshape(ml, -1), xl, mod_lat(l, 2), g_post_mix[l], w_out,
                 tm=TILES["out_tm"], rows_per_mod=t_len)
    xc = _hy_out(attn_c.reshape(mc, -1), conv_c.reshape(mc, -1), xc, mod_ctx(l, 2), g_post_mix[l], w_out,
                 tm=TILES["out_tm"], rows_per_mod=mc)
    xl = _ffn(xl, g_pre_ffn[l], mod_lat(l, 3), mod_lat(l, 4), mod_lat(l, 5), g_post_ffn[l], w1, w2, l,
              **TILES["ffn"], rows_per_mod=t_len)
    xc = _ffn(xc, g_pre_ffn[l], mod_ctx(l, 3), mod_ctx(l, 4), mod_ctx(l, 5), g_post_ffn[l], w1, w2, l,
              **TILES["ffn"], rows_per_mod=mc)

    l = 1
    qkv_w = 2 * ML_HEADS * ML_DK + ML_HEADS * ML_DV
    n_gates = 4 * ML_HEADS
    w_main = jnp.concatenate([wm[:, :qkv_w], wm[:, qkv_w + n_gates:]], axis=1)
    w_gate = jnp.pad(wm[:, qkv_w:qkv_w + n_gates], ((0, 0), (0, LANES - n_gates)))
    n_main = w_main.shape[1]
    zl, gl = _prenorm_mm(xl, g_pre_mix[l], mod_lat(l, 0), mod_lat(l, 1), w_main, n_main, **TILES["ml_in"],
                         rows_per_mod=t_len, out_dtype=BF16, w_side=w_gate, name="ml_in_proj_lat")
    zc, gc = _prenorm_mm(xc, g_pre_mix[l], mod_ctx(l, 0), mod_ctx(l, 1), w_main, qkv_w, tm=mc, tn=TILES["ctx_ml_tn"],
                         rows_per_mod=mc, out_dtype=BF16, w_side=w_gate, name="ml_in_proj_ctx")
    zl3 = zl.reshape(bsz, t_len, n_main)
    zc3 = zc.reshape(bsz, c_len, qkv_w)
    qk_l = _conv3_silu(zl3, ml_conv_w[0], ml_conv_b[0], tt=TILES["conv_tt"])
    qk_c = _conv3_silu(zc3, ml_conv_w[0], ml_conv_b[0], tt=c_len)

    def gates_t(gm, n):
        return gm[:, :n_gates].reshape(bsz, n, 4, ML_HEADS).transpose(0, 2, 3, 1)

    h2 = _mlstm_scan(qk_l, zl3, gates_t(gl, t_len), qk_c, zc3, gates_t(gc, c_len), ml_gate_b[0])
    xl = _ml_out(h2.reshape(2, ml, -1), zl, qkv_w // (ML_HEADS * ML_DV), xl, ml_norm_g[0], mod_lat(l, 2),
                 g_post_mix[l], w_ml_out, tm=TILES["out_tm"], rows_per_mod=t_len)
    xl = _ffn(xl, g_pre_ffn[l], mod_lat(l, 3), mod_lat(l, 4), mod_lat(l, 5), g_post_ffn[l], w1, w2, l,
              **TILES["ffn"], rows_per_mod=t_len)
    return xl.reshape(bsz, t_len, d)
```

```python
import functools
import math

import jax
import jax.numpy as jnp
from jax import lax
from jax.experimental import pallas as pl
from jax.experimental.pallas import tpu as pltpu

F32 = jnp.float32
BF16 = jnp.bfloat16

EPS = 1e-6
ROPE_BASE = 10000.0
GRID_W = 64
LANES = 128
SUBLANES = 8
HALO = 16
DIFF_DH = 64
DIFF_HEADS = 8
DIFF_DV = 128
CONV_W = 31
ML_HEADS = 8
ML_DK = 128
ML_DV = 256
ML_CONV_W = 3
SCAN_CHUNK = 256
ATTN_QSCALE = DIFF_DH ** -0.5 * math.log2(math.e)
VMEM_LIMIT = 56 * 1024 * 1024
TILES = dict(
    hy_in=dict(tm=512, tn=2560), ml_in=dict(tm=1024, tn=2048), ctx_in_tn=1024, ctx_ml_tn=2048,
    attn=dict(tq=2048, sub=256, kchunk=2048), conv_tt=512, out_tm=512, ffn=dict(tm=512, tf=1024),
)


def _params(sem):
    return pltpu.CompilerParams(dimension_semantics=sem, vmem_limit_bytes=VMEM_LIMIT)


def _sigmoid(x):
    return 1.0 / (1.0 + jnp.exp(-x))


def _runtime_zero():
    return jnp.minimum(pl.program_id(0), 0)


def _ada_kernel(c_ref, w_ref, b_ref, o_ref):
    c = c_ref[...]
    s = (c * _sigmoid(c)).astype(BF16)
    o_ref[0] = jnp.dot(s, w_ref[0].astype(BF16), preferred_element_type=F32) + b_ref[0]


def _ada(cvec, ada_w, ada_b):
    depth, d, n = ada_w.shape
    tn = 1024
    return pl.pallas_call(
        _ada_kernel,
        grid=(depth, n // tn),
        in_specs=[pl.BlockSpec((8, d), lambda l, j: (0, 0)),
                  pl.BlockSpec((1, d, tn), lambda l, j: (l, 0, j)),
                  pl.BlockSpec((1, 1, tn), lambda l, j: (l, 0, j))],
        out_specs=pl.BlockSpec((1, 8, tn), lambda l, j: (l, 0, j)),
        out_shape=jax.ShapeDtypeStruct((depth, 8, n), F32),
        compiler_params=_params(("parallel", "parallel")),
        name="ada_ln",
    )(cvec, ada_w, ada_b.reshape(depth, 1, n))


ROW_CHUNK = 16


def _prenorm_into(h_ref, x_ref, g, shift, scale):
    gs = g * (1.0 + scale)
    for r0 in range(0, x_ref.shape[0], ROW_CHUNK):
        x = x_ref[r0:r0 + ROW_CHUNK, :]
        r = lax.rsqrt(jnp.mean(x * x, axis=-1, keepdims=True) + EPS)
        h_ref[r0:r0 + ROW_CHUNK, :] = (x * r * gs + shift).astype(h_ref.dtype)


def _prenorm_mm_kernel(*refs, rope_cols, tn, qscale_cols, qscale, has_side):
    x_ref, g_ref, sh_ref, sc_ref, w_ref = refs[:5]
    rest = list(refs[5:])
    if rope_cols:
        cos_ref, sina_ref, sinb_ref = rest[:3]
        rest = rest[3:]
    if has_side:
        ws_ref, o_ref, os_ref, h_ref = rest
    else:
        o_ref, h_ref = rest
    j = pl.program_id(1)

    @pl.when(j == 0)
    def _():
        _prenorm_into(h_ref, x_ref, g_ref[...], sh_ref[0], sc_ref[0])
        if has_side:
            os_ref[...] = jnp.dot(h_ref[...], ws_ref[...], preferred_element_type=F32)

    acc = jnp.dot(h_ref[...], w_ref[...], preferred_element_type=F32)
    if not (rope_cols or qscale_cols):
        o_ref[...] = acc.astype(o_ref.dtype)
        return
    for c in range(tn // LANES):
        col0 = j * tn + c * LANES
        xs = acc[:, c * LANES:(c + 1) * LANES]
        if qscale_cols:
            xs = xs * jnp.where(col0 < qscale_cols, qscale, 1.0)
        if rope_cols:
            sel = jnp.where(col0 < rope_cols, 0, 1)
            rot = pltpu.roll(xs, LANES - 16, 1) * sina_ref[sel] + pltpu.roll(xs, 16, 1) * sinb_ref[sel]
            xs = xs * cos_ref[sel] + rot
        o_ref[:, c * LANES:(c + 1) * LANES] = xs.astype(o_ref.dtype)


def _prenorm_mm(x, g, shift, scale, w, n_out, *, tm, tn, rows_per_mod, out_dtype, rope=None, rope_cols=0, qscale_cols=0,
                qscale=1.0, w_side=None, name):
    m, d = x.shape
    mod_map = lambda i, j: ((i * tm) // rows_per_mod, 0, 0)
    in_specs = [pl.BlockSpec((tm, d), lambda i, j: (i, 0)),
                pl.BlockSpec((1, d), lambda i, j: (0, 0)),
                pl.BlockSpec((1, 1, d), mod_map),
                pl.BlockSpec((1, 1, d), mod_map),
                pl.BlockSpec((d, tn), lambda i, j: (0, j))]
    args = [x, g.reshape(1, d), shift, scale, w]
    if rope is not None:
        assert rope_cols % LANES == 0 and qscale_cols % LANES == 0
        t_len = rope[0].shape[0]
        assert t_len % tm == 0
        nblk = t_len // tm
        for tab, ident in zip(rope, (1.0, 0.0, 0.0)):
            in_specs.append(pl.BlockSpec((2, tm, LANES), lambda i, j: (0, i % nblk, 0)))
            args.append(jnp.stack([tab, jnp.full_like(tab, ident)]))
    else:
        rope_cols = 0
    out_specs = pl.BlockSpec((tm, tn), lambda i, j: (i, j))
    out_shape = jax.ShapeDtypeStruct((m, n_out), out_dtype)
    if w_side is not None:
        ns = w_side.shape[1]
        in_specs.append(pl.BlockSpec((d, ns), lambda i, j: (0, 0)))
        args.append(w_side)
        out_specs = (out_specs, pl.BlockSpec((tm, ns), lambda i, j: (i, 0)))
        out_shape = (out_shape, jax.ShapeDtypeStruct((m, ns), F32))
    return pl.pallas_call(
        functools.partial(_prenorm_mm_kernel, rope_cols=rope_cols, tn=tn, qscale_cols=qscale_cols, qscale=qscale,
                          has_side=w_side is not None),
        grid=(m // tm, n_out // tn),
        in_specs=in_specs,
        out_specs=out_specs,
        out_shape=out_shape,
        scratch_shapes=[pltpu.VMEM((tm, d), BF16)],
        compiler_params=_params(("parallel", "arbitrary")),
        name=name,
    )(*args)


def _rope_tables(t_len):
    rows = t_len // GRID_W
    r = jnp.repeat(jnp.arange(rows, dtype=F32), GRID_W)
    col = jnp.tile(jnp.arange(GRID_W, dtype=F32), rows)
    n_freq = DIFF_DH // 4
    inv = ROPE_BASE ** (-jnp.arange(n_freq, dtype=F32) / n_freq)
    ar = r[:, None] * inv
    ac = col[:, None] * inv
    ang = jnp.concatenate([ar, ar, ac, ac], axis=-1)
    cos = jnp.tile(jnp.cos(ang), (1, LANES // DIFF_DH))
    sin = jnp.tile(jnp.sin(ang), (1, LANES // DIFF_DH))
    first = (jnp.arange(LANES) % (2 * n_freq)) < n_freq
    return cos, jnp.where(first, -sin, 0.0), jnp.where(first, 0.0, sin)


def _attn_kernel(*refs, lam_init, two_sources, sub, kchunk, n_side):
    side_in, refs = refs[:n_side], refs[n_side:]
    if two_sources:
        q_ref, k_ref, v_ref, kc_ref, vc_ref, lq1_ref, lk1_ref, lq2_ref, lk2_ref, g_ref, o_ref, *rest = refs
    else:
        q_ref, k_ref, v_ref, lq1_ref, lk1_ref, lq2_ref, lk2_ref, g_ref, o_ref, *rest = refs
    side_out, (vx_ref, *ss_refs) = rest[:n_side], rest[n_side:]
    for src_ref, dst_ref in zip(side_in, side_out):
        dst_ref[...] = src_ref[...].astype(dst_ref.dtype)
    z = _runtime_zero()
    lam = (jnp.exp(jnp.sum(lq1_ref[...] * lk1_ref[...], axis=-1, keepdims=True))
           - jnp.exp(jnp.sum(lq2_ref[...] * lk2_ref[...], axis=-1, keepdims=True)) + lam_init)
    dn = (((1,), (1,)), ((), ()))
    tq = q_ref.shape[1]
    n_keys = k_ref.shape[1]
    chunks = [(k_ref, c0, min(kchunk, n_keys - c0), c0) for c0 in range(0, n_keys, kchunk)]
    if two_sources:
        chunks.append((kc_ref, 0, kc_ref.shape[1], n_keys))

    @pl.when(pl.program_id(2) == 0)
    def _():
        srcs = [(v_ref, 0)] + ([(vc_ref, n_keys)] if two_sources else [])
        for vr, off in srcs:
            n = vr.shape[1]
            vx_ref[off:off + n, 0:DIFF_DV] = vr[0]
            vx_ref[off:off + n, DIFF_DV:2 * DIFF_DV] = (
                lax.broadcasted_iota(jnp.int32, (n, DIFF_DV), 1) == 0).astype(BF16)

    nblk = tq // sub
    st = [dict() for _ in range(2 * nblk)]

    def qk(u, j):
        x, c = divmod(u, 2)
        d = st[u]
        if j == 0:
            q = q_ref[0, x * sub:(x + 1) * sub, :]
            lane = lax.broadcasted_iota(jnp.int32, q.shape, 1)
            keep = (lane < DIFF_DH) if c == 0 else (lane >= DIFF_DH)
            d["q"] = jnp.where(keep, q, jnp.zeros_like(q))
        kr, c0, n, off = chunks[j]
        s = lax.dot_general(d["q"], kr[0, c0:c0 + n, :], dn, preferred_element_type=F32)
        ss_refs[u % 2][z, :, off:off + n] = s
        mj = jnp.max(s, axis=-1, keepdims=True)
        d["m"] = mj if j == 0 else jnp.maximum(d["m"], mj)

    def ev(u, j):
        x, c = divmod(u, 2)
        d = st[u]
        _, _, n, voff = chunks[j]
        p = jnp.exp2(ss_refs[u % 2][z, :, voff:voff + n] - d["m"]).astype(BF16)
        part = jnp.dot(p, vx_ref[voff:voff + n, :], preferred_element_type=F32)
        d["acc"] = part if j == 0 else d["acc"] + part
        if j == len(chunks) - 1:
            acc = d["acc"]
            on = acc[:, 0:DIFF_DV] * (1.0 / acc[:, DIFF_DV:DIFF_DV + 1])
            o1 = d.get("o1")
            d.clear()
            if c == 0:
                st[u + 1]["o1"] = on
            else:
                o = o1 - lam * on
                r = lax.rsqrt(jnp.mean(o * o, axis=-1, keepdims=True) + EPS)
                o_ref[0, x * sub:(x + 1) * sub, :] = (o * r * g_ref[...] * (1.0 - lam_init)).astype(o_ref.dtype)

    nch = len(chunks)
    for u in range(2 * nblk + 1):
        for j in range(nch):
            if u < 2 * nblk:
                qk(u, j)
            if u >= 1:
                ev(u - 1, j)


def _diff_attn(zq, zkv_extra, lams, subln_g, lam_init, *, tq, sub, kchunk, side_casts=()):
    bsz, t_len, _ = zq.shape
    nh = DIFF_HEADS
    two = zkv_extra is not None
    n_all = t_len + (zkv_extra.shape[1] if two else 0)
    n_q = t_len // tq
    steps = bsz * nh * n_q
    step_map = lambda b, h, i: ((b * nh + h) * n_q + i, 0)
    side_specs = []
    for a in side_casts:
        assert a.shape[0] % (steps * ROW_CHUNK) == 0
        side_specs.append(pl.BlockSpec((a.shape[0] // steps, a.shape[1]), step_map))
    in_specs = side_specs + [pl.BlockSpec((1, tq, LANES), lambda b, h, i: (b, i, h)),
                pl.BlockSpec((1, t_len, LANES), lambda b, h, i: (b, 0, nh + h)),
                pl.BlockSpec((1, t_len, LANES), lambda b, h, i: (b, 0, 2 * nh + h))]
    args = [*side_casts, zq, zq, zq]
    if two:
        c_len = zkv_extra.shape[1]
        in_specs += [pl.BlockSpec((1, c_len, LANES), lambda b, h, i: (b, 0, nh + h)),
                     pl.BlockSpec((1, c_len, LANES), lambda b, h, i: (b, 0, 2 * nh + h))]
        args += [zkv_extra, zkv_extra]
    for v in lams:
        in_specs.append(pl.BlockSpec((1, DIFF_DH), lambda b, h, i: (0, 0)))
        args.append(v.reshape(1, DIFF_DH))
    in_specs.append(pl.BlockSpec((1, DIFF_DV), lambda b, h, i: (0, 0)))
    args.append(subln_g.reshape(1, DIFF_DV))
    out_specs = [pl.BlockSpec((1, tq, LANES), lambda b, h, i: (b, i, h))] + side_specs
    out_shape = [jax.ShapeDtypeStruct((bsz, t_len, nh * DIFF_DV), BF16)]
    out_shape += [jax.ShapeDtypeStruct(a.shape, BF16) for a in side_casts]
    out = pl.pallas_call(
        functools.partial(_attn_kernel, lam_init=lam_init, two_sources=two, sub=sub, kchunk=kchunk,
                          n_side=len(side_casts)),
        grid=(bsz, nh, n_q),
        in_specs=in_specs,
        out_specs=out_specs,
        out_shape=out_shape,
        scratch_shapes=[pltpu.VMEM((n_all, 2 * DIFF_DV), BF16), pltpu.VMEM((1, sub, n_all), F32),
                        pltpu.VMEM((1, sub, n_all), F32)],
        compiler_params=_params(("parallel", "parallel", "arbitrary")),
        name="diff_attn" + ("_lat" if two else "_ctx"),
    )(*args)
    return out if side_casts else out[0]


def _dwconv_phases(width):
    offs = [HALO - width // 2 + w for w in range(width)]
    return sorted({o % SUBLANES for o in offs} - {0}), max(offs) // SUBLANES * SUBLANES


def _dwconv_scratch(tt, width, nch):
    phases, reach = _dwconv_phases(width)
    return pltpu.VMEM((max(len(phases), 1), tt + reach, nch), F32)


def _dwconv_into(y_ref, ysh_ref, w_ref, b_ref, out_ref, *, tt, width, nch):
    phases, reach = _dwconv_phases(width)
    for idx, p in enumerate(phases):
        ysh_ref[idx] = y_ref[p:p + tt + reach, :]
    rc = 64
    for c0 in range(0, nch, LANES):
        for r0 in range(0, tt, rc):
            acc = jnp.broadcast_to(b_ref[:, c0:c0 + LANES], (rc, LANES))
            for w in range(width):
                off = HALO - width // 2 + w
                p, start = off % SUBLANES, r0 + off // SUBLANES * SUBLANES
                if p == 0:
                    tap = y_ref[start:start + rc, c0:c0 + LANES]
                else:
                    tap = ysh_ref[phases.index(p), start:start + rc, c0:c0 + LANES]
                acc = acc + tap * w_ref[w:w + 1, c0:c0 + LANES]
            out_ref[r0:r0 + rc, c0:c0 + LANES] = acc


def _convmod_kernel(ap_ref, a_ref, an_ref, gp_ref, g_ref, gn_ref, cw_ref, cb_ref, lng_ref, lnb_ref, o_ref,
                    y_ref, c_ref, ysh_ref, *, tt, nch):
    i = pl.program_id(1)
    n = pl.num_programs(1)

    def glu(a, g):
        return a.astype(F32) * _sigmoid(g.astype(F32))

    y_ref[0:HALO] = jnp.where(i > 0, glu(ap_ref[0], gp_ref[0]), 0.0)
    y_ref[HALO:HALO + tt] = glu(a_ref[0], g_ref[0])
    y_ref[HALO + tt:2 * HALO + tt] = jnp.where(i < n - 1, glu(an_ref[0], gn_ref[0]), 0.0)
    _dwconv_into(y_ref, ysh_ref, cw_ref, cb_ref, c_ref, tt=tt, width=CONV_W, nch=nch)
    c = c_ref[...]
    mu = jnp.mean(c, axis=-1, keepdims=True)
    xc = c - mu
    var = jnp.mean(xc * xc, axis=-1, keepdims=True)
    y = xc * lax.rsqrt(var + EPS) * lng_ref[...] + lnb_ref[...]
    o_ref[0] = (y * _sigmoid(y)).astype(o_ref.dtype)


def _halo_specs(tt, t_len, width, colblk):
    per = tt // HALO
    last = t_len // HALO - 1
    return (pl.BlockSpec((1, HALO, width), lambda b, i: (b, jnp.maximum(i * per - 1, 0), colblk)),
            pl.BlockSpec((1, tt, width), lambda b, i: (b, i, colblk)),
            pl.BlockSpec((1, HALO, width), lambda b, i: (b, jnp.minimum((i + 1) * per, last), colblk)))


def _conv_module(z, conv_w, conv_b, ln_g, ln_b, *, tt):
    bsz, t_len, n = z.shape
    nch = conv_w.shape[1]
    a_blk = (n - 2 * nch) // nch
    vec = lambda: pl.BlockSpec((1, nch), lambda b, i: (0, 0))
    return pl.pallas_call(
        functools.partial(_convmod_kernel, tt=tt, nch=nch),
        grid=(bsz, t_len // tt),
        in_specs=[*_halo_specs(tt, t_len, nch, a_blk), *_halo_specs(tt, t_len, nch, a_blk + 1),
                  pl.BlockSpec((CONV_W, nch), lambda b, i: (0, 0)), vec(), vec(), vec()],
        out_specs=pl.BlockSpec((1, tt, nch), lambda b, i: (b, i, 0)),
        out_shape=jax.ShapeDtypeStruct((bsz, t_len, nch), BF16),
        scratch_shapes=[pltpu.VMEM((tt + 2 * HALO, nch), F32), pltpu.VMEM((tt, nch), F32),
                        _dwconv_scratch(tt, CONV_W, nch)],
        compiler_params=_params(("parallel", "arbitrary")),
        name="conformer_conv",
    )(z, z, z, z, z, z, conv_w, conv_b.reshape(1, nch), ln_g.reshape(1, nch), ln_b.reshape(1, nch))


def _conv3_kernel(zp_ref, z_ref, zn_ref, cw_ref, cb_ref, o_ref, y_ref, c_ref, ysh_ref, *, tt, nch, kscale):
    i = pl.program_id(1)
    n = pl.num_programs(1)
    y_ref[0:HALO] = jnp.where(i > 0, zp_ref[0].astype(F32), 0.0)
    y_ref[HALO:HALO + tt] = z_ref[0].astype(F32)
    y_ref[HALO + tt:2 * HALO + tt] = jnp.where(i < n - 1, zn_ref[0].astype(F32), 0.0)
    _dwconv_into(y_ref, ysh_ref, cw_ref, cb_ref, c_ref, tt=tt, width=ML_CONV_W, nch=nch)
    c = c_ref[...]
    s = c * _sigmoid(c)
    half = nch // 2
    o_ref[0, :, 0:half] = s[:, 0:half].astype(o_ref.dtype)
    o_ref[0, :, half:nch] = (s[:, half:nch] * kscale).astype(o_ref.dtype)


def _conv3_silu(z, conv_w, conv_b, *, tt):
    bsz, t_len, _ = z.shape
    nch = conv_w.shape[1]
    return pl.pallas_call(
        functools.partial(_conv3_kernel, tt=tt, nch=nch, kscale=ML_DK ** -0.5),
        grid=(bsz, t_len // tt),
        in_specs=[*_halo_specs(tt, t_len, nch, 0),
                  pl.BlockSpec((ML_CONV_W, nch), lambda b, i: (0, 0)),
                  pl.BlockSpec((1, nch), lambda b, i: (0, 0))],
        out_specs=pl.BlockSpec((1, tt, nch), lambda b, i: (b, i, 0)),
        out_shape=jax.ShapeDtypeStruct((bsz, t_len, nch), BF16),
        scratch_shapes=[pltpu.VMEM((tt + 2 * HALO, nch), F32), pltpu.VMEM((tt, nch), F32),
                        _dwconv_scratch(tt, ML_CONV_W, nch)],
        compiler_params=_params(("parallel", "arbitrary")),
        name="ml_conv3_silu",
    )(z, z, z, conv_w, conv_b.reshape(1, nch))


def _post_residual(y, x, gate, gpost):
    r = lax.rsqrt(jnp.mean(y * y, axis=-1, keepdims=True) + EPS)
    return x + gate * (y * r * gpost)


OUT_SUB = 256


def _hy_out_kernel(a_ref, c_ref, x_ref, gate_ref, gpost_ref, w_ref, o_ref):
    ka = a_ref.shape[1]
    y = (jnp.dot(a_ref[...], w_ref[0:ka, :], preferred_element_type=F32)
         + jnp.dot(c_ref[...], w_ref[ka:, :], preferred_element_type=F32))
    o_ref[...] = _post_residual(y, x_ref[...], gate_ref[0], gpost_ref[...])


def _hy_out(attn, conv, x, gate, gpost, w, *, tm, rows_per_mod):
    m, d = x.shape
    ka, kc = attn.shape[1], conv.shape[1]
    return pl.pallas_call(
        _hy_out_kernel,
        grid=(m // tm,),
        in_specs=[pl.BlockSpec((tm, ka), lambda i: (i, 0)),
                  pl.BlockSpec((tm, kc), lambda i: (i, 0)),
                  pl.BlockSpec((tm, d), lambda i: (i, 0)),
                  pl.BlockSpec((1, 1, d), lambda i: ((i * tm) // rows_per_mod, 0, 0)),
                  pl.BlockSpec((1, d), lambda i: (0, 0)),
                  pl.BlockSpec((ka + kc, d), lambda i: (0, 0), pipeline_mode=pl.Buffered(1))],
        out_specs=pl.BlockSpec((tm, d), lambda i: (i, 0)),
        out_shape=jax.ShapeDtypeStruct((m, d), F32),
        compiler_params=_params(("parallel",)),
        name="hybrid_out_proj",
    )(attn, conv, x, gate, gpost.reshape(1, d), w)


def _ml_out_kernel(hf_ref, hb_ref, og_ref, x_ref, ng_ref, gate_ref, gpost_ref, w_ref, o_ref):
    def gated(rows):
        hsum = hf_ref[0, rows, :].astype(F32) + hb_ref[0, rows, :].astype(F32)
        parts = []
        for h in range(ML_HEADS):
            hh = hsum[:, h * ML_DV:(h + 1) * ML_DV]
            r = lax.rsqrt(jnp.mean(hh * hh, axis=-1, keepdims=True) + EPS)
            parts.append(hh * r * ng_ref[:, h * ML_DV:(h + 1) * ML_DV])
        hn = jnp.concatenate(parts, axis=1)
        return (hn * _sigmoid(og_ref[rows, :].astype(F32))).astype(BF16)

    subs = [slice(r0, r0 + OUT_SUB) for r0 in range(0, x_ref.shape[0], OUT_SUB)]
    a = gated(subs[0])
    y_prev = None
    for idx, rows in enumerate(subs):
        y = jnp.dot(a, w_ref[...], preferred_element_type=F32)
        if idx + 1 < len(subs):
            a = gated(subs[idx + 1])
        if y_prev is not None:
            prev = subs[idx - 1]
            o_ref[prev, :] = _post_residual(y_prev, x_ref[prev, :], gate_ref[0], gpost_ref[...])
        y_prev = y
    o_ref[subs[-1], :] = _post_residual(y_prev, x_ref[subs[-1], :], gate_ref[0], gpost_ref[...])


def _ml_out(h2, z, o_blk, x, norm_g, gate, gpost, w, *, tm, rows_per_mod):
    m, d = x.shape
    kv = h2.shape[2]
    return pl.pallas_call(
        _ml_out_kernel,
        grid=(m // tm,),
        in_specs=[pl.BlockSpec((1, tm, kv), lambda i: (0, i, 0)),
                  pl.BlockSpec((1, tm, kv), lambda i: (1, i, 0)),
                  pl.BlockSpec((tm, kv), lambda i: (i, o_blk)),
                  pl.BlockSpec((tm, d), lambda i: (i, 0)),
                  pl.BlockSpec((1, kv), lambda i: (0, 0)),
                  pl.BlockSpec((1, 1, d), lambda i: ((i * tm) // rows_per_mod, 0, 0)),
                  pl.BlockSpec((1, d), lambda i: (0, 0)),
                  pl.BlockSpec((kv, d), lambda i: (0, 0), pipeline_mode=pl.Buffered(1))],
        out_specs=pl.BlockSpec((tm, d), lambda i: (i, 0)),
        out_shape=jax.ShapeDtypeStruct((m, d), F32),
        compiler_params=_params(("parallel",)),
        name="mlstm_out_proj",
    )(h2, h2, z, x, norm_g.reshape(1, kv), gate, gpost.reshape(1, d), w)


def _ffn_kernel(x_ref, g_ref, sh_ref, sc_ref, gate_ref, gpost_ref, w1_ref, w2_ref, o_ref, h_ref, acc_ref):
    f = pl.program_id(1)
    nf = pl.num_programs(1)

    @pl.when(f == 0)
    def _():
        _prenorm_into(h_ref, x_ref, g_ref[...], sh_ref[0], sc_ref[0])

    @pl.when((f == 0) & (pl.program_id(0) == 0))
    def _():
        acc_ref[...] = jnp.zeros_like(acc_ref)

    t = jnp.maximum(jnp.dot(h_ref[...], w1_ref[0], preferred_element_type=F32), 0.0)
    acc_ref[...] += jnp.dot((t * t).astype(BF16), w2_ref[0], preferred_element_type=F32)

    @pl.when(f == nf - 1)
    def _():
        for r0 in range(0, x_ref.shape[0], ROW_CHUNK):
            rows = slice(r0, r0 + ROW_CHUNK)
            o_ref[rows, :] = _post_residual(acc_ref[rows, :], x_ref[rows, :], gate_ref[0], gpost_ref[...])
            acc_ref[rows, :] = jnp.zeros((ROW_CHUNK, acc_ref.shape[1]), F32)


def _ffn(x, g, shift, scale, gate, gpost, w1, w2, layer, *, tm, tf, rows_per_mod):
    m, d = x.shape
    dff = w1.shape[2]
    mod_map = lambda i, f: ((i * tm) // rows_per_mod, 0, 0)
    vec = lambda: pl.BlockSpec((1, d), lambda i, f: (0, 0))
    return pl.pallas_call(
        _ffn_kernel,
        grid=(m // tm, dff // tf),
        in_specs=[pl.BlockSpec((tm, d), lambda i, f: (i, 0)), vec(),
                  pl.BlockSpec((1, 1, d), mod_map), pl.BlockSpec((1, 1, d), mod_map),
                  pl.BlockSpec((1, 1, d), mod_map), vec(),
                  pl.BlockSpec((1, d, tf), lambda i, f: (layer, 0, f)),
                  pl.BlockSpec((1, tf, d), lambda i, f: (layer, f, 0))],
        out_specs=pl.BlockSpec((tm, d), lambda i, f: (i, 0)),
        out_shape=jax.ShapeDtypeStruct((m, d), F32),
        scratch_shapes=[pltpu.VMEM((tm, d), BF16), pltpu.VMEM((tm, d), F32)],
        compiler_params=_params(("arbitrary", "arbitrary")),
        name="ffn",
    )(x, g.reshape(1, d), shift, scale, gate, gpost.reshape(1, d), w1, w2)


def _scan_kernel(ql_ref, kl_ref, vl_ref, gil_ref, gfl_ref, qc_ref, kc_ref, vc_ref, gic_ref, gfc_ref,
                 bi_ref, bf_ref, o_ref, c_ref, m_ref):
    L, H, DK, DV = SCAN_CHUNK, ML_HEADS, ML_DK, ML_DV
    fwd = pl.program_id(0) == 0
    s = pl.program_id(2)

    @pl.when(s == 0)
    def _():
        c_ref[...] = jnp.zeros_like(c_ref)
        m_ref[...] = jnp.zeros_like(m_ref)

    is_ctx = s == 0
    q = jnp.where(is_ctx, qc_ref[0], ql_ref[0])
    k = jnp.where(is_ctx, kc_ref[0], kl_ref[0])
    v = jnp.where(is_ctx, vc_ref[0], vl_ref[0])
    ipre = jnp.where(is_ctx, gic_ref[0, 0], gil_ref[0, 0]) + bi_ref[0]
    fpre = jnp.where(is_ctx, gfc_ref[0, 0], gfl_ref[0, 0]) + bf_ref[0]
    lf = jnp.minimum(fpre, 0.0) - jnp.log1p(jnp.exp(-jnp.abs(fpre)))

    row = lax.broadcasted_iota(jnp.int32, (L, L), 0)
    col = lax.broadcasted_iota(jnp.int32, (L, L), 1)
    delta = (row - col) * jnp.where(fwd, 1, -1)
    tri = jnp.where(delta <= 0, 1.0, 0.0).astype(F32)
    b_all = jnp.dot(lf, tri, precision=lax.Precision.HIGHEST, preferred_element_type=F32)
    r_all = ipre - b_all
    b_last_all = jnp.where(fwd, b_all[:, L - 1:L], b_all[:, 0:1])
    mask = delta >= 0
    ones_blk = (lax.broadcasted_iota(jnp.int32, (L, LANES), 1) == 0).astype(BF16)
    dn_t = (((1,), (1,)), ((), ()))

    for h in range(H):
        b_row = b_all[h:h + 1, :]
        r_row = r_all[h:h + 1, :]
        bcol = jnp.transpose(jnp.broadcast_to(b_row, (LANES, L)))
        bcol_l = jnp.concatenate([bcol] * (L // LANES), axis=1)
        dm = jnp.where(mask, bcol_l + r_row, -jnp.inf)
        m_loc = jnp.max(dm, axis=1, keepdims=True)
        m_prev = m_ref[h:h + 1, 0:1]
        g = bcol[:, 0:1] + m_prev
        m_row = jnp.maximum(g, m_loc)
        inter = jnp.exp(g - m_row)
        p = jnp.exp(dm - m_row)
        qh = q[:, h * DK:(h + 1) * DK]
        kh = k[:, h * DK:(h + 1) * DK]
        vext = jnp.concatenate([v[:, h * DV:(h + 1) * DV], ones_blk], axis=1)
        sm = (lax.dot_general(qh, kh, dn_t, preferred_element_type=F32) * p).astype(BF16)
        cst = c_ref[h]
        nd = (inter * jnp.dot(qh, cst.astype(BF16), preferred_element_type=F32)
              + jnp.dot(sm, vext, preferred_element_type=F32))
        den = jnp.maximum(jnp.abs(nd[:, DV:DV + 1]), jnp.exp(-m_row))
        o_ref[0, 0, :, h * DV:(h + 1) * DV] = (nd[:, 0:DV] * (1.0 / den)).astype(o_ref.dtype)

        b_last = b_last_all[h:h + 1, :]
        wlog = b_last + r_row
        m_new = jnp.maximum(b_last + m_prev, jnp.max(wlog, axis=1, keepdims=True))
        decay = jnp.exp(b_last + m_prev - m_new)
        kt = jnp.transpose(kh.astype(F32))
        ktw = (kt * jnp.exp(wlog - m_new)).astype(BF16)
        c_ref[h] = decay * cst + jnp.dot(ktw, vext, preferred_element_type=F32)
        m_ref[h:h + 1, :] = jnp.broadcast_to(m_new, (1, LANES))


def _mlstm_scan(qk_l, z_l, gates_l, qk_c, z_c, gates_c, gate_b):
    L, H = SCAN_CHUNK, ML_HEADS
    bsz, t_len, _ = qk_l.shape
    assert qk_c.shape[1] == L
    n_lat = t_len // L
    hk, hv = H * ML_DK, H * ML_DV

    def lat(d, s):
        return jnp.where(d == 0, jnp.maximum(s - 1, 0), n_lat - jnp.maximum(s, 1))

    in_specs = [
        pl.BlockSpec((1, L, hk), lambda d, b, s: (b, lat(d, s), 0)),
        pl.BlockSpec((1, L, hk), lambda d, b, s: (b, lat(d, s), 1)),
        pl.BlockSpec((1, L, hv), lambda d, b, s: (b, lat(d, s), 1)),
        pl.BlockSpec((1, 1, H, L), lambda d, b, s: (b, 2 * d, 0, lat(d, s))),
        pl.BlockSpec((1, 1, H, L), lambda d, b, s: (b, 2 * d + 1, 0, lat(d, s))),
        pl.BlockSpec((1, L, hk), lambda d, b, s: (b, 0, 0)),
        pl.BlockSpec((1, L, hk), lambda d, b, s: (b, 0, 1)),
        pl.BlockSpec((1, L, hv), lambda d, b, s: (b, 0, 1)),
        pl.BlockSpec((1, 1, H, L), lambda d, b, s: (b, 2 * d, 0, 0)),
        pl.BlockSpec((1, 1, H, L), lambda d, b, s: (b, 2 * d + 1, 0, 0)),
        pl.BlockSpec((1, H, 1), lambda d, b, s: (2 * d, 0, 0)),
        pl.BlockSpec((1, H, 1), lambda d, b, s: (2 * d + 1, 0, 0)),
    ]
    return pl.pallas_call(
        _scan_kernel,
        grid=(2, bsz, n_lat + 1),
        in_specs=in_specs,
        out_specs=pl.BlockSpec((1, 1, L, hv), lambda d, b, s: (d, b, lat(d, s), 0)),
        out_shape=jax.ShapeDtypeStruct((2, bsz, t_len, hv), BF16),
        scratch_shapes=[pltpu.VMEM((H, ML_DK, ML_DV + LANES), F32), pltpu.VMEM((H, LANES), F32)],
        compiler_params=_params(("parallel", "parallel", "arbitrary")),
        name="mlstm_scan",
    )(qk_l, qk_l, z_l, gates_l, gates_l, qk_c, qk_c, z_c, gates_c, gates_c,
      gate_b.reshape(4, H, 1), gate_b.reshape(4, H, 1))


def kernel(x, c, ctx, c_ctx, ada_w, ada_b, g_pre_mix, g_post_mix, g_pre_ffn, g_post_ffn, ffn_w1, ffn_w2, hy_w_in, hy_w_out, diff_lq1, diff_lk1, diff_lq2, diff_lk2, diff_subln_g, conv_w, conv_b, conv_ln_g, conv_ln_b, ml_w_in, ml_conv_w, ml_conv_b, ml_gate_b, ml_norm_g, ml_w_out):
    bsz, t_len, d = x.shape
    c_len = ctx.shape[1]
    depth = ada_w.shape[0]
    assert depth == 2 and bsz <= 7 and c_len == SCAN_CHUNK
    ml = t_len * bsz
    mc = c_len * bsz

    cvec = jnp.concatenate([c, c_ctx[None, :], jnp.zeros((8 - bsz - 1, d), F32)], axis=0)
    mods = _ada(cvec, ada_w, ada_b)

    def mod_lat(l, k):
        return mods[l, :bsz, k * d:(k + 1) * d].reshape(bsz, 1, d)

    def mod_ctx(l, k):
        return mods[l, bsz:bsz + 1, k * d:(k + 1) * d].reshape(1, 1, d)

    xl = x.reshape(ml, d)
    xc = ctx.reshape(mc, d)

    l = 0
    lam_init = 0.8 - 0.6 * math.exp(-0.3 * l)
    w_in = hy_w_in[0].astype(BF16)
    n_in = w_in.shape[1]
    qk_w = 2 * DIFF_HEADS * 2 * DIFF_DH
    rope = _rope_tables(t_len)
    zl = _prenorm_mm(xl, g_pre_mix[l], mod_lat(l, 0), mod_lat(l, 1), w_in, n_in, **TILES["hy_in"],
                     rows_per_mod=t_len, out_dtype=BF16, rope=rope, rope_cols=qk_w, qscale_cols=qk_w // 2,
                     qscale=ATTN_QSCALE, name="hy_in_proj_lat")
    zc = _prenorm_mm(xc, g_pre_mix[l], mod_ctx(l, 0), mod_ctx(l, 1), w_in, n_in, tm=mc, tn=TILES["ctx_in_tn"],
                     rows_per_mod=mc, out_dtype=BF16, qscale_cols=qk_w // 2, qscale=ATTN_QSCALE,
                     name="hy_in_proj_ctx")
    zl3 = zl.reshape(bsz, t_len, n_in)
    zc3 = zc.reshape(bsz, c_len, n_in)
    lams = (diff_lq1[0], diff_lk1[0], diff_lq2[0], diff_lk2[0])
    dff = ffn_w1.shape[2]
    attn_l, w1, w2, w_out, w_ml_out, wm = _diff_attn(
        zl3, zc3, lams, diff_subln_g[0], lam_init, **TILES["attn"],
        side_casts=(ffn_w1.reshape(depth * d, dff), ffn_w2.reshape(depth * dff, d), hy_w_out[0], ml_w_out[0],
                    ml_w_in[0]))
    w1 = w1.reshape(depth, d, dff)
    w2 = w2.reshape(depth, dff, d)
    attn_c = _diff_attn(zc3, None, lams, diff_subln_g[0], lam_init, tq=c_len, sub=c_len, kchunk=c_len)
    conv_l = _conv_module(zl3, conv_w[0], conv_b[0], conv_ln_g[0], conv_ln_b[0], tt=TILES["conv_tt"])
    conv_c = _conv_module(zc3, conv_w[0], conv_b[0], conv_ln_g[0], conv_ln_b[0], tt=c_len)
    xl = _hy_out(attn_l.reshape(ml, -1), conv_l.reshape(ml, -1), xl, mod_lat(l, 2), g_post_mix[l], w_out,
                 tm=TILES["out_tm"], rows_per_mod=t_len)
    xc = _hy_out(attn_c.reshape(mc, -1), conv_c.reshape(mc, -1), xc, mod_ctx(l, 2), g_post_mix[l], w_out,
                 tm=TILES["out_tm"], rows_per_mod=mc)
    xl = _ffn(xl, g_pre_ffn[l], mod_lat(l, 3), mod_lat(l, 4), mod_lat(l, 5), g_post_ffn[l], w1, w2, l,
              **TILES["ffn"], rows_per_mod=t_len)
    xc = _ffn(xc, g_pre_ffn[l], mod_ctx(l, 3), mod_ctx(l, 4), mod_ctx(l, 5), g_post_ffn[l], w1, w2, l,
              **TILES["ffn"], rows_per_mod=mc)

    l = 1
    qkv_w = 2 * ML_HEADS * ML_DK + ML_HEADS * ML_DV
    n_gates = 4 * ML_HEADS
    w_main = jnp.concatenate([wm[:, :qkv_w], wm[:, qkv_w + n_gates:]], axis=1)
    w_gate = jnp.pad(wm[:, qkv_w:qkv_w + n_gates], ((0, 0), (0, LANES - n_gates)))
    n_main = w_main.shape[1]
    zl, gl = _prenorm_mm(xl, g_pre_mix[l], mod_lat(l, 0), mod_lat(l, 1), w_main, n_main, **TILES["ml_in"],
                         rows_per_mod=t_len, out_dtype=BF16, w_side=w_gate, name="ml_in_proj_lat")
    zc, gc = _prenorm_mm(xc, g_pre_mix[l], mod_ctx(l, 0), mod_ctx(l, 1), w_main, qkv_w, tm=mc, tn=TILES["ctx_ml_tn"],
                         rows_per_mod=mc, out_dtype=BF16, w_side=w_gate, name="ml_in_proj_ctx")
    zl3 = zl.reshape(bsz, t_len, n_main)
    zc3 = zc.reshape(bsz, c_len, qkv_w)
    qk_l = _conv3_silu(zl3, ml_conv_w[0], ml_conv_b[0], tt=TILES["conv_tt"])
    qk_c = _conv3_silu(zc3, ml_conv_w[0], ml_conv_b[0], tt=c_len)

    def gates_t(gm, n):
        return gm[:, :n_gates].reshape(bsz, n, 4, ML_HEADS).transpose(0, 2, 3, 1)

    h2 = _mlstm_scan(qk_l, zl3, gates_t(gl, t_len), qk_c, zc3, gates_t(gc, c_len), ml_gate_b[0])
    xl = _ml_out(h2.reshape(2, ml, -1), zl, qkv_w // (ML_HEADS * ML_DV), xl, ml_norm_g[0], mod_lat(l, 2),
                 g_post_mix[l], w_ml_out, tm=TILES["out_tm"], rows_per_mod=t_len)
    xl = _ffn(xl, g_pre_ffn[l], mod_lat(l, 3), mod_lat(l, 4), mod_lat(l, 5), g_post_ffn[l], w1, w2, l,
              **TILES["ffn"], rows_per_mod=t_len)
    return xl.reshape(bsz, t_len, d)
```
